```python
import math
import jax, jax.numpy as jnp
from jax import lax
import numpy as np

D_MODEL = 4096
BATCH = 8
SEQ = 2048
DEPTH = 2
DEC_BATCH = 32
DEC_SEQ = 64
PAST_LEN = 4096

F32 = jnp.float32
CHUNK = 64
Q_BLOCK = 128
N_AB = (DEPTH + 1) // 2
N_C = DEPTH // 2
D_FF = 11008
NORM_EPS = 1e-6
MLA_HEADS = 16
Q_LORA = 1024
KV_LORA = 512
QK_NOPE = 128
QK_ROPE = 64
V_HEAD = 128
QK_HEAD = QK_NOPE + QK_ROPE
ROPE_THETA = 10000.0
MLA_WIDTH = MLA_HEADS * V_HEAD
ATTN_SCALE = QK_HEAD ** -0.5
NEG_INF = -1e30
S5_WIDTH = D_MODEL - MLA_WIDTH
S5_GROUP = 16
S5_GROUPS = S5_WIDTH // S5_GROUP
S5_STATE = 64
IN_COLS = Q_LORA + KV_LORA + QK_ROPE + S5_WIDTH
SPLITS = (Q_LORA, Q_LORA + KV_LORA, Q_LORA + KV_LORA + QK_ROPE)
POOL_WINDOWS = (2, 4, 8, 16)
POOL_GROUPS = 4
POOL_GC = D_MODEL // POOL_GROUPS
POOL_HIST = max(POOL_WINDOWS) - 1

kernel_name = 'hybrid_mla_s5_pool_stream_step'


def rms_norm(x, g):
    xf = x.astype(F32)
    y = xf * lax.rsqrt(jnp.mean(xf * xf, axis=-1, keepdims=True) + NORM_EPS)
    return (y * g.astype(F32)).astype(x.dtype)


def swiglu(h, w1, w3, w2):
    return (jax.nn.silu(h @ w1) * (h @ w3)) @ w2


def rope_tables(pos):
    inv = 1.0 / (ROPE_THETA ** (jnp.arange(0, QK_ROPE, 2, dtype=F32) / QK_ROPE))
    ang = pos.astype(F32)[:, None] * inv[None, :]
    return jnp.cos(ang), jnp.sin(ang)


def apply_rope(x, cos, sin):
    xf = x.astype(F32)
    x1, x2 = jnp.split(xf, 2, axis=-1)
    return jnp.concatenate([x1 * cos - x2 * sin, x2 * cos + x1 * sin], axis=-1).astype(x.dtype)


def mla_chunk_causal_attention(q_nope, q_rope, ckv, kr, w_uk, w_uv):
    b, l, h, _ = q_nope.shape
    k_nope = jnp.einsum('bsc,chd->bshd', ckv, w_uk)
    v = jnp.einsum('bsc,chd->bshd', ckv, w_uv)
    k_chunk = jnp.arange(l) // CHUNK
    n_blk = l // Q_BLOCK

    def attend_block(args):
        qn, qr, blk = args
        s = (jnp.einsum('bqhd,bshd->bhqs', qn, k_nope)
             + jnp.einsum('bqhr,bsr->bhqs', qr, kr)).astype(F32) * ATTN_SCALE
        q_chunk = (blk * Q_BLOCK + jnp.arange(Q_BLOCK)) // CHUNK
        s = jnp.where(k_chunk[None, :] <= q_chunk[:, None], s, NEG_INF)
        p = jax.nn.softmax(s, axis=-1).astype(v.dtype)
        return jnp.einsum('bhqs,bshd->bqhd', p, v)

    qn_b = q_nope.reshape(b, n_blk, Q_BLOCK, h, QK_NOPE).swapaxes(0, 1)
    qr_b = q_rope.reshape(b, n_blk, Q_BLOCK, h, QK_ROPE).swapaxes(0, 1)
    o = lax.map(attend_block, (qn_b, qr_b, jnp.arange(n_blk)))
    return o.swapaxes(0, 1).reshape(b, l, h * V_HEAD)


def mla_cached_attention(q_nope, q_rope, ckv_all, kr_all, w_uk, w_uv):
    b, l, h, _ = q_nope.shape
    q_lat = jnp.einsum('bqhd,chd->bqhc', q_nope, w_uk)
    s = (jnp.einsum('bqhc,bsc->bhqs', q_lat, ckv_all)
         + jnp.einsum('bqhr,bsr->bhqs', q_rope, kr_all)).astype(F32) * ATTN_SCALE
    p = jax.nn.softmax(s, axis=-1).astype(ckv_all.dtype)
    o_lat = jnp.einsum('bhqs,bsc->bqhc', p, ckv_all)
    return jnp.einsum('bqhc,chd->bqhd', o_lat, w_uv).reshape(b, l, h * V_HEAD)


def s5_mixer(u, h0, a_re, a_im, log_dt, b_re, b_im, c_re, c_im, d_skip, w_glu):
    bsz, l, _ = u.shape
    ug = u.astype(F32).reshape(bsz, l, S5_GROUPS, S5_GROUP)
    a = lax.complex(a_re.astype(F32), a_im.astype(F32))
    dt = jnp.exp(log_dt.astype(F32))[:, None]
    a_bar = jnp.exp(a * dt)
    b_bar = ((a_bar - 1.0) / a)[..., None] * lax.complex(b_re.astype(F32), b_im.astype(F32))
    bu = jnp.einsum('blgc,gpc->blgp', ug.astype(jnp.complex64), b_bar)
    if h0 is not None:
        bu = bu.at[:, 0].add(a_bar * h0)
    a_seq = jnp.broadcast_to(a_bar, (1, l) + a_bar.shape)

    def combine(left, right):
        a_l, b_l = left
        a_r, b_r = right
        return a_l * a_r, a_r * b_l + b_r

    _, states = lax.associative_scan(combine, (a_seq, bu), axis=1)
    c = lax.complex(c_re.astype(F32), c_im.astype(F32))
    y = (jnp.real(jnp.einsum('blgp,gcp->blgc', states, c))
         + d_skip.astype(F32).reshape(S5_GROUPS, S5_GROUP) * ug)
    z = jax.nn.gelu(y.reshape(bsz, l, S5_WIDTH)).astype(u.dtype)
    out = z * jax.nn.sigmoid(z @ w_glu)
    return out, states[:, -1]


def pool_mixer(h, hist, pos, w_pool, scale):
    bsz, l, d = h.shape
    hp = jnp.concatenate([hist, h], axis=1).astype(F32)
    cs = jnp.concatenate([jnp.zeros((bsz, 1, d), F32), jnp.cumsum(hp, axis=1)], axis=1)
    cur = hp[:, POOL_HIST:]
    parts = []
    for g, w in enumerate(POOL_WINDOWS):
        sl = slice(g * POOL_GC, (g + 1) * POOL_GC)
        wsum = (cs[:, POOL_HIST + 1:POOL_HIST + 1 + l, sl]
                - cs[:, POOL_HIST + 1 - w:POOL_HIST + 1 - w + l, sl])
        count = jnp.minimum(pos + 1, w).astype(F32)[None, :, None]
        parts.append(wsum / count - cur[..., sl])
    delta = jnp.concatenate(parts, axis=-1).reshape(bsz, l, POOL_GROUPS, POOL_GC).astype(h.dtype)
    out = jnp.einsum('blgc,gcd->blgd', delta, w_pool).reshape(bsz, l, d) * scale
    return out, hp[:, -POOL_HIST:].astype(h.dtype)


def trunk(x, pos0, prm, past):
    bsz, l, d = x.shape
    pos = pos0 + jnp.arange(l)
    cos, sin = rope_tables(pos)
    ckv_rows, kr_rows, s5_re, s5_im, pool_rows = [], [], [], [], []
    for layer in range(DEPTH):
        x = x + 0.5 * swiglu(rms_norm(x, prm['ffn1_g'][layer]), prm['ffn1_w1'][layer],
                             prm['ffn1_w3'][layer], prm['ffn1_w2'][layer])
        h = rms_norm(x, prm['mix_g'][layer])
        if layer % 2 == 0:
            i = layer // 2
            proj = h @ prm['ab_w_in'][i]
            c_q, c_kv, k_r, u = jnp.split(proj, SPLITS, axis=-1)
            q = jnp.einsum('blr,rhd->blhd', rms_norm(c_q, prm['mla_g_q'][i]), prm['mla_w_uq'][i])
            q_nope = q[..., :QK_NOPE]
            q_rope = apply_rope(q[..., QK_NOPE:], cos[:, None], sin[:, None])
            c_kv = rms_norm(c_kv, prm['mla_g_kv'][i])
            k_r = apply_rope(k_r, cos, sin)
            if past is None:
                o_a = mla_chunk_causal_attention(q_nope, q_rope, c_kv, k_r,
                                                 prm['mla_w_uk'][i], prm['mla_w_uv'][i])
                h0 = None
            else:
                o_a = mla_cached_attention(q_nope, q_rope,
                                           jnp.concatenate([past['ckv'][i], c_kv], axis=1),
                                           jnp.concatenate([past['krope'][i], k_r], axis=1),
                                           prm['mla_w_uk'][i], prm['mla_w_uv'][i])
                h0 = lax.complex(past['s5_re'][i].astype(F32), past['s5_im'][i].astype(F32))
            o_b, h_last = s5_mixer(u, h0, prm['s5_a_re'][i], prm['s5_a_im'][i], prm['s5_log_dt'][i],
                                   prm['s5_b_re'][i], prm['s5_b_im'][i], prm['s5_c_re'][i],
                                   prm['s5_c_im'][i], prm['s5_d'][i], prm['s5_w_glu'][i])
            x = x + jnp.concatenate([o_a, o_b], axis=-1) @ prm['ab_w_o'][i]
            ckv_rows.append(c_kv)
            kr_rows.append(k_r)
            s5_re.append(jnp.real(h_last).astype(x.dtype))
            s5_im.append(jnp.imag(h_last).astype(x.dtype))
        else:
            j = layer // 2
            hist = jnp.zeros((bsz, POOL_HIST, d), h.dtype) if past is None else past['pool'][j]
            o_c, tail = pool_mixer(h, hist, pos, prm['pool_w'][j], prm['pool_scale'][j])
            x = x + o_c
            pool_rows.append(tail)
        x = x + 0.5 * swiglu(rms_norm(x, prm['ffn2_g'][layer]), prm['ffn2_w1'][layer],
                             prm['ffn2_w3'][layer], prm['ffn2_w2'][layer])
    y = rms_norm(x, prm['final_g'])
    return (y, jnp.stack(ckv_rows), jnp.stack(kr_rows), jnp.stack(s5_re), jnp.stack(s5_im),
            jnp.stack(pool_rows))


def setup_inputs(seed: int = 0) -> dict:
    key = jax.random.key(seed)
    ks = iter(jax.random.split(key, 40))

    def nrm(shape, scale=1.0):
        return scale * jax.random.normal(next(ks), shape, F32)

    def gain(shape):
        return 1.0 + 0.02 * jax.random.normal(next(ks), shape, F32)

    n_idx = jnp.arange(S5_STATE, dtype=F32)
    return {
        'x_prompt': nrm((BATCH, SEQ, D_MODEL)),
        'x_sample': nrm((DEC_BATCH, DEC_SEQ, D_MODEL)),
        'cache_mla_ckv': nrm((N_AB, DEC_BATCH, PAST_LEN, KV_LORA)),
        'cache_mla_krope': nrm((N_AB, DEC_BATCH, PAST_LEN, QK_ROPE)),
        'state_s5_re': nrm((N_AB, DEC_BATCH, S5_GROUPS, S5_STATE), 0.1),
        'state_s5_im': nrm((N_AB, DEC_BATCH, S5_GROUPS, S5_STATE), 0.1),
        'cache_pool': nrm((N_C, DEC_BATCH, POOL_HIST, D_MODEL)),
        'ffn1_g': gain((DEPTH, D_MODEL)),
        'ffn1_w1': nrm((DEPTH, D_MODEL, D_FF), D_MODEL ** -0.5),
        'ffn1_w3': nrm((DEPTH, D_MODEL, D_FF), D_MODEL ** -0.5),
        'ffn1_w2': nrm((DEPTH, D_FF, D_MODEL), D_FF ** -0.5),
        'mix_g': gain((DEPTH, D_MODEL)),
        'ffn2_g': gain((DEPTH, D_MODEL)),
        'ffn2_w1': nrm((DEPTH, D_MODEL, D_FF), D_MODEL ** -0.5),
        'ffn2_w3': nrm((DEPTH, D_MODEL, D_FF), D_MODEL ** -0.5),
        'ffn2_w2': nrm((DEPTH, D_FF, D_MODEL), D_FF ** -0.5),
        'final_g': gain((D_MODEL,)),
        'ab_w_in': nrm((N_AB, D_MODEL, IN_COLS), D_MODEL ** -0.5),
        'mla_g_q': gain((N_AB, Q_LORA)),
        'mla_g_kv': gain((N_AB, KV_LORA)),
        'mla_w_uq': nrm((N_AB, Q_LORA, MLA_HEADS, QK_HEAD), Q_LORA ** -0.5),
        'mla_w_uk': nrm((N_AB, KV_LORA, MLA_HEADS, QK_NOPE), KV_LORA ** -0.5),
        'mla_w_uv': nrm((N_AB, KV_LORA, MLA_HEADS, V_HEAD), KV_LORA ** -0.5),
        's5_a_re': -0.5 + nrm((N_AB, S5_GROUPS, S5_STATE), 0.01),
        's5_a_im': jnp.pi * n_idx + nrm((N_AB, S5_GROUPS, S5_STATE), 0.01),
        's5_log_dt': jax.random.uniform(next(ks), (N_AB, S5_GROUPS), F32,
                                        math.log(1e-3), math.log(1e-1)),
        's5_b_re': nrm((N_AB, S5_GROUPS, S5_STATE, S5_GROUP), (2 * S5_GROUP) ** -0.5),
        's5_b_im': nrm((N_AB, S5_GROUPS, S5_STATE, S5_GROUP), (2 * S5_GROUP) ** -0.5),
        's5_c_re': nrm((N_AB, S5_GROUPS, S5_GROUP, S5_STATE), S5_STATE ** -0.5),
        's5_c_im': nrm((N_AB, S5_GROUPS, S5_GROUP, S5_STATE), S5_STATE ** -0.5),
        's5_d': nrm((N_AB, S5_WIDTH)),
        's5_w_glu': nrm((N_AB, S5_WIDTH, S5_WIDTH), S5_WIDTH ** -0.5),
        'ab_w_o': nrm((N_AB, D_MODEL, D_MODEL), D_MODEL ** -0.5),
        'pool_w': nrm((N_C, POOL_GROUPS, POOL_GC, POOL_GC), POOL_GC ** -0.5),
        'pool_scale': gain((N_C, D_MODEL)),
    }


def reference(x_prompt, x_sample, cache_mla_ckv, cache_mla_krope, state_s5_re, state_s5_im,
              cache_pool, ffn1_g, ffn1_w1, ffn1_w3, ffn1_w2, mix_g, ffn2_g, ffn2_w1, ffn2_w3,
              ffn2_w2, final_g, ab_w_in, mla_g_q, mla_g_kv, mla_w_uq, mla_w_uk, mla_w_uv,
              s5_a_re, s5_a_im, s5_log_dt, s5_b_re, s5_b_im, s5_c_re, s5_c_im, s5_d, s5_w_glu,
              ab_w_o, pool_w, pool_scale):
    prm = dict(ffn1_g=ffn1_g, ffn1_w1=ffn1_w1, ffn1_w3=ffn1_w3, ffn1_w2=ffn1_w2, mix_g=mix_g,
               ffn2_g=ffn2_g, ffn2_w1=ffn2_w1, ffn2_w3=ffn2_w3, ffn2_w2=ffn2_w2, final_g=final_g,
               ab_w_in=ab_w_in, mla_g_q=mla_g_q, mla_g_kv=mla_g_kv, mla_w_uq=mla_w_uq,
               mla_w_uk=mla_w_uk, mla_w_uv=mla_w_uv, s5_a_re=s5_a_re, s5_a_im=s5_a_im,
               s5_log_dt=s5_log_dt, s5_b_re=s5_b_re, s5_b_im=s5_b_im, s5_c_re=s5_c_re,
               s5_c_im=s5_c_im, s5_d=s5_d, s5_w_glu=s5_w_glu, ab_w_o=ab_w_o, pool_w=pool_w,
               pool_scale=pool_scale)
    past = dict(ckv=cache_mla_ckv, krope=cache_mla_krope, s5_re=state_s5_re, s5_im=state_s5_im,
                pool=cache_pool)
    y_prompt, p_ckv, p_krope, p_s5_re, p_s5_im, p_pool = trunk(x_prompt, 0, prm, None)
    past_len = cache_mla_ckv.shape[2]
    y_sample, s_ckv, s_krope, s_s5_re, s_s5_im, s_pool = trunk(x_sample, past_len, prm, past)
    return (y_prompt, y_sample, p_ckv, p_krope, p_s5_re, p_s5_im, p_pool,
            s_ckv, s_krope, s_s5_re, s_s5_im, s_pool)
```

```python
import functools
import math

import jax
import jax.numpy as jnp
from jax import lax
from jax.experimental import pallas as pl
from jax.experimental.pallas import tpu as pltpu

F32 = jnp.float32
BF16 = jnp.bfloat16
NORM_EPS = 1e-6
CHUNK = 64
ROPE_THETA = 10000.0
NEG_INF = -1e30
POOL_WINDOWS = (2, 4, 8, 16)
POOL_PAD = 16
LANE = 128
SUBLANE = 8
VMEM_LIMIT = 56 * 1024 * 1024


def _params(*sem):
    return pltpu.CompilerParams(dimension_semantics=sem, vmem_limit_bytes=VMEM_LIMIT)


def _pick(n, candidates):
    for c in candidates:
        if n % c == 0:
            return c
    return n


def _rms(x, g):
    y = x * lax.rsqrt(jnp.mean(x * x, axis=-1, keepdims=True) + NORM_EPS)
    return y * g


def _dot(a, b):
    return jnp.dot(a, b, preferred_element_type=F32)


def _dot_nt(a, b):
    return lax.dot_general(a, b, (((1,), (1,)), ((), ())), preferred_element_type=F32)


def _ffn_body(x_ref, g_ref, w1_ref, w3_ref, w2_ref, fg_ref, o_ref, h_ref, *, nf, final_norm):
    f = pl.program_id(1)

    @pl.when(f == 0)
    def _():
        h_ref[...] = _rms(x_ref[...], g_ref[...]).astype(BF16)
        o_ref[...] = jnp.zeros_like(o_ref)

    h = h_ref[...]
    a = _dot(h, w1_ref[...])
    b = _dot(h, w3_ref[...])
    act = (a * jax.nn.sigmoid(a) * b).astype(BF16)
    d = o_ref.shape[1]
    tc = _pick(d, (1024, 512, 256, 128))
    for c in range(0, d, tc):
        o_ref[:, c:c + tc] += _dot(act, w2_ref[:, c:c + tc])

    @pl.when(f == nf - 1)
    def _():
        y = x_ref[...] + 0.5 * o_ref[...]
        if final_norm:
            y = _rms(y, fg_ref[...])
        o_ref[...] = y


def _ffn(x, g, w1, w3, w2, fg, *, final_norm):
    t, d = x.shape
    dff = w1.shape[1]
    tm = _pick(t, (512, 256, 128, 64))
    tf = _pick(dff, (256, 128))
    nf = dff // tf
    body = functools.partial(_ffn_body, nf=nf, final_norm=final_norm)
    return pl.pallas_call(
        body,
        grid=(t // tm, nf),
        in_specs=[
            pl.BlockSpec((tm, d), lambda i, f: (i, 0), pipeline_mode=pl.Buffered(1)),
            pl.BlockSpec((1, d), lambda i, f: (0, 0)),
            pl.BlockSpec((d, tf), lambda i, f: (0, f)),
            pl.BlockSpec((d, tf), lambda i, f: (0, f)),
            pl.BlockSpec((tf, d), lambda i, f: (f, 0)),
            pl.BlockSpec((1, d), lambda i, f: (0, 0)),
        ],
        out_specs=pl.BlockSpec((tm, d), lambda i, f: (i, 0)),
        out_shape=jax.ShapeDtypeStruct((t, d), F32),
        scratch_shapes=[pltpu.VMEM((tm, d), BF16)],
        compiler_params=_params("parallel", "arbitrary"),
        name="ffn",
    )(x, g, w1, w3, w2, fg)


def _mm_body(*refs, has_gain, use_scratch, epilogue):
    refs = list(refs)
    lhs_ref = refs.pop(0)
    g_ref = refs.pop(0) if has_gain else None
    w_ref = refs.pop(0)
    e_ref = refs.pop(0) if epilogue is not None else None
    o_ref = refs.pop(0)
    if use_scratch:
        s_ref = refs.pop(0)

        @pl.when(pl.program_id(1) == 0)
        def _():
            v = lhs_ref[...]
            if has_gain:
                v = _rms(v.astype(F32), g_ref[...])
            s_ref[...] = v.astype(BF16)

        lhs = s_ref[...]
    else:
        lhs = lhs_ref[...]
    acc = _dot(lhs, w_ref[...])
    if epilogue == "glu":
        acc = e_ref[...] * jax.nn.sigmoid(acc)
    elif epilogue == "residual":
        acc = e_ref[...] + acc
    o_ref[...] = acc.astype(o_ref.dtype)


def _mm(lhs, w, *, out_dtype, gain=None, lhs_col=0, epilogue=None, extra=None, name):
    t = lhs.shape[0]
    k, n = w.shape
    tm = _pick(t, (512, 256, 128, 64))
    tn = _pick(n, (1024, 512, 256, 128))
    has_gain = gain is not None
    use_scratch = has_gain or lhs.dtype != BF16
    in_specs = [pl.BlockSpec((tm, k), lambda i, j: (i, lhs_col))]
    args = [lhs]
    if has_gain:
        in_specs.append(pl.BlockSpec((1, k), lambda i, j: (0, 0)))
        args.append(gain)
    in_specs.append(pl.BlockSpec((k, tn), lambda i, j: (0, j)))
    args.append(w)
    if epilogue is not None:
        in_specs.append(pl.BlockSpec((tm, tn), lambda i, j: (i, j)))
        args.append(extra)
    body = functools.partial(_mm_body, has_gain=has_gain, use_scratch=use_scratch, epilogue=epilogue)
    return pl.pallas_call(
        body,
        grid=(t // tm, n // tn),
        in_specs=in_specs,
        out_specs=pl.BlockSpec((tm, tn), lambda i, j: (i, j)),
        out_shape=jax.ShapeDtypeStruct((t, n), out_dtype),
        scratch_shapes=[pltpu.VMEM((tm, k), BF16)] if use_scratch else [],
        compiler_params=_params("parallel", "arbitrary"),
        name=name,
    )(*args)


def _kvpost_body(p_ref, g_ref, cos_ref, sin_ref, ckv_ref, kr_ref, *, kv, rope):
    ckv_ref[...] = _rms(p_ref[:, :kv], g_ref[...])
    k = p_ref[:, kv:kv + rope]
    k_rot = p_ref[:, kv + LANE:kv + LANE + rope]
    kr_ref[...] = k * cos_ref[:, :rope] + k_rot * sin_ref[:, :rope]


def _kvpost(proj, g_kv, cos, sin, *, q_lora, kv, rope):
    t = proj.shape[0]
    tm = _pick(t, (512, 256, 128, 64))
    body = functools.partial(_kvpost_body, kv=kv, rope=rope)
    return pl.pallas_call(
        body,
        grid=(t // tm,),
        in_specs=[
            pl.BlockSpec((tm, q_lora), lambda i: (i, 1)),
            pl.BlockSpec((1, kv), lambda i: (0, 0)),
            pl.BlockSpec((tm, LANE), lambda i: (i, 0)),
            pl.BlockSpec((tm, LANE), lambda i: (i, 0)),
        ],
        out_specs=[
            pl.BlockSpec((tm, kv), lambda i: (i, 0)),
            pl.BlockSpec((tm, rope), lambda i: (i, 0)),
        ],
        out_shape=[
            jax.ShapeDtypeStruct((t, kv), F32),
            jax.ShapeDtypeStruct((t, rope), F32),
        ],
        compiler_params=_params("parallel"),
        name="kv_post",
    )(proj, g_kv, cos, sin)


def _qproj_body(c_ref, g_ref, wq_ref, wrot_ref, cos_ref, sin_ref, o_ref, *, heads, nope):
    c = _rms(c_ref[...], g_ref[...]).astype(BF16)
    cos = cos_ref[...]
    sin = sin_ref[...]
    hw = nope + LANE
    for h in range(heads):
        main = _dot(c, wq_ref[:, h * hw:(h + 1) * hw])
        rot = _dot(c, wrot_ref[:, h * LANE:(h + 1) * LANE])
        o_ref[:, h * hw:h * hw + nope] = main[:, :nope].astype(BF16)
        o_ref[:, h * hw + nope:(h + 1) * hw] = (main[:, nope:] * cos + rot * sin).astype(BF16)


def _qproj(proj, g_q, wq, wrot, cos, sin, *, q_lora, heads, nope):
    t = proj.shape[0]
    tm = _pick(t, (512, 256, 128, 64))
    hw = nope + LANE
    body = functools.partial(_qproj_body, heads=heads, nope=nope)
    return pl.pallas_call(
        body,
        grid=(t // tm,),
        in_specs=[
            pl.BlockSpec((tm, q_lora), lambda i: (i, 0)),
            pl.BlockSpec((1, q_lora), lambda i: (0, 0)),
            pl.BlockSpec((q_lora, heads * hw), lambda i: (0, 0)),
            pl.BlockSpec((q_lora, heads * LANE), lambda i: (0, 0)),
            pl.BlockSpec((tm, LANE), lambda i: (i, 0)),
            pl.BlockSpec((tm, LANE), lambda i: (i, 0)),
        ],
        out_specs=pl.BlockSpec((tm, heads * hw), lambda i: (i, 0)),
        out_shape=jax.ShapeDtypeStruct((t, heads * hw), BF16),
        compiler_params=_params("parallel"),
        name="q_proj",
    )(proj, g_q, wq, wrot, cos, sin)


def _attn_prompt_body(q_ref, k_ref, v_ref, kr_ref, o_ref, *, tq, nope, rope, scale):
    qi = pl.program_id(2)
    q = q_ref[...]
    qn = q[:, :nope]
    qr = q[:, nope:nope + rope]
    vdim = v_ref.shape[1]
    q_chunk = (qi * tq + lax.broadcasted_iota(jnp.int32, (tq, tq), 0)) // CHUNK
    k_iota = lax.broadcasted_iota(jnp.int32, (tq, tq), 1)

    def body(kb, carry):
        m, l, acc = carry
        off = pl.multiple_of(kb * tq, tq)
        kn = k_ref[pl.ds(off, tq), :]
        kr = kr_ref[pl.ds(off, tq), :].astype(BF16)
        v = v_ref[pl.ds(off, tq), :]
        s = (_dot_nt(qn, kn) + _dot_nt(qr, kr)) * scale
        s = jnp.where((off + k_iota) // CHUNK <= q_chunk, s, NEG_INF)
        m_new = jnp.maximum(m, jnp.max(s, axis=-1, keepdims=True))
        alpha = jnp.exp(m - m_new)
        p = jnp.exp(s - m_new)
        l = alpha * l + jnp.sum(p, axis=-1, keepdims=True)
        acc = alpha * acc + _dot(p.astype(BF16), v)
        return m_new, l, acc

    init = (jnp.full((tq, 1), NEG_INF, F32), jnp.zeros((tq, 1), F32), jnp.zeros((tq, vdim), F32))
    m, l, acc = lax.fori_loop(0, qi + 1, body, init)
    o_ref[...] = (acc / l).astype(o_ref.dtype)


def _attn_prompt(q_cat, kv_exp, kr, *, batch, seq, heads, nope, rope, vdim, scale):
    tq = _pick(seq, (512, 256, 128, 64))
    nq = seq // tq
    hw = nope + LANE
    body = functools.partial(_attn_prompt_body, tq=tq, nope=nope, rope=rope, scale=scale)
    return pl.pallas_call(
        body,
        grid=(batch, heads, nq),
        in_specs=[
            pl.BlockSpec((tq, hw), lambda b, h, i: (b * nq + i, h)),
            pl.BlockSpec((seq, nope), lambda b, h, i: (b, h)),
            pl.BlockSpec((seq, vdim), lambda b, h, i: (b, heads * nope // vdim + h)),
            pl.BlockSpec((seq, rope), lambda b, h, i: (b, 0)),
        ],
        out_specs=pl.BlockSpec((tq, vdim), lambda b, h, i: (b * nq + i, h)),
        out_shape=jax.ShapeDtypeStruct((batch * seq, heads * vdim), BF16),
        compiler_params=_params("parallel", "parallel", "arbitrary"),
        name="attn_prompt",
    )(q_cat, kv_exp, kv_exp, kr)


def _attn_sample_body(q_ref, cc_ref, ck_ref, nc_ref, nk_ref, wuk_ref, wuv_ref, o_ref,
                      qlat_ref, qr_ref, m_ref, l_ref, acc_ref, *, heads, nope, rope, vdim, tk, scale):
    ls = q_ref.shape[0]
    past = cc_ref.shape[0]
    hw = nope + LANE
    for h in range(heads):
        qh = q_ref[:, h * hw:(h + 1) * hw]
        qlat_ref[h * ls:(h + 1) * ls, :] = _dot_nt(qh[:, :nope], wuk_ref[:, h * nope:(h + 1) * nope]).astype(BF16)
        qr_ref[h * ls:(h + 1) * ls, :] = qh[:, nope:nope + rope]
    m_ref[...] = jnp.full(m_ref.shape, NEG_INF, F32)
    l_ref[...] = jnp.zeros(l_ref.shape, F32)
    acc_ref[...] = jnp.zeros(acc_ref.shape, F32)

    def step(k, kr):
        s = (_dot_nt(qlat_ref[...], k) + _dot_nt(qr_ref[...], kr)) * scale
        m = m_ref[...]
        m_new = jnp.maximum(m, jnp.max(s, axis=-1, keepdims=True))
        alpha = jnp.exp(m - m_new)
        p = jnp.exp(s - m_new)
        l_ref[...] = alpha * l_ref[...] + jnp.sum(p, axis=-1, keepdims=True)
        acc_ref[...] = alpha * acc_ref[...] + _dot(p.astype(BF16), k)
        m_ref[...] = m_new

    def body(kb, carry):
        off = pl.multiple_of(kb * tk, tk)
        step(cc_ref[pl.ds(off, tk), :].astype(BF16), ck_ref[pl.ds(off, tk), :].astype(BF16))
        return carry

    lax.fori_loop(0, past // tk, body, 0)
    step(nc_ref[...].astype(BF16), nk_ref[...].astype(BF16))
    o_lat = (acc_ref[...] / l_ref[...]).astype(BF16)
    for h in range(heads):
        o_ref[:, h * vdim:(h + 1) * vdim] = _dot(
            o_lat[h * ls:(h + 1) * ls, :], wuv_ref[:, h * vdim:(h + 1) * vdim]).astype(o_ref.dtype)


def _attn_sample(q_cat, cache_ckv, cache_kr, ckv, kr, wuk, wuv, *, row0, batch, ls, heads, nope, rope,
                 vdim, scale):
    past, kvl = cache_ckv.shape[1], cache_ckv.shape[2]
    hw = nope + LANE
    tk = _pick(past, (512, 256, 128, 64))
    blk0 = row0 // ls
    body = functools.partial(_attn_sample_body, heads=heads, nope=nope, rope=rope, vdim=vdim, tk=tk,
                             scale=scale)
    return pl.pallas_call(
        body,
        grid=(batch,),
        in_specs=[
            pl.BlockSpec((ls, heads * hw), lambda b: (blk0 + b, 0)),
            pl.BlockSpec((None, past, kvl), lambda b: (b, 0, 0)),
            pl.BlockSpec((None, past, rope), lambda b: (b, 0, 0)),
            pl.BlockSpec((ls, kvl), lambda b: (blk0 + b, 0)),
            pl.BlockSpec((ls, rope), lambda b: (blk0 + b, 0)),
            pl.BlockSpec((kvl, heads * nope), lambda b: (0, 0)),
            pl.BlockSpec((kvl, heads * vdim), lambda b: (0, 0)),
        ],
        out_specs=pl.BlockSpec((ls, heads * vdim), lambda b: (b, 0)),
        out_shape=jax.ShapeDtypeStruct((batch * ls, heads * vdim), BF16),
        scratch_shapes=[
            pltpu.VMEM((heads * ls, kvl), BF16),
            pltpu.VMEM((heads * ls, rope), BF16),
            pltpu.VMEM((heads * ls, 1), F32),
            pltpu.VMEM((heads * ls, 1), F32),
            pltpu.VMEM((heads * ls, kvl), F32),
        ],
        compiler_params=_params("parallel"),
        name="attn_sample",
    )(q_cat, cache_ckv, cache_kr, ckv, kr, wuk, wuv)


S5_PAIRS = 4


def _s5_body(u_ref, h0r_ref, h0i_ref, ar_ref, ai_ref, wb_ref, wc_ref, d_ref,
             z_ref, sr_ref, si_ref, bu_ref, *, tl):
    t = pl.program_id(2)

    @pl.when(t == 0)
    def _():
        sr_ref[...] = h0r_ref[...]
        si_ref[...] = h0i_ref[...]

    u = u_ref[...].reshape(tl * SUBLANE, LANE)
    bu_ref[...] = _dot(u.astype(BF16), wb_ref[0])
    a_re = [jnp.broadcast_to(ar_ref[0, p:p + 1, :], (SUBLANE, LANE)) for p in range(S5_PAIRS)]
    a_im = [jnp.broadcast_to(ai_ref[0, p:p + 1, :], (SUBLANE, LANE)) for p in range(S5_PAIRS)]

    def step(l, carry):
        row = pl.multiple_of(l * SUBLANE, SUBLANE)
        out = []
        for p in range(S5_PAIRS):
            s_re, s_im = carry[2 * p], carry[2 * p + 1]
            c0 = 2 * p * LANE
            n_re = a_re[p] * s_re - a_im[p] * s_im + bu_ref[pl.ds(row, SUBLANE), c0:c0 + LANE]
            n_im = a_re[p] * s_im + a_im[p] * s_re + bu_ref[pl.ds(row, SUBLANE), c0 + LANE:c0 + 2 * LANE]
            bu_ref[pl.ds(row, SUBLANE), c0:c0 + LANE] = n_re
            bu_ref[pl.ds(row, SUBLANE), c0 + LANE:c0 + 2 * LANE] = n_im
            out += [n_re, n_im]
        return tuple(out)

    init = []
    for p in range(S5_PAIRS):
        init += [sr_ref[:, p * LANE:(p + 1) * LANE], si_ref[:, p * LANE:(p + 1) * LANE]]
    fin = lax.fori_loop(0, tl, step, tuple(init), unroll=4)
    for p in range(S5_PAIRS):
        sr_ref[:, p * LANE:(p + 1) * LANE] = fin[2 * p]
        si_ref[:, p * LANE:(p + 1) * LANE] = fin[2 * p + 1]
    y = _dot(bu_ref[...].astype(BF16), wc_ref[0]) + d_ref[...] * u
    z_ref[...] = jax.nn.gelu(y).reshape(tl, SUBLANE, LANE)


def _s5(u_tm, h0_re, h0_im, a_re, a_im, wb, wc, d_skip):
    l, b, w = u_tm.shape
    nblk = w // LANE
    sw = h0_re.shape[1] // nblk
    tl = _pick(l, (128, 64))
    body = functools.partial(_s5_body, tl=tl)
    return pl.pallas_call(
        body,
        grid=(b // SUBLANE, nblk, l // tl),
        in_specs=[
            pl.BlockSpec((tl, SUBLANE, LANE), lambda g, q, t: (t, g, q)),
            pl.BlockSpec((SUBLANE, sw), lambda g, q, t: (g, q)),
            pl.BlockSpec((SUBLANE, sw), lambda g, q, t: (g, q)),
            pl.BlockSpec((1, S5_PAIRS, LANE), lambda g, q, t: (q, 0, 0)),
            pl.BlockSpec((1, S5_PAIRS, LANE), lambda g, q, t: (q, 0, 0)),
            pl.BlockSpec((1, LANE, 2 * sw), lambda g, q, t: (q, 0, 0)),
            pl.BlockSpec((1, 2 * sw, LANE), lambda g, q, t: (q, 0, 0)),
            pl.BlockSpec((1, LANE), lambda g, q, t: (0, q)),
        ],
        out_specs=[
            pl.BlockSpec((tl, SUBLANE, LANE), lambda g, q, t: (t, g, q)),
            pl.BlockSpec((SUBLANE, sw), lambda g, q, t: (g, q)),
            pl.BlockSpec((SUBLANE, sw), lambda g, q, t: (g, q)),
        ],
        out_shape=[
            jax.ShapeDtypeStruct((l, b, w), F32),
            jax.ShapeDtypeStruct(h0_re.shape, F32),
            jax.ShapeDtypeStruct(h0_im.shape, F32),
        ],
        scratch_shapes=[pltpu.VMEM((tl * SUBLANE, 2 * sw), F32)],
        compiler_params=_params("parallel", "parallel", "arbitrary"),
        name="s5",
    )(u_tm, h0_re, h0_im, a_re, a_im, wb, wc, d_skip)


def _s5_weights(a_re, a_im, log_dt, b_re, b_im, c_re, c_im):
    g, p = a_re.shape
    grp = b_re.shape[2]
    gpb = LANE // grp
    nblk = g // gpb
    a = lax.complex(a_re.astype(F32), a_im.astype(F32))
    dt = jnp.exp(log_dt.astype(F32))[:, None]
    a_bar = jnp.exp(a * dt)
    b_bar = ((a_bar - 1.0) / a)[..., None] * lax.complex(b_re.astype(F32), b_im.astype(F32))
    eye = jnp.eye(gpb, dtype=F32)

    def blockdiag_in(m):
        m = m.reshape(nblk, gpb, p, grp)
        return jnp.einsum("qjpc,jk->qjckp", m, eye).reshape(nblk, gpb * grp, gpb * p)

    def blockdiag_out(m):
        m = m.reshape(nblk, gpb, grp, p)
        return jnp.einsum("qjcp,jk->qjpkc", m, eye).reshape(nblk, gpb * p, gpb * grp)

    def interleave(re, im, axis):
        shp = list(re.shape)
        n = shp[axis] // LANE
        shp[axis:axis + 1] = [n, LANE]
        st = jnp.stack([re.reshape(shp), im.reshape(shp)], axis=axis + 1)
        shp[axis:axis + 2] = [2 * n * LANE]
        return st.reshape(shp)

    wb = interleave(blockdiag_in(jnp.real(b_bar)), blockdiag_in(jnp.imag(b_bar)), 2).astype(BF16)
    wc = interleave(blockdiag_out(c_re.astype(F32)), blockdiag_out(-c_im.astype(F32)), 1).astype(BF16)
    pairs = g * p // LANE
    ar = jnp.real(a_bar).reshape(pairs // S5_PAIRS, S5_PAIRS, LANE)
    ai = jnp.imag(a_bar).reshape(pairs // S5_PAIRS, S5_PAIRS, LANE)
    return ar, ai, wb, wc


def _pool_body(x_ref, g_ref, hist_ref, w_ref, sc_ref, o_ref, tail_ref, ext_ref, *, tl, nt, pos0, gc):
    t = pl.program_id(1)

    @pl.when(t == 0)
    def _():
        ext_ref[0:POOL_PAD, :] = hist_ref[...]

    @pl.when(t > 0)
    def _():
        ext_ref[0:POOL_PAD, :] = ext_ref[tl:tl + POOL_PAD, :]

    x = x_ref[...]
    ext_ref[POOL_PAD:POOL_PAD + tl, :] = _rms(x, g_ref[...])
    pos = pos0 + t * tl + lax.broadcasted_iota(jnp.int32, (tl, 1), 0)
    for gi, win in enumerate(POOL_WINDOWS):
        cols = slice(gi * gc, (gi + 1) * gc)
        cur = ext_ref[POOL_PAD:POOL_PAD + tl, cols]
        wsum = cur
        for k in range(1, win):
            wsum = wsum + ext_ref[POOL_PAD - k:POOL_PAD - k + tl, cols]
        count = jnp.minimum(pos + 1, win).astype(F32)
        delta = (wsum / count - cur).astype(BF16)
        o_ref[:, cols] = x[:, cols] + _dot(delta, w_ref[gi]) * sc_ref[:, cols]

    @pl.when(t == nt - 1)
    def _():
        tail_ref[...] = ext_ref[tl:tl + POOL_PAD, :]


def _pool(x, g, hist, w, scale, *, row0, batch, seq, pos0):
    t, d = x.shape
    gc = d // len(POOL_WINDOWS)
    tl = _pick(seq, (256, 128, 64))
    nt = seq // tl
    blk0 = row0 // tl
    body = functools.partial(_pool_body, tl=tl, nt=nt, pos0=pos0, gc=gc)
    return pl.pallas_call(
        body,
        grid=(batch, nt),
        in_specs=[
            pl.BlockSpec((tl, d), lambda b, i: (blk0 + b * nt + i, 0)),
            pl.BlockSpec((1, d), lambda b, i: (0, 0)),
            pl.BlockSpec((None, POOL_PAD, d), lambda b, i: (b, 0, 0)),
            pl.BlockSpec(w.shape, lambda b, i: (0, 0, 0)),
            pl.BlockSpec((1, d), lambda b, i: (0, 0)),
        ],
        out_specs=[
            pl.BlockSpec((tl, d), lambda b, i: (blk0 + b * nt + i, 0)),
            pl.BlockSpec((None, POOL_PAD, d), lambda b, i: (b, 0, 0)),
        ],
        out_shape=[
            jax.ShapeDtypeStruct((t, d), F32),
            jax.ShapeDtypeStruct((batch, POOL_PAD, d), F32),
        ],
        scratch_shapes=[pltpu.VMEM((POOL_PAD + tl, d), F32)],
        input_output_aliases={0: 0},
        compiler_params=_params("parallel", "arbitrary"),
        name="pool",
    )(x, g, hist, w, scale)


def _rot_cols(w):
    half = w.shape[-1] // 2
    return jnp.concatenate([-w[..., half:], w[..., :half]], axis=-1)


def _pad_cols(w, n):
    return jnp.pad(w, [(0, 0)] * (w.ndim - 1) + [(0, n - w.shape[-1])])


def kernel(x_prompt, x_sample, cache_mla_ckv, cache_mla_krope, state_s5_re, state_s5_im, cache_pool, ffn1_g, ffn1_w1, ffn1_w3, ffn1_w2, mix_g, ffn2_g, ffn2_w1, ffn2_w3, ffn2_w2, final_g, ab_w_in, mla_g_q, mla_g_kv, mla_w_uq, mla_w_uk, mla_w_uv, s5_a_re, s5_a_im, s5_log_dt, s5_b_re, s5_b_im, s5_c_re, s5_c_im, s5_d, s5_w_glu, ab_w_o, pool_w, pool_scale):
    bp, seq, d = x_prompt.shape
    bs, ls, _ = x_sample.shape
    tp, ts = bp * seq, bs * ls
    depth = ffn1_g.shape[0]
    past = cache_mla_ckv.shape[2]
    q_lora, heads, qk_head = mla_w_uq.shape[1:]
    kvl, _, nope = mla_w_uk.shape[1:]
    vdim = mla_w_uv.shape[3]
    rope = qk_head - nope
    s5w = s5_d.shape[1]
    groups, nstate = s5_a_re.shape[1:]
    hist_rows = cache_pool.shape[2]
    scale = qk_head ** -0.5
    assert nope == LANE and vdim == LANE and 2 * rope == LANE and kvl + 2 * LANE <= q_lora
    assert bp % SUBLANE == 0 and bs % SUBLANE == 0 and hist_rows == POOL_PAD - 1

    x = jnp.concatenate([x_prompt.reshape(tp, d), x_sample.reshape(ts, d)], axis=0)

    pos = jnp.concatenate([jnp.tile(jnp.arange(seq), bp), past + jnp.tile(jnp.arange(ls), bs)])
    inv = 1.0 / (ROPE_THETA ** (jnp.arange(0, rope, 2, dtype=F32) / rope))
    ang = pos.astype(F32)[:, None] * inv[None, :]
    cos = _pad_cols(jnp.tile(jnp.cos(ang), (1, 2)), LANE)
    sin = _pad_cols(jnp.tile(jnp.sin(ang), (1, 2)), LANE)

    row = lambda v: v.reshape(1, -1).astype(F32)
    ckv_out, kr_out, s5re_out, s5im_out, pool_out = [], [], [], [], []
    for layer in range(depth):
        x = _ffn(x, row(ffn1_g[layer]), ffn1_w1[layer].astype(BF16), ffn1_w3[layer].astype(BF16),
                 ffn1_w2[layer].astype(BF16), row(final_g), final_norm=False)
        if layer % 2 == 0:
            i = layer // 2
            w_in = ab_w_in[i]
            w_kr = w_in[:, q_lora + kvl:q_lora + kvl + rope]
            w_mid = jnp.concatenate([w_in[:, q_lora:q_lora + kvl], _pad_cols(w_kr, LANE),
                                     _pad_cols(_rot_cols(w_kr), LANE)], axis=1)
            w_proj = jnp.concatenate([w_in[:, :q_lora], _pad_cols(w_mid, q_lora),
                                      w_in[:, q_lora + kvl + rope:]], axis=1).astype(BF16)
            proj = _mm(x, w_proj, out_dtype=F32, gain=row(mix_g[layer]), name="in_proj")
            ckv, kr = _kvpost(proj, row(mla_g_kv[i]), cos, sin, q_lora=q_lora, kv=kvl, rope=rope)

            w_uq = mla_w_uq[i]
            wq = jnp.concatenate([w_uq[..., :nope], _pad_cols(w_uq[..., nope:], LANE)], axis=-1)
            wrot = _pad_cols(_rot_cols(w_uq[..., nope:]), LANE)
            q_cat = _qproj(proj, row(mla_g_q[i]), wq.reshape(q_lora, -1).astype(BF16),
                           wrot.reshape(q_lora, -1).astype(BF16), cos, sin,
                           q_lora=q_lora, heads=heads, nope=nope)

            wuk = mla_w_uk[i].reshape(kvl, heads * nope).astype(BF16)
            wuv = mla_w_uv[i].reshape(kvl, heads * vdim).astype(BF16)
            kv_exp = _mm(ckv[:tp], jnp.concatenate([wuk, wuv], axis=1), out_dtype=BF16, name="kv_expand")
            o_p = _attn_prompt(q_cat, kv_exp, kr, batch=bp, seq=seq, heads=heads, nope=nope, rope=rope,
                               vdim=vdim, scale=scale)
            o_s = _attn_sample(q_cat, cache_mla_ckv[i], cache_mla_krope[i], ckv, kr, wuk, wuv, row0=tp,
                               batch=bs, ls=ls, heads=heads, nope=nope, rope=rope, vdim=vdim, scale=scale)

            ar, ai, wb, wc = _s5_weights(s5_a_re[i], s5_a_im[i], s5_log_dt[i], s5_b_re[i], s5_b_im[i],
                                         s5_c_re[i], s5_c_im[i])
            u = proj[:, 2 * q_lora:]
            u_p = u[:tp].reshape(bp, seq, s5w).swapaxes(0, 1)
            u_s = u[tp:].reshape(bs, ls, s5w).swapaxes(0, 1)
            zeros = jnp.zeros((bp, groups * nstate), F32)
            d_row = row(s5_d[i])
            z_p, pre, pim = _s5(u_p, zeros, zeros, ar, ai, wb, wc, d_row)
            z_s, sre, sim = _s5(u_s, state_s5_re[i].reshape(bs, -1).astype(F32),
                                state_s5_im[i].reshape(bs, -1).astype(F32), ar, ai, wb, wc, d_row)
            wglu = s5_w_glu[i].astype(BF16)
            z_p = z_p.reshape(tp, s5w)
            z_s = z_s.reshape(ts, s5w)
            ob_p = _mm(z_p, wglu, out_dtype=BF16, epilogue="glu", extra=z_p, name="s5_glu")
            ob_s = _mm(z_s, wglu, out_dtype=BF16, epilogue="glu", extra=z_s, name="s5_glu")
            ob_p = ob_p.reshape(seq, bp, s5w).swapaxes(0, 1).reshape(tp, s5w)
            ob_s = ob_s.reshape(ls, bs, s5w).swapaxes(0, 1).reshape(ts, s5w)
            o_cat = jnp.concatenate([jnp.concatenate([o_p, o_s], axis=0),
                                     jnp.concatenate([ob_p, ob_s], axis=0)], axis=1)
            x = _mm(o_cat, ab_w_o[i].astype(BF16), out_dtype=F32, epilogue="residual", extra=x, name="out_proj")

            ckv_out.append((ckv[:tp].reshape(bp, seq, kvl), ckv[tp:].reshape(bs, ls, kvl)))
            kr_out.append((kr[:tp].reshape(bp, seq, rope), kr[tp:].reshape(bs, ls, rope)))
            s5re_out.append((pre.reshape(bp, groups, nstate), sre.reshape(bs, groups, nstate)))
            s5im_out.append((pim.reshape(bp, groups, nstate), sim.reshape(bs, groups, nstate)))
        else:
            j = layer // 2
            wp = pool_w[j].astype(BF16)
            g_mix, sc = row(mix_g[layer]), row(pool_scale[j])
            hist_p = jnp.zeros((bp, POOL_PAD, d), F32)
            hist_s = jnp.pad(cache_pool[j].astype(F32), ((0, 0), (1, 0), (0, 0)))
            x, tail_p = _pool(x, g_mix, hist_p, wp, sc, row0=0, batch=bp, seq=seq, pos0=0)
            x, tail_s = _pool(x, g_mix, hist_s, wp, sc, row0=tp, batch=bs, seq=ls, pos0=past)
            pool_out.append((tail_p[:, 1:], tail_s[:, 1:]))
        x = _ffn(x, row(ffn2_g[layer]), ffn2_w1[layer].astype(BF16), ffn2_w3[layer].astype(BF16),
                 ffn2_w2[layer].astype(BF16), row(final_g), final_norm=(layer == depth - 1))

    stack = lambda items, k: jnp.stack([it[k] for it in items])
    return (x[:tp].reshape(bp, seq, d), x[tp:].reshape(bs, ls, d),
            stack(ckv_out, 0), stack(kr_out, 0), stack(s5re_out, 0), stack(s5im_out, 0), stack(pool_out, 0),
            stack(ckv_out, 1), stack(kr_out, 1), stack(s5re_out, 1), stack(s5im_out, 1), stack(pool_out, 1))
```

```python
import functools
import math

import jax
import jax.numpy as jnp
from jax import lax
from jax.experimental import pallas as pl
from jax.experimental.pallas import tpu as pltpu

F32 = jnp.float32
BF16 = jnp.bfloat16
NORM_EPS = 1e-6
CHUNK = 64
ROPE_THETA = 10000.0
NEG_INF = -1e30
POOL_WINDOWS = (2, 4, 8, 16)
POOL_PAD = 16
LANE = 128
SUBLANE = 8
VMEM_LIMIT = 56 * 1024 * 1024


def _params(*sem):
    return pltpu.CompilerParams(dimension_semantics=sem, vmem_limit_bytes=VMEM_LIMIT)


def _pick(n, candidates):
    for c in candidates:
        if n % c == 0:
            return c
    return n


def _rms(x, g):
    y = x * lax.rsqrt(jnp.mean(x * x, axis=-1, keepdims=True) + NORM_EPS)
    return y * g


def _dot(a, b):
    return jnp.dot(a, b, preferred_element_type=F32)


def _dot_nt(a, b):
    return lax.dot_general(a, b, (((1,), (1,)), ((), ())), preferred_element_type=F32)


def _cast_body(w_ref, o_ref):
    o_ref[...] = w_ref[...].astype(BF16)


def _cast(w_stack, layer):
    _, r, c = w_stack.shape
    br = _pick(r, (256, 128, 64, 16))
    return pl.pallas_call(
        _cast_body,
        grid=(r // br,),
        in_specs=[pl.BlockSpec((None, br, c), lambda i: (layer, i, 0))],
        out_specs=pl.BlockSpec((br, c), lambda i: (i, 0)),
        out_shape=jax.ShapeDtypeStruct((r, c), BF16),
        compiler_params=_params("parallel"),
        name="cast",
    )(w_stack)


def _ffn_body(*refs, nf, final_norm, ride, aliased):
    refs = list(refs)
    x_ref, g_ref, w1_ref, w3_ref, w2_ref, fg_ref = refs[:6]
    del refs[:6]
    if ride:
        nxt = refs[:3]
        del refs[:3]
    if aliased:
        del refs[:1]
    o_ref = refs.pop(0)
    if ride:
        cst = refs[:3]
        del refs[:3]
    h_ref = refs.pop(0)
    f = pl.program_id(1)

    @pl.when(f == 0)
    def _():
        h_ref[...] = _rms(x_ref[...], g_ref[...]).astype(BF16)
        o_ref[...] = jnp.zeros_like(o_ref)

    h = h_ref[...]
    a = _dot(h, w1_ref[...])
    b = _dot(h, w3_ref[...])
    act = (a * jax.nn.sigmoid(a) * b).astype(BF16)
    d = o_ref.shape[1]
    tc = _pick(d, (1024, 512, 256, 128))
    for c in range(0, d, tc):
        o_ref[:, c:c + tc] += _dot(act, w2_ref[:, c:c + tc])
    if ride:
        for src, dst in zip(nxt, cst):
            dst[...] = src[...].astype(BF16)

    @pl.when(f == nf - 1)
    def _():
        y = x_ref[...] + 0.5 * o_ref[...]
        if final_norm:
            y = _rms(y, fg_ref[...])
        o_ref[...] = y


def _ffn(x, g, w, fg, *, row0, nrows, final_norm=False, out_rows=None, out_row0=0, out_alias=None,
         ride=None):
    d = x.shape[1]
    w1, w3, w2 = w
    dff = w1.shape[1]
    tm = _pick(nrows, (512, 256, 128, 64))
    tf = _pick(dff, (256, 128))
    nf = dff // tf
    ni = nrows // tm
    out_rows = nrows if out_rows is None else out_rows
    rb0, ob0 = row0 // tm, out_row0 // tm
    in_specs = [
        pl.BlockSpec((tm, d), lambda i, f: (rb0 + i, 0), pipeline_mode=pl.Buffered(1)),
        pl.BlockSpec((1, d), lambda i, f: (0, 0)),
        pl.BlockSpec((d, tf), lambda i, f: (0, f)),
        pl.BlockSpec((d, tf), lambda i, f: (0, f)),
        pl.BlockSpec((tf, d), lambda i, f: (f, 0)),
        pl.BlockSpec((1, d), lambda i, f: (0, 0)),
    ]
    args = [x, g, w1, w3, w2, fg]
    out_specs = [pl.BlockSpec((tm, d), lambda i, f: (ob0 + i, 0))]
    out_shape = [jax.ShapeDtypeStruct((out_rows, d), F32)]
    if ride is not None:
        n1, n3, n2, layer = ride
        pb = d // ni
        assert d % ni == 0 and pb % LANE == 0
        in_specs += [
            pl.BlockSpec((None, pb, tf), lambda i, f: (layer, i, f)),
            pl.BlockSpec((None, pb, tf), lambda i, f: (layer, i, f)),
            pl.BlockSpec((None, tf, pb), lambda i, f: (layer, f, i)),
        ]
        args += [n1, n3, n2]
        out_specs += [
            pl.BlockSpec((pb, tf), lambda i, f: (i, f)),
            pl.BlockSpec((pb, tf), lambda i, f: (i, f)),
            pl.BlockSpec((tf, pb), lambda i, f: (f, i)),
        ]
        out_shape += [jax.ShapeDtypeStruct((d, dff), BF16), jax.ShapeDtypeStruct((d, dff), BF16),
                      jax.ShapeDtypeStruct((dff, d), BF16)]
    aliases = {}
    if out_alias is not None:
        aliases = {len(args): 0}
        in_specs.append(pl.BlockSpec(memory_space=pl.ANY))
        args.append(out_alias)
    body = functools.partial(_ffn_body, nf=nf, final_norm=final_norm, ride=ride is not None,
                             aliased=out_alias is not None)
    outs = pl.pallas_call(
        body,
        grid=(ni, nf),
        in_specs=in_specs,
        out_specs=out_specs,
        out_shape=out_shape,
        scratch_shapes=[pltpu.VMEM((tm, d), BF16)],
        input_output_aliases=aliases,
        compiler_params=_params("parallel", "arbitrary"),
        name="ffn",
    )(*args)
    return outs[0], tuple(outs[1:])


def _mm_body(*refs, use_scratch, epilogue):
    refs = list(refs)
    lhs_ref = refs.pop(0)
    w_ref = refs.pop(0)
    e_ref = refs.pop(0) if epilogue is not None else None
    o_ref = refs.pop(0)
    if use_scratch:
        s_ref = refs.pop(0)

        @pl.when(pl.program_id(1) == 0)
        def _():
            s_ref[...] = lhs_ref[...].astype(BF16)

        lhs = s_ref[...]
    else:
        lhs = lhs_ref[...]
    acc = _dot(lhs, w_ref[...])
    if epilogue == "glu":
        acc = e_ref[...] * jax.nn.sigmoid(acc)
    o_ref[...] = acc.astype(o_ref.dtype)


def _mm(lhs, w, *, out_dtype, epilogue=None, extra=None, name):
    t = lhs.shape[0]
    k, n = w.shape
    tm = _pick(t, (512, 256, 128, 64))
    tn = _pick(n, (1024, 512, 256, 128))
    use_scratch = lhs.dtype != BF16
    in_specs = [pl.BlockSpec((tm, k), lambda i, j: (i, 0)),
                pl.BlockSpec((k, tn), lambda i, j: (0, j))]
    args = [lhs, w]
    if epilogue is not None:
        in_specs.append(pl.BlockSpec((tm, tn), lambda i, j: (i, j)))
        args.append(extra)
    body = functools.partial(_mm_body, use_scratch=use_scratch, epilogue=epilogue)
    return pl.pallas_call(
        body,
        grid=(t // tm, n // tn),
        in_specs=in_specs,
        out_specs=pl.BlockSpec((tm, tn), lambda i, j: (i, j)),
        out_shape=jax.ShapeDtypeStruct((t, n), out_dtype),
        scratch_shapes=[pltpu.VMEM((tm, k), BF16)] if use_scratch else [],
        compiler_params=_params("parallel", "arbitrary"),
        name=name,
    )(*args)


def _inproj_body(x_ref, g_ref, w_ref, a_ref, u_ref, h_ref, *, na):
    j = pl.program_id(1)

    @pl.when(j == 0)
    def _():
        h_ref[...] = _rms(x_ref[...], g_ref[...]).astype(BF16)

    acc = _dot(h_ref[...], w_ref[...])

    @pl.when(j < na)
    def _():
        a_ref[...] = acc

    @pl.when(j >= na)
    def _():
        u_ref[...] = acc


def _inproj(x, g, w, *, n_lat):
    t, d = x.shape
    n = w.shape[1]
    tm = _pick(t, (512, 256, 128, 64))
    tn = math.gcd(_pick(n_lat, (512, 256, 128)), _pick(n - n_lat, (512, 256, 128)))
    na = n_lat // tn
    body = functools.partial(_inproj_body, na=na)
    return pl.pallas_call(
        body,
        grid=(t // tm, n // tn),
        in_specs=[
            pl.BlockSpec((tm, d), lambda i, j: (i, 0)),
            pl.BlockSpec((1, d), lambda i, j: (0, 0)),
            pl.BlockSpec((d, tn), lambda i, j: (0, j)),
        ],
        out_specs=[
            pl.BlockSpec((tm, tn), lambda i, j: (i, jnp.minimum(j, na - 1))),
            pl.BlockSpec((tm, tn), lambda i, j: (i, jnp.maximum(j - na, 0))),
        ],
        out_shape=[
            jax.ShapeDtypeStruct((t, n_lat), F32),
            jax.ShapeDtypeStruct((t, n - n_lat), F32),
        ],
        scratch_shapes=[pltpu.VMEM((tm, d), BF16)],
        compiler_params=_params("parallel", "arbitrary"),
        name="in_proj",
    )(x, g, w)


def _outproj_body(oa_ref, ob_ref, wa_ref, wb_ref, r_ref, o_ref):
    o_ref[...] = r_ref[...] + _dot(oa_ref[...], wa_ref[...]) + _dot(ob_ref[...], wb_ref[...])


def _outproj(oa, ob, w, res):
    t, ka = oa.shape
    kb = ob.shape[1]
    n = w.shape[1]
    assert ka == kb
    tm = _pick(t, (512, 256, 128, 64))
    tn = _pick(n, (1024, 512, 256, 128))
    return pl.pallas_call(
        _outproj_body,
        grid=(t // tm, n // tn),
        in_specs=[
            pl.BlockSpec((tm, ka), lambda i, j: (i, 0)),
            pl.BlockSpec((tm, kb), lambda i, j: (i, 0)),
            pl.BlockSpec((ka, tn), lambda i, j: (0, j)),
            pl.BlockSpec((kb, tn), lambda i, j: (1, j)),
            pl.BlockSpec((tm, tn), lambda i, j: (i, j)),
        ],
        out_specs=pl.BlockSpec((tm, tn), lambda i, j: (i, j)),
        out_shape=jax.ShapeDtypeStruct((t, n), F32),
        compiler_params=_params("parallel", "arbitrary"),
        name="out_proj",
    )(oa, ob, w, w, res)


def _kvpost_body(p_ref, g_ref, cos_ref, sin_ref, ckv_ref, kr_ref, *, kv, rope):
    ckv_ref[...] = _rms(p_ref[:, :kv], g_ref[...])
    k = p_ref[:, kv:kv + rope]
    k_rot = p_ref[:, kv + LANE:kv + LANE + rope]
    kr_ref[...] = k * cos_ref[:, :rope] + k_rot * sin_ref[:, :rope]


def _kvpost(lat, g_kv, cos, sin, *, row0, nrows, q_lora, kv, rope):
    tm = _pick(nrows, (512, 256, 128, 64))
    rb0 = row0 // tm
    body = functools.partial(_kvpost_body, kv=kv, rope=rope)
    return pl.pallas_call(
        body,
        grid=(nrows // tm,),
        in_specs=[
            pl.BlockSpec((tm, q_lora), lambda i: (rb0 + i, 1)),
            pl.BlockSpec((1, kv), lambda i: (0, 0)),
            pl.BlockSpec((tm, LANE), lambda i: (rb0 + i, 0)),
            pl.BlockSpec((tm, LANE), lambda i: (rb0 + i, 0)),
        ],
        out_specs=[
            pl.BlockSpec((tm, kv), lambda i: (i, 0)),
            pl.BlockSpec((tm, rope), lambda i: (i, 0)),
        ],
        out_shape=[
            jax.ShapeDtypeStruct((nrows, kv), F32),
            jax.ShapeDtypeStruct((nrows, rope), F32),
        ],
        compiler_params=_params("parallel"),
        name="kv_post",
    )(lat, g_kv, cos, sin)


def _qproj_body(c_ref, g_ref, wq_ref, wrot_ref, cos_ref, sin_ref, o_ref, *, heads, nope):
    c = _rms(c_ref[...], g_ref[...]).astype(BF16)
    cos = cos_ref[...]
    sin = sin_ref[...]
    hw = nope + LANE
    for h in range(heads):
        main = _dot(c, wq_ref[:, h * hw:(h + 1) * hw])
        rot = _dot(c, wrot_ref[:, h * LANE:(h + 1) * LANE])
        o_ref[:, h * hw:h * hw + nope] = main[:, :nope].astype(BF16)
        o_ref[:, h * hw + nope:(h + 1) * hw] = (main[:, nope:] * cos + rot * sin).astype(BF16)


def _qproj(lat, g_q, wq, wrot, cos, sin, *, q_lora, heads, nope):
    t = lat.shape[0]
    tm = _pick(t, (512, 256, 128, 64))
    hw = nope + LANE
    body = functools.partial(_qproj_body, heads=heads, nope=nope)
    return pl.pallas_call(
        body,
        grid=(t // tm,),
        in_specs=[
            pl.BlockSpec((tm, q_lora), lambda i: (i, 0)),
            pl.BlockSpec((1, q_lora), lambda i: (0, 0)),
            pl.BlockSpec((q_lora, heads * hw), lambda i: (0, 0)),
            pl.BlockSpec((q_lora, heads * LANE), lambda i: (0, 0)),
            pl.BlockSpec((tm, LANE), lambda i: (i, 0)),
            pl.BlockSpec((tm, LANE), lambda i: (i, 0)),
        ],
        out_specs=pl.BlockSpec((tm, heads * hw), lambda i: (i, 0)),
        out_shape=jax.ShapeDtypeStruct((t, heads * hw), BF16),
        compiler_params=_params("parallel"),
        name="q_proj",
    )(lat, g_q, wq, wrot, cos, sin)


def _attn_prompt_body(q_ref, k_ref, v_ref, kr_ref, o_ref, *, tq, nope, rope, scale):
    qi = pl.program_id(2)
    q = q_ref[...]
    qn = q[:, :nope]
    qr = q[:, nope:nope + rope]
    vdim = v_ref.shape[1]

    def block(kb, carry, diagonal):
        m, l, acc = carry
        off = pl.multiple_of(kb * tq, tq)
        kn = k_ref[pl.ds(off, tq), :]
        kr = kr_ref[pl.ds(off, tq), :].astype(BF16)
        v = v_ref[pl.ds(off, tq), :]
        s = (_dot_nt(qn, kn) + _dot_nt(qr, kr)) * scale
        if diagonal:
            q_chunk = lax.broadcasted_iota(jnp.int32, (tq, tq), 0) // CHUNK
            k_chunk = lax.broadcasted_iota(jnp.int32, (tq, tq), 1) // CHUNK
            s = jnp.where(k_chunk <= q_chunk, s, NEG_INF)
        m_new = jnp.maximum(m, jnp.max(s, axis=-1, keepdims=True))
        alpha = jnp.exp(m - m_new)
        p = jnp.exp(s - m_new)
        l = alpha * l + jnp.sum(p, axis=-1, keepdims=True)
        acc = alpha * acc + _dot(p.astype(BF16), v)
        return m_new, l, acc

    init = (jnp.full((tq, 1), NEG_INF, F32), jnp.zeros((tq, 1), F32), jnp.zeros((tq, vdim), F32))
    carry = lax.fori_loop(0, qi, lambda kb, c: block(kb, c, False), init)
    m, l, acc = block(qi, carry, True)
    o_ref[...] = (acc / l).astype(o_ref.dtype)


def _attn_prompt(q_cat, kv_exp, kr, *, out_rows, batch, seq, heads, nope, rope, vdim, scale):
    tq = _pick(seq, (512, 256, 128, 64))
    nq = seq // tq
    hw = nope + LANE
    body = functools.partial(_attn_prompt_body, tq=tq, nope=nope, rope=rope, scale=scale)
    return pl.pallas_call(
        body,
        grid=(batch, heads, nq),
        in_specs=[
            pl.BlockSpec((tq, hw), lambda b, h, i: (b * nq + i, h)),
            pl.BlockSpec((seq, nope), lambda b, h, i: (b, h)),
            pl.BlockSpec((seq, vdim), lambda b, h, i: (b, heads * nope // vdim + h)),
            pl.BlockSpec((seq, rope), lambda b, h, i: (b, 0)),
        ],
        out_specs=pl.BlockSpec((tq, vdim), lambda b, h, i: (b * nq + i, h)),
        out_shape=jax.ShapeDtypeStruct((out_rows, heads * vdim), BF16),
        compiler_params=_params("parallel", "parallel", "arbitrary"),
        name="attn_prompt",
    )(q_cat, kv_exp, kv_exp, kr)


def _attn_sample_body(q_ref, cc_ref, ck_ref, nc_ref, nk_ref, wuk_ref, wuv_ref, prev_ref, o_ref,
                      qlat_ref, qr_ref, m_ref, l_ref, acc_ref, *, heads, nope, rope, vdim, tk, scale):
    del prev_ref
    ls = q_ref.shape[0]
    past = cc_ref.shape[0]
    hw = nope + LANE
    for h in range(heads):
        qh = q_ref[:, h * hw:(h + 1) * hw]
        qlat_ref[h * ls:(h + 1) * ls, :] = _dot_nt(qh[:, :nope], wuk_ref[:, h * nope:(h + 1) * nope]).astype(BF16)
        qr_ref[h * ls:(h + 1) * ls, :] = qh[:, nope:nope + rope]
    m_ref[...] = jnp.full(m_ref.shape, NEG_INF, F32)
    l_ref[...] = jnp.zeros(l_ref.shape, F32)
    acc_ref[...] = jnp.zeros(acc_ref.shape, F32)

    def step(k, kr):
        s = (_dot_nt(qlat_ref[...], k) + _dot_nt(qr_ref[...], kr)) * scale
        m = m_ref[...]
        m_new = jnp.maximum(m, jnp.max(s, axis=-1, keepdims=True))
        alpha = jnp.exp(m - m_new)
        p = jnp.exp(s - m_new)
        l_ref[...] = alpha * l_ref[...] + jnp.sum(p, axis=-1, keepdims=True)
        acc_ref[...] = alpha * acc_ref[...] + _dot(p.astype(BF16), k)
        m_ref[...] = m_new

    def body(kb, carry):
        off = pl.multiple_of(kb * tk, tk)
        step(cc_ref[pl.ds(off, tk), :].astype(BF16), ck_ref[pl.ds(off, tk), :].astype(BF16))
        return carry

    lax.fori_loop(0, past // tk, body, 0)
    step(nc_ref[...].astype(BF16), nk_ref[...].astype(BF16))
    o_lat = (acc_ref[...] / l_ref[...]).astype(BF16)
    for h in range(heads):
        o_ref[:, h * vdim:(h + 1) * vdim] = _dot(
            o_lat[h * ls:(h + 1) * ls, :], wuv_ref[:, h * vdim:(h + 1) * vdim]).astype(o_ref.dtype)


def _attn_sample(q_cat, cache_ckv, cache_kr, ckv, kr, wuk, wuv, o_prev, *, layer, row0, batch, ls, heads,
                 nope, rope, vdim, scale):
    past, kvl = cache_ckv.shape[2], cache_ckv.shape[3]
    hw = nope + LANE
    tk = _pick(past, (512, 256, 128, 64))
    blk0 = row0 // ls
    body = functools.partial(_attn_sample_body, heads=heads, nope=nope, rope=rope, vdim=vdim, tk=tk,
                             scale=scale)
    return pl.pallas_call(
        body,
        grid=(batch,),
        in_specs=[
            pl.BlockSpec((ls, heads * hw), lambda b: (blk0 + b, 0)),
            pl.BlockSpec((None, None, past, kvl), lambda b: (layer, b, 0, 0)),
            pl.BlockSpec((None, None, past, rope), lambda b: (layer, b, 0, 0)),
            pl.BlockSpec((ls, kvl), lambda b: (b, 0)),
            pl.BlockSpec((ls, rope), lambda b: (b, 0)),
            pl.BlockSpec((kvl, heads * nope), lambda b: (0, 0)),
            pl.BlockSpec((kvl, heads * vdim), lambda b: (0, 0)),
            pl.BlockSpec(memory_space=pl.ANY),
        ],
        out_specs=pl.BlockSpec((ls, heads * vdim), lambda b: (blk0 + b, 0)),
        out_shape=jax.ShapeDtypeStruct(o_prev.shape, BF16),
        scratch_shapes=[
            pltpu.VMEM((heads * ls, kvl), BF16),
            pltpu.VMEM((heads * ls, rope), BF16),
            pltpu.VMEM((heads * ls, 1), F32),
            pltpu.VMEM((heads * ls, 1), F32),
            pltpu.VMEM((heads * ls, kvl), F32),
        ],
        input_output_aliases={7: 0},
        compiler_params=_params("parallel"),
        name="attn_sample",
    )(q_cat, cache_ckv, cache_kr, ckv, kr, wuk, wuv, o_prev)


S5_PAIRS = 4


def _s5_body(u_ref, h0r_ref, h0i_ref, ar_ref, ai_ref, wb_ref, wc_ref, d_ref,
             z_ref, sr_ref, si_ref, bu_ref, *, tl):
    t = pl.program_id(2)

    @pl.when(t == 0)
    def _():
        sr_ref[...] = h0r_ref[...]
        si_ref[...] = h0i_ref[...]

    u = u_ref[...].reshape(tl * SUBLANE, LANE)
    bu_ref[...] = _dot(u.astype(BF16), wb_ref[0])
    a_re = [jnp.broadcast_to(ar_ref[0, p:p + 1, :], (SUBLANE, LANE)) for p in range(S5_PAIRS)]
    a_im = [jnp.broadcast_to(ai_ref[0, p:p + 1, :], (SUBLANE, LANE)) for p in range(S5_PAIRS)]

    def step(l, carry):
        row = pl.multiple_of(l * SUBLANE, SUBLANE)
        out = []
        for p in range(S5_PAIRS):
            s_re, s_im = carry[2 * p], carry[2 * p + 1]
            c0 = 2 * p * LANE
            n_re = a_re[p] * s_re - a_im[p] * s_im + bu_ref[pl.ds(row, SUBLANE), c0:c0 + LANE]
            n_im = a_re[p] * s_im + a_im[p] * s_re + bu_ref[pl.ds(row, SUBLANE), c0 + LANE:c0 + 2 * LANE]
            bu_ref[pl.ds(row, SUBLANE), c0:c0 + LANE] = n_re
            bu_ref[pl.ds(row, SUBLANE), c0 + LANE:c0 + 2 * LANE] = n_im
            out += [n_re, n_im]
        return tuple(out)

    init = []
    for p in range(S5_PAIRS):
        init += [sr_ref[:, p * LANE:(p + 1) * LANE], si_ref[:, p * LANE:(p + 1) * LANE]]
    fin = lax.fori_loop(0, tl, step, tuple(init), unroll=4)
    for p in range(S5_PAIRS):
        sr_ref[:, p * LANE:(p + 1) * LANE] = fin[2 * p]
        si_ref[:, p * LANE:(p + 1) * LANE] = fin[2 * p + 1]
    y = _dot(bu_ref[...].astype(BF16), wc_ref[0]) + d_ref[...] * u
    z_ref[...] = jax.nn.gelu(y).reshape(tl, SUBLANE, LANE)


def _s5(u_tm, h0_re, h0_im, a_re, a_im, wb, wc, d_skip):
    l, b, w = u_tm.shape
    nblk = w // LANE
    sw = h0_re.shape[1] // nblk
    tl = _pick(l, (128, 64))
    body = functools.partial(_s5_body, tl=tl)
    return pl.pallas_call(
        body,
        grid=(b // SUBLANE, nblk, l // tl),
        in_specs=[
            pl.BlockSpec((tl, SUBLANE, LANE), lambda g, q, t: (t, g, q)),
            pl.BlockSpec((SUBLANE, sw), lambda g, q, t: (g, q)),
            pl.BlockSpec((SUBLANE, sw), lambda g, q, t: (g, q)),
            pl.BlockSpec((1, S5_PAIRS, LANE), lambda g, q, t: (q, 0, 0)),
            pl.BlockSpec((1, S5_PAIRS, LANE), lambda g, q, t: (q, 0, 0)),
            pl.BlockSpec((1, LANE, 2 * sw), lambda g, q, t: (q, 0, 0)),
            pl.BlockSpec((1, 2 * sw, LANE), lambda g, q, t: (q, 0, 0)),
            pl.BlockSpec((1, LANE), lambda g, q, t: (0, q)),
        ],
        out_specs=[
            pl.BlockSpec((tl, SUBLANE, LANE), lambda g, q, t: (t, g, q)),
            pl.BlockSpec((SUBLANE, sw), lambda g, q, t: (g, q)),
            pl.BlockSpec((SUBLANE, sw), lambda g, q, t: (g, q)),
        ],
        out_shape=[
            jax.ShapeDtypeStruct((l, b, w), F32),
            jax.ShapeDtypeStruct(h0_re.shape, F32),
            jax.ShapeDtypeStruct(h0_im.shape, F32),
        ],
        scratch_shapes=[pltpu.VMEM((tl * SUBLANE, 2 * sw), F32)],
        compiler_params=_params("parallel", "parallel", "arbitrary"),
        name="s5",
    )(u_tm, h0_re, h0_im, a_re, a_im, wb, wc, d_skip)


def _s5_weights(a_re, a_im, log_dt, b_re, b_im, c_re, c_im):
    g, p = a_re.shape
    grp = b_re.shape[2]
    gpb = LANE // grp
    nblk = g // gpb
    a_re, a_im = a_re.astype(F32), a_im.astype(F32)
    dt = jnp.exp(log_dt.astype(F32))[:, None]
    mag = jnp.exp(a_re * dt)
    ab_re, ab_im = mag * jnp.cos(a_im * dt), mag * jnp.sin(a_im * dt)
    den = a_re * a_re + a_im * a_im
    q_re = ((ab_re - 1.0) * a_re + ab_im * a_im) / den
    q_im = (ab_im * a_re - (ab_re - 1.0) * a_im) / den
    b_re, b_im = b_re.astype(F32), b_im.astype(F32)
    bb_re = q_re[..., None] * b_re - q_im[..., None] * b_im
    bb_im = q_re[..., None] * b_im + q_im[..., None] * b_re
    eye = jnp.eye(gpb, dtype=F32)

    def blockdiag_in(m):
        m = m.reshape(nblk, gpb, p, grp)
        return jnp.einsum("qjpc,jk->qjckp", m, eye).reshape(nblk, gpb * grp, gpb * p)

    def blockdiag_out(m):
        m = m.reshape(nblk, gpb, grp, p)
        return jnp.einsum("qjcp,jk->qjpkc", m, eye).reshape(nblk, gpb * p, gpb * grp)

    def interleave(re, im, axis):
        shp = list(re.shape)
        n = shp[axis] // LANE
        shp[axis:axis + 1] = [n, LANE]
        st = jnp.stack([re.reshape(shp), im.reshape(shp)], axis=axis + 1)
        shp[axis:axis + 2] = [2 * n * LANE]
        return st.reshape(shp)

    wb = interleave(blockdiag_in(bb_re), blockdiag_in(bb_im), 2).astype(BF16)
    wc = interleave(blockdiag_out(c_re.astype(F32)), blockdiag_out(-c_im.astype(F32)), 1).astype(BF16)
    pairs = g * p // LANE
    ar = ab_re.reshape(pairs // S5_PAIRS, S5_PAIRS, LANE)
    ai = ab_im.reshape(pairs // S5_PAIRS, S5_PAIRS, LANE)
    return ar, ai, wb, wc


def _pool_body(x_ref, g_ref, hist_ref, w_ref, sc_ref, o_ref, tail_ref, ext_ref, *, tl, nt, pos0, gc):
    t = pl.program_id(1)

    @pl.when(t == 0)
    def _():
        ext_ref[0:POOL_PAD, :] = hist_ref[...]

    @pl.when(t > 0)
    def _():
        ext_ref[0:POOL_PAD, :] = ext_ref[tl:tl + POOL_PAD, :]

    x = x_ref[...]
    ext_ref[POOL_PAD:POOL_PAD + tl, :] = _rms(x, g_ref[...])
    pos = pos0 + t * tl + lax.broadcasted_iota(jnp.int32, (tl, 1), 0)
    for gi, win in enumerate(POOL_WINDOWS):
        cols = slice(gi * gc, (gi + 1) * gc)
        cur = ext_ref[POOL_PAD:POOL_PAD + tl, cols]
        wsum = cur
        for k in range(1, win):
            wsum = wsum + ext_ref[POOL_PAD - k:POOL_PAD - k + tl, cols]
        count = jnp.minimum(pos + 1, win).astype(F32)
        delta = (wsum / count - cur).astype(BF16)
        o_ref[:, cols] = x[:, cols] + _dot(delta, w_ref[gi]) * sc_ref[:, cols]

    @pl.when(t == nt - 1)
    def _():
        tail_ref[...] = ext_ref[tl:tl + POOL_PAD, :]


def _pool(x, g, hist, w, scale, *, row0, batch, seq, pos0):
    t, d = x.shape
    gc = d // len(POOL_WINDOWS)
    tl = _pick(seq, (256, 128, 64))
    nt = seq // tl
    blk0 = row0 // tl
    body = functools.partial(_pool_body, tl=tl, nt=nt, pos0=pos0, gc=gc)
    return pl.pallas_call(
        body,
        grid=(batch, nt),
        in_specs=[
            pl.BlockSpec((tl, d), lambda b, i: (blk0 + b * nt + i, 0)),
            pl.BlockSpec((1, d), lambda b, i: (0, 0)),
            pl.BlockSpec((None, POOL_PAD, d), lambda b, i: (b, 0, 0)),
            pl.BlockSpec(w.shape, lambda b, i: (0, 0, 0)),
            pl.BlockSpec((1, d), lambda b, i: (0, 0)),
        ],
        out_specs=[
            pl.BlockSpec((tl, d), lambda b, i: (blk0 + b * nt + i, 0)),
            pl.BlockSpec((None, POOL_PAD, d), lambda b, i: (b, 0, 0)),
        ],
        out_shape=[
            jax.ShapeDtypeStruct((t, d), F32),
            jax.ShapeDtypeStruct((batch, POOL_PAD, d), F32),
        ],
        scratch_shapes=[pltpu.VMEM((POOL_PAD + tl, d), F32)],
        input_output_aliases={0: 0},
        compiler_params=_params("parallel", "arbitrary"),
        name="pool",
    )(x, g, hist, w, scale)


def _rot_cols(w):
    half = w.shape[-1] // 2
    return jnp.concatenate([-w[..., half:], w[..., :half]], axis=-1)


def _pad_cols(w, n):
    return jnp.pad(w, [(0, 0)] * (w.ndim - 1) + [(0, n - w.shape[-1])])


def kernel(x_prompt, x_sample, cache_mla_ckv, cache_mla_krope, state_s5_re, state_s5_im, cache_pool, ffn1_g, ffn1_w1, ffn1_w3, ffn1_w2, mix_g, ffn2_g, ffn2_w1, ffn2_w3, ffn2_w2, final_g, ab_w_in, mla_g_q, mla_g_kv, mla_w_uq, mla_w_uk, mla_w_uv, s5_a_re, s5_a_im, s5_log_dt, s5_b_re, s5_b_im, s5_c_re, s5_c_im, s5_d, s5_w_glu, ab_w_o, pool_w, pool_scale):
    bp, seq, d = x_prompt.shape
    bs, ls, _ = x_sample.shape
    tp, ts = bp * seq, bs * ls
    tt = tp + ts
    depth = ffn1_g.shape[0]
    past = cache_mla_ckv.shape[2]
    q_lora, heads, qk_head = mla_w_uq.shape[1:]
    kvl, _, nope = mla_w_uk.shape[1:]
    vdim = mla_w_uv.shape[3]
    rope = qk_head - nope
    s5w = s5_d.shape[1]
    groups, nstate = s5_a_re.shape[1:]
    hist_rows = cache_pool.shape[2]
    scale = qk_head ** -0.5
    assert nope == LANE and vdim == LANE and 2 * rope == LANE and kvl + 2 * LANE <= q_lora
    assert bp % SUBLANE == 0 and bs % SUBLANE == 0 and hist_rows == POOL_PAD - 1

    pos = jnp.concatenate([jnp.tile(jnp.arange(seq), bp), past + jnp.tile(jnp.arange(ls), bs)])
    inv = 1.0 / (ROPE_THETA ** (jnp.arange(0, rope, 2, dtype=F32) / rope))
    ang = pos.astype(F32)[:, None] * inv[None, :]
    cos = _pad_cols(jnp.tile(jnp.cos(ang), (1, 2)), LANE)
    sin = _pad_cols(jnp.tile(jnp.sin(ang), (1, 2)), LANE)

    row = lambda v: v.reshape(1, -1).astype(F32)
    fg = row(final_g)

    ffns = []
    for layer in range(depth):
        ffns.append((ffn1_g[layer], (ffn1_w1, ffn1_w3, ffn1_w2), layer))
        ffns.append((ffn2_g[layer], (ffn2_w1, ffn2_w3, ffn2_w2), layer))
    w_next = tuple(_cast(ws, ffns[0][2]) for ws in ffns[0][1])

    def ffn_pair(k, xs, final):
        nonlocal w_next
        g = row(ffns[k][0])
        w = w_next
        ride = (*ffns[k + 1][1], ffns[k + 1][2]) if k + 1 < len(ffns) else None
        (xp, rp), (xs_, rs) = xs
        if final:
            yp, _ = _ffn(xp, g, w, fg, row0=rp, nrows=tp, final_norm=True)
            ys, _ = _ffn(xs_, g, w, fg, row0=rs, nrows=ts, final_norm=True)
            return yp, ys
        y, w_next = _ffn(xp, g, w, fg, row0=rp, nrows=tp, out_rows=tt, ride=ride)
        y, _ = _ffn(xs_, g, w, fg, row0=rs, nrows=ts, out_rows=tt, out_row0=tp, out_alias=y)
        return y

    ckv_out, kr_out, s5re_out, s5im_out, pool_out = [], [], [], [], []
    x = None
    for layer in range(depth):
        src = ((x_prompt.reshape(tp, d), 0), (x_sample.reshape(ts, d), 0)) if x is None else ((x, 0), (x, tp))
        x = ffn_pair(2 * layer, src, False)
        if layer % 2 == 0:
            i = layer // 2
            w_in = ab_w_in[i]
            w_kr = w_in[:, q_lora + kvl:q_lora + kvl + rope]
            w_mid = jnp.concatenate([w_in[:, q_lora:q_lora + kvl], _pad_cols(w_kr, LANE),
                                     _pad_cols(_rot_cols(w_kr), LANE)], axis=1)
            w_proj = jnp.concatenate([w_in[:, :q_lora], _pad_cols(w_mid, q_lora),
                                      w_in[:, q_lora + kvl + rope:]], axis=1).astype(BF16)
            lat, u = _inproj(x, row(mix_g[layer]), w_proj, n_lat=2 * q_lora)
            g_kv = row(mla_g_kv[i])
            ckv_p, kr_p = _kvpost(lat, g_kv, cos, sin, row0=0, nrows=tp, q_lora=q_lora, kv=kvl, rope=rope)
            ckv_s, kr_s = _kvpost(lat, g_kv, cos, sin, row0=tp, nrows=ts, q_lora=q_lora, kv=kvl, rope=rope)

            w_uq = mla_w_uq[i]
            wq = jnp.concatenate([w_uq[..., :nope], _pad_cols(w_uq[..., nope:], LANE)], axis=-1)
            wrot = _pad_cols(_rot_cols(w_uq[..., nope:]), LANE)
            q_cat = _qproj(lat, row(mla_g_q[i]), wq.reshape(q_lora, -1).astype(BF16),
                           wrot.reshape(q_lora, -1).astype(BF16), cos, sin,
                           q_lora=q_lora, heads=heads, nope=nope)

            wuk = mla_w_uk[i].reshape(kvl, heads * nope).astype(BF16)
            wuv = mla_w_uv[i].reshape(kvl, heads * vdim).astype(BF16)
            kv_exp = _mm(ckv_p, jnp.concatenate([wuk, wuv], axis=1), out_dtype=BF16, name="kv_expand")
            o_a = _attn_prompt(q_cat, kv_exp, kr_p, out_rows=tt, batch=bp, seq=seq, heads=heads, nope=nope,
                               rope=rope, vdim=vdim, scale=scale)
            o_a = _attn_sample(q_cat, cache_mla_ckv, cache_mla_krope, ckv_s, kr_s, wuk, wuv, o_a, layer=i,
                               row0=tp, batch=bs, ls=ls, heads=heads, nope=nope, rope=rope, vdim=vdim,
                               scale=scale)

            ar, ai, wb, wc = _s5_weights(s5_a_re[i], s5_a_im[i], s5_log_dt[i], s5_b_re[i], s5_b_im[i],
                                         s5_c_re[i], s5_c_im[i])
            u_p = u[:tp].reshape(bp, seq, s5w).swapaxes(0, 1)
            u_s = u[tp:].reshape(bs, ls, s5w).swapaxes(0, 1)
            zeros = jnp.zeros((bp, groups * nstate), F32)
            d_row = row(s5_d[i])
            z_p, pre, pim = _s5(u_p, zeros, zeros, ar, ai, wb, wc, d_row)
            z_s, sre, sim = _s5(u_s, state_s5_re[i].reshape(bs, -1).astype(F32),
                                state_s5_im[i].reshape(bs, -1).astype(F32), ar, ai, wb, wc, d_row)
            wglu = s5_w_glu[i].astype(BF16)
            z_p = z_p.reshape(tp, s5w)
            z_s = z_s.reshape(ts, s5w)
            ob_p = _mm(z_p, wglu, out_dtype=BF16, epilogue="glu", extra=z_p, name="s5_glu")
            ob_s = _mm(z_s, wglu, out_dtype=BF16, epilogue="glu", extra=z_s, name="s5_glu")
            o_b = jnp.concatenate([ob_p.reshape(seq, bp, s5w).swapaxes(0, 1).reshape(tp, s5w),
                                   ob_s.reshape(ls, bs, s5w).swapaxes(0, 1).reshape(ts, s5w)], axis=0)
            x = _outproj(o_a, o_b, ab_w_o[i].astype(BF16), x)

            ckv_out.append((ckv_p.reshape(bp, seq, kvl), ckv_s.reshape(bs, ls, kvl)))
            kr_out.append((kr_p.reshape(bp, seq, rope), kr_s.reshape(bs, ls, rope)))
            s5re_out.append((pre.reshape(bp, groups, nstate), sre.reshape(bs, groups, nstate)))
            s5im_out.append((pim.reshape(bp, groups, nstate), sim.reshape(bs, groups, nstate)))
        else:
            j = layer // 2
            wp = pool_w[j].astype(BF16)
            g_mix, sc = row(mix_g[layer]), row(pool_scale[j])
            hist_p = jnp.zeros((bp, POOL_PAD, d), F32)
            hist_s = jnp.pad(cache_pool[j].astype(F32), ((0, 0), (1, 0), (0, 0)))
            x, tail_p = _pool(x, g_mix, hist_p, wp, sc, row0=0, batch=bp, seq=seq, pos0=0)
            x, tail_s = _pool(x, g_mix, hist_s, wp, sc, row0=tp, batch=bs, seq=ls, pos0=past)
            pool_out.append((tail_p[:, 1:], tail_s[:, 1:]))
        last = layer == depth - 1
        y = ffn_pair(2 * layer + 1, ((x, 0), (x, tp)), last)
        if last:
            y_p, y_s = y
        else:
            x = y

    stack = lambda items, k: jnp.stack([it[k] for it in items])
    return (y_p.reshape(bp, seq, d), y_s.reshape(bs, ls, d),
            stack(ckv_out, 0), stack(kr_out, 0), stack(s5re_out, 0), stack(s5im_out, 0), stack(pool_out, 0),
            stack(ckv_out, 1), stack(kr_out, 1), stack(s5re_out, 1), stack(s5im_out, 1), stack(pool_out, 1))
```

```python
import functools
import math

import jax
import jax.numpy as jnp
from jax import lax
from jax.experimental import pallas as pl
from jax.experimental.pallas import tpu as pltpu

F32 = jnp.float32
BF16 = jnp.bfloat16
NORM_EPS = 1e-6
CHUNK = 64
ROPE_THETA = 10000.0
NEG_INF = -1e30
POOL_WINDOWS = (2, 4, 8, 16)
POOL_PAD = 16
LANE = 128
SUBLANE = 8
VMEM_LIMIT = 56 * 1024 * 1024


def _params(*sem):
    return pltpu.CompilerParams(dimension_semantics=sem, vmem_limit_bytes=VMEM_LIMIT)


def _pick(n, candidates):
    for c in candidates:
        if n % c == 0:
            return c
    return n


def _rms(x, g):
    y = x * lax.rsqrt(jnp.mean(x * x, axis=-1, keepdims=True) + NORM_EPS)
    return y * g


def _dot(a, b):
    return jnp.dot(a, b, preferred_element_type=F32)


def _dot_nt(a, b):
    return lax.dot_general(a, b, (((1,), (1,)), ((), ())), preferred_element_type=F32)


def _cast_body(w_ref, o_ref):
    o_ref[...] = w_ref[...].astype(BF16)


def _cast(w_stack, layer):
    _, r, c = w_stack.shape
    br = _pick(r, (256, 128, 64, 16))
    return pl.pallas_call(
        _cast_body,
        grid=(r // br,),
        in_specs=[pl.BlockSpec((None, br, c), lambda i: (layer, i, 0))],
        out_specs=pl.BlockSpec((br, c), lambda i: (i, 0)),
        out_shape=jax.ShapeDtypeStruct((r, c), BF16),
        compiler_params=_params("parallel"),
        name="cast",
    )(w_stack)


def _ffn_body(*refs, nf, final_norm, ride, aliased):
    refs = list(refs)
    x_ref, g_ref, w1_ref, w3_ref, w2_ref, fg_ref = refs[:6]
    del refs[:6]
    if ride:
        nxt = refs[:3]
        del refs[:3]
    if aliased:
        del refs[:1]
    o_ref = refs.pop(0)
    if ride:
        cst = refs[:3]
        del refs[:3]
    h_ref = refs.pop(0)
    f = pl.program_id(1)

    @pl.when(f == 0)
    def _():
        h_ref[...] = _rms(x_ref[...], g_ref[...]).astype(BF16)
        o_ref[...] = jnp.zeros_like(o_ref)

    h = h_ref[...]
    a = _dot(h, w1_ref[...])
    b = _dot(h, w3_ref[...])
    act = (a * jax.nn.sigmoid(a) * b).astype(BF16)
    d = o_ref.shape[1]
    tc = _pick(d, (1024, 512, 256, 128))
    for c in range(0, d, tc):
        o_ref[:, c:c + tc] += _dot(act, w2_ref[:, c:c + tc])
    if ride:
        for src, dst in zip(nxt, cst):
            dst[...] = src[...].astype(BF16)

    @pl.when(f == nf - 1)
    def _():
        y = x_ref[...] + 0.5 * o_ref[...]
        if final_norm:
            y = _rms(y, fg_ref[...])
        o_ref[...] = y


def _ffn(x, g, w, fg, *, row0, nrows, final_norm=False, out_rows=None, out_row0=0, out_alias=None,
         ride=None):
    d = x.shape[1]
    w1, w3, w2 = w
    dff = w1.shape[1]
    tm = _pick(nrows, (512, 256, 128, 64))
    tf = _pick(dff, (256, 128))
    nf = dff // tf
    ni = nrows // tm
    out_rows = nrows if out_rows is None else out_rows
    rb0, ob0 = row0 // tm, out_row0 // tm
    in_specs = [
        pl.BlockSpec((tm, d), lambda i, f: (rb0 + i, 0), pipeline_mode=pl.Buffered(1)),
        pl.BlockSpec((1, d), lambda i, f: (0, 0)),
        pl.BlockSpec((d, tf), lambda i, f: (0, f)),
        pl.BlockSpec((d, tf), lambda i, f: (0, f)),
        pl.BlockSpec((tf, d), lambda i, f: (f, 0)),
        pl.BlockSpec((1, d), lambda i, f: (0, 0)),
    ]
    args = [x, g, w1, w3, w2, fg]
    out_specs = [pl.BlockSpec((tm, d), lambda i, f: (ob0 + i, 0))]
    out_shape = [jax.ShapeDtypeStruct((out_rows, d), F32)]
    if ride is not None:
        n1, n3, n2, layer = ride
        pb = d // ni
        assert d % ni == 0 and pb % LANE == 0
        in_specs += [
            pl.BlockSpec((None, pb, tf), lambda i, f: (layer, i, f)),
            pl.BlockSpec((None, pb, tf), lambda i, f: (layer, i, f)),
            pl.BlockSpec((None, tf, pb), lambda i, f: (layer, f, i)),
        ]
        args += [n1, n3, n2]
        out_specs += [
            pl.BlockSpec((pb, tf), lambda i, f: (i, f)),
            pl.BlockSpec((pb, tf), lambda i, f: (i, f)),
            pl.BlockSpec((tf, pb), lambda i, f: (f, i)),
        ]
        out_shape += [jax.ShapeDtypeStruct((d, dff), BF16), jax.ShapeDtypeStruct((d, dff), BF16),
                      jax.ShapeDtypeStruct((dff, d), BF16)]
    aliases = {}
    if out_alias is not None:
        aliases = {len(args): 0}
        in_specs.append(pl.BlockSpec(memory_space=pl.ANY))
        args.append(out_alias)
    body = functools.partial(_ffn_body, nf=nf, final_norm=final_norm, ride=ride is not None,
                             aliased=out_alias is not None)
    outs = pl.pallas_call(
        body,
        grid=(ni, nf),
        in_specs=in_specs,
        out_specs=out_specs,
        out_shape=out_shape,
        scratch_shapes=[pltpu.VMEM((tm, d), BF16)],
        input_output_aliases=aliases,
        compiler_params=_params("parallel", "arbitrary"),
        name="ffn",
    )(*args)
    return outs[0], tuple(outs[1:])


def _mm_body(*refs, use_scratch, epilogue):
    refs = list(refs)
    lhs_ref = refs.pop(0)
    w_ref = refs.pop(0)
    e_ref = refs.pop(0) if epilogue is not None else None
    o_ref = refs.pop(0)
    if use_scratch:
        s_ref = refs.pop(0)

        @pl.when(pl.program_id(1) == 0)
        def _():
            s_ref[...] = lhs_ref[...].astype(BF16)

        lhs = s_ref[...]
    else:
        lhs = lhs_ref[...]
    acc = _dot(lhs, w_ref[...])
    if epilogue == "glu":
        acc = e_ref[...] * jax.nn.sigmoid(acc)
    o_ref[...] = acc.astype(o_ref.dtype)


def _mm(lhs, w, *, out_dtype, epilogue=None, extra=None, name):
    t = lhs.shape[0]
    k, n = w.shape
    tm = _pick(t, (512, 256, 128, 64))
    tn = _pick(n, (1024, 512, 256, 128))
    use_scratch = lhs.dtype != BF16
    in_specs = [pl.BlockSpec((tm, k), lambda i, j: (i, 0)),
                pl.BlockSpec((k, tn), lambda i, j: (0, j))]
    args = [lhs, w]
    if epilogue is not None:
        in_specs.append(pl.BlockSpec((tm, tn), lambda i, j: (i, j)))
        args.append(extra)
    body = functools.partial(_mm_body, use_scratch=use_scratch, epilogue=epilogue)
    return pl.pallas_call(
        body,
        grid=(t // tm, n // tn),
        in_specs=in_specs,
        out_specs=pl.BlockSpec((tm, tn), lambda i, j: (i, j)),
        out_shape=jax.ShapeDtypeStruct((t, n), out_dtype),
        scratch_shapes=[pltpu.VMEM((tm, k), BF16)] if use_scratch else [],
        compiler_params=_params("parallel", "arbitrary"),
        name=name,
    )(*args)


def _inproj_body(x_ref, g_ref, w_ref, a_ref, u_ref, h_ref, *, na):
    j = pl.program_id(1)

    @pl.when(j == 0)
    def _():
        h_ref[...] = _rms(x_ref[...], g_ref[...]).astype(BF16)

    acc = _dot(h_ref[...], w_ref[...])

    @pl.when(j < na)
    def _():
        a_ref[...] = acc

    @pl.when(j >= na)
    def _():
        u_ref[...] = acc


def _inproj(x, g, w, *, n_lat):
    t, d = x.shape
    n = w.shape[1]
    tm = _pick(t, (512, 256, 128, 64))
    tn = math.gcd(_pick(n_lat, (1024, 512, 256, 128)), _pick(n - n_lat, (1024, 512, 256, 128)))
    na = n_lat // tn
    body = functools.partial(_inproj_body, na=na)
    return pl.pallas_call(
        body,
        grid=(t // tm, n // tn),
        in_specs=[
            pl.BlockSpec((tm, d), lambda i, j: (i, 0), pipeline_mode=pl.Buffered(1)),
            pl.BlockSpec((1, d), lambda i, j: (0, 0)),
            pl.BlockSpec((d, tn), lambda i, j: (0, j)),
        ],
        out_specs=[
            pl.BlockSpec((tm, tn), lambda i, j: (i, jnp.minimum(j, na - 1))),
            pl.BlockSpec((tm, tn), lambda i, j: (i, jnp.maximum(j - na, 0))),
        ],
        out_shape=[
            jax.ShapeDtypeStruct((t, n_lat), F32),
            jax.ShapeDtypeStruct((t, n - n_lat), F32),
        ],
        scratch_shapes=[pltpu.VMEM((tm, d), BF16)],
        compiler_params=_params("parallel", "arbitrary"),
        name="in_proj",
    )(x, g, w)


def _outproj_body(oa_ref, ob_ref, wa_ref, wb_ref, r_ref, o_ref):
    o_ref[...] = r_ref[...] + _dot(oa_ref[...], wa_ref[...]) + _dot(ob_ref[...], wb_ref[...])


def _outproj(oa, ob, w, res):
    t, ka = oa.shape
    kb = ob.shape[1]
    n = w.shape[1]
    assert ka == kb
    tm = _pick(t, (512, 256, 128, 64))
    tn = _pick(n, (1024, 512, 256, 128))
    return pl.pallas_call(
        _outproj_body,
        grid=(t // tm, n // tn),
        in_specs=[
            pl.BlockSpec((tm, ka), lambda i, j: (i, 0)),
            pl.BlockSpec((tm, kb), lambda i, j: (i, 0)),
            pl.BlockSpec((ka, tn), lambda i, j: (0, j)),
            pl.BlockSpec((kb, tn), lambda i, j: (1, j)),
            pl.BlockSpec((tm, tn), lambda i, j: (i, j)),
        ],
        out_specs=pl.BlockSpec((tm, tn), lambda i, j: (i, j)),
        out_shape=jax.ShapeDtypeStruct((t, n), F32),
        compiler_params=_params("parallel", "arbitrary"),
        name="out_proj",
    )(oa, ob, w, w, res)


def _kvpost_body(p_ref, g_ref, cos_ref, sin_ref, ckv_ref, kr_ref, *, kv, rope):
    ckv_ref[...] = _rms(p_ref[:, :kv], g_ref[...])
    k = p_ref[:, kv:kv + rope]
    k_rot = p_ref[:, kv + LANE:kv + LANE + rope]
    kr_ref[...] = k * cos_ref[:, :rope] + k_rot * sin_ref[:, :rope]


def _kvpost(lat, g_kv, cos, sin, *, row0, nrows, q_lora, kv, rope):
    tm = _pick(nrows, (512, 256, 128, 64))
    rb0 = row0 // tm
    body = functools.partial(_kvpost_body, kv=kv, rope=rope)
    return pl.pallas_call(
        body,
        grid=(nrows // tm,),
        in_specs=[
            pl.BlockSpec((tm, q_lora), lambda i: (rb0 + i, 1)),
            pl.BlockSpec((1, kv), lambda i: (0, 0)),
            pl.BlockSpec((tm, LANE), lambda i: (rb0 + i, 0)),
            pl.BlockSpec((tm, LANE), lambda i: (rb0 + i, 0)),
        ],
        out_specs=[
            pl.BlockSpec((tm, kv), lambda i: (i, 0)),
            pl.BlockSpec((tm, rope), lambda i: (i, 0)),
        ],
        out_shape=[
            jax.ShapeDtypeStruct((nrows, kv), F32),
            jax.ShapeDtypeStruct((nrows, rope), F32),
        ],
        compiler_params=_params("parallel"),
        name="kv_post",
    )(lat, g_kv, cos, sin)


def _qproj_body(c_ref, g_ref, wq_ref, wrot_ref, cos_ref, sin_ref, o_ref, *, heads, nope):
    c = _rms(c_ref[...], g_ref[...]).astype(BF16)
    cos = cos_ref[...]
    sin = sin_ref[...]
    hw = nope + LANE
    for h in range(heads):
        main = _dot(c, wq_ref[:, h * hw:(h + 1) * hw])
        rot = _dot(c, wrot_ref[:, h * LANE:(h + 1) * LANE])
        o_ref[:, h * hw:h * hw + nope] = main[:, :nope].astype(BF16)
        o_ref[:, h * hw + nope:(h + 1) * hw] = (main[:, nope:] * cos + rot * sin).astype(BF16)


def _qproj(lat, g_q, wq, wrot, cos, sin, *, q_lora, heads, nope):
    t = lat.shape[0]
    tm = _pick(t, (512, 256, 128, 64))
    hw = nope + LANE
    body = functools.partial(_qproj_body, heads=heads, nope=nope)
    return pl.pallas_call(
        body,
        grid=(t // tm,),
        in_specs=[
            pl.BlockSpec((tm, q_lora), lambda i: (i, 0)),
            pl.BlockSpec((1, q_lora), lambda i: (0, 0)),
            pl.BlockSpec((q_lora, heads * hw), lambda i: (0, 0)),
            pl.BlockSpec((q_lora, heads * LANE), lambda i: (0, 0)),
            pl.BlockSpec((tm, LANE), lambda i: (i, 0)),
            pl.BlockSpec((tm, LANE), lambda i: (i, 0)),
        ],
        out_specs=pl.BlockSpec((tm, heads * hw), lambda i: (i, 0)),
        out_shape=jax.ShapeDtypeStruct((t, heads * hw), BF16),
        compiler_params=_params("parallel"),
        name="q_proj",
    )(lat, g_q, wq, wrot, cos, sin)


def _attn_prompt_body(q_ref, k_ref, v_ref, kr_ref, o_ref, kc_ref, *, tq, nope, rope, scale):
    seq = k_ref.shape[0]
    kc_ref[:, :nope] = k_ref[...]
    kc_ref[:, nope:nope + rope] = kr_ref[...].astype(BF16)
    kc_ref[:, nope + rope:] = jnp.zeros((seq, kc_ref.shape[1] - nope - rope), BF16)
    q_chunk = lax.broadcasted_iota(jnp.int32, (tq, tq), 0) // CHUNK
    k_chunk = lax.broadcasted_iota(jnp.int32, (tq, tq), 1) // CHUNK
    visible = k_chunk <= q_chunk
    los = [qi * tq for qi in range(seq // tq)]
    s_diag = [jnp.where(visible, _dot_nt(q_ref[lo:lo + tq, :], kc_ref[lo:lo + tq, :]) * scale, NEG_INF)
              for lo in los]
    s_off = [_dot_nt(q_ref[lo:lo + tq, :], kc_ref[0:lo, :]) * scale if lo else None for lo in los]
    probs = []
    for s_d, s_o in zip(s_diag, s_off):
        m = jnp.max(s_d, axis=-1, keepdims=True)
        if s_o is not None:
            m = jnp.maximum(m, jnp.max(s_o, axis=-1, keepdims=True))
        p_d = jnp.exp(s_d - m)
        l = jnp.sum(p_d, axis=-1, keepdims=True)
        p_o = None
        if s_o is not None:
            p_o = jnp.exp(s_o - m)
            l = l + jnp.sum(p_o, axis=-1, keepdims=True)
            p_o = p_o.astype(BF16)
        probs.append((p_d.astype(BF16), p_o, l))
    for lo, (p_d, p_o, l) in zip(los, probs):
        acc = _dot(p_d, v_ref[lo:lo + tq, :])
        if p_o is not None:
            acc = acc + _dot(p_o, v_ref[0:lo, :])
        o_ref[lo:lo + tq, :] = (acc / l).astype(o_ref.dtype)


def _attn_prompt(q_cat, kv_exp, kr, *, out_rows, batch, seq, heads, nope, rope, vdim, scale):
    tq = _pick(seq, (512, 256, 128, 64))
    hw = nope + LANE
    body = functools.partial(_attn_prompt_body, tq=tq, nope=nope, rope=rope, scale=scale)
    return pl.pallas_call(
        body,
        grid=(batch, heads),
        in_specs=[
            pl.BlockSpec((seq, hw), lambda b, h: (b, h)),
            pl.BlockSpec((seq, nope), lambda b, h: (b, h)),
            pl.BlockSpec((seq, vdim), lambda b, h: (b, heads * nope // vdim + h)),
            pl.BlockSpec((seq, rope), lambda b, h: (b, 0)),
        ],
        out_specs=pl.BlockSpec((seq, vdim), lambda b, h: (b, h)),
        out_shape=jax.ShapeDtypeStruct((out_rows, heads * vdim), BF16),
        scratch_shapes=[pltpu.VMEM((seq, hw), BF16)],
        compiler_params=_params("parallel", "arbitrary"),
        name="attn_prompt",
    )(q_cat, kv_exp, kv_exp, kr)


def _attn_sample_body(q_ref, cc_ref, ck_ref, nc_ref, nk_ref, wuk_ref, wuv_ref, prev_ref, o_ref,
                      qlat_ref, qr_ref, m_ref, l_ref, acc_ref, *, heads, nope, rope, vdim, tk, scale):
    del prev_ref
    ls = q_ref.shape[0]
    past = cc_ref.shape[0]
    hw = nope + LANE
    for h in range(heads):
        qh = q_ref[:, h * hw:(h + 1) * hw]
        qlat_ref[h * ls:(h + 1) * ls, :] = _dot_nt(qh[:, :nope], wuk_ref[:, h * nope:(h + 1) * nope]).astype(BF16)
        qr_ref[h * ls:(h + 1) * ls, :] = qh[:, nope:nope + rope]
    m_ref[...] = jnp.full(m_ref.shape, NEG_INF, F32)
    l_ref[...] = jnp.zeros(l_ref.shape, F32)
    acc_ref[...] = jnp.zeros(acc_ref.shape, F32)

    nrow = heads * ls
    hr = _pick(nrow, (256, 128, 64))

    def step(k, kr):
        groups = [slice(r0, r0 + hr) for r0 in range(0, nrow, hr)]
        scores = [(_dot_nt(qlat_ref[rows, :], k) + _dot_nt(qr_ref[rows, :], kr)) * scale for rows in groups]
        probs, alphas = [], []
        for rows, s in zip(groups, scores):
            m = m_ref[rows, :]
            m_new = jnp.maximum(m, jnp.max(s, axis=-1, keepdims=True))
            alpha = jnp.exp(m - m_new)
            p = jnp.exp(s - m_new)
            l_ref[rows, :] = alpha * l_ref[rows, :] + jnp.sum(p, axis=-1, keepdims=True)
            m_ref[rows, :] = m_new
            probs.append(p.astype(BF16))
            alphas.append(alpha)
        for rows, p, alpha in zip(groups, probs, alphas):
            acc_ref[rows, :] = alpha * acc_ref[rows, :] + _dot(p, k)

    def body(kb, carry):
        off = pl.multiple_of(kb * tk, tk)
        step(cc_ref[pl.ds(off, tk), :].astype(BF16), ck_ref[pl.ds(off, tk), :].astype(BF16))
        return carry

    lax.fori_loop(0, past // tk, body, 0)
    step(nc_ref[...].astype(BF16), nk_ref[...].astype(BF16))
    o_lat = (acc_ref[...] / l_ref[...]).astype(BF16)
    for h in range(heads):
        o_ref[:, h * vdim:(h + 1) * vdim] = _dot(
            o_lat[h * ls:(h + 1) * ls, :], wuv_ref[:, h * vdim:(h + 1) * vdim]).astype(o_ref.dtype)


def _attn_sample(q_cat, cache_ckv, cache_kr, ckv, kr, wuk, wuv, o_prev, *, layer, row0, batch, ls, heads,
                 nope, rope, vdim, scale):
    past, kvl = cache_ckv.shape[2], cache_ckv.shape[3]
    hw = nope + LANE
    tk = _pick(past, (512, 256, 128, 64))
    blk0 = row0 // ls
    body = functools.partial(_attn_sample_body, heads=heads, nope=nope, rope=rope, vdim=vdim, tk=tk,
                             scale=scale)
    return pl.pallas_call(
        body,
        grid=(batch,),
        in_specs=[
            pl.BlockSpec((ls, heads * hw), lambda b: (blk0 + b, 0)),
            pl.BlockSpec((None, None, past, kvl), lambda b: (layer, b, 0, 0)),
            pl.BlockSpec((None, None, past, rope), lambda b: (layer, b, 0, 0)),
            pl.BlockSpec((ls, kvl), lambda b: (b, 0)),
            pl.BlockSpec((ls, rope), lambda b: (b, 0)),
            pl.BlockSpec((kvl, heads * nope), lambda b: (0, 0)),
            pl.BlockSpec((kvl, heads * vdim), lambda b: (0, 0)),
            pl.BlockSpec(memory_space=pl.ANY),
        ],
        out_specs=pl.BlockSpec((ls, heads * vdim), lambda b: (blk0 + b, 0)),
        out_shape=jax.ShapeDtypeStruct(o_prev.shape, BF16),
        scratch_shapes=[
            pltpu.VMEM((heads * ls, kvl), BF16),
            pltpu.VMEM((heads * ls, rope), BF16),
            pltpu.VMEM((heads * ls, 1), F32),
            pltpu.VMEM((heads * ls, 1), F32),
            pltpu.VMEM((heads * ls, kvl), F32),
        ],
        input_output_aliases={7: 0},
        compiler_params=_params("parallel"),
        name="attn_sample",
    )(q_cat, cache_ckv, cache_kr, ckv, kr, wuk, wuv, o_prev)


S5_PAIRS = 4


def _s5_body(u_ref, h0r_ref, h0i_ref, ar_ref, ai_ref, wb_ref, wc_ref, d_ref,
             z_ref, sr_ref, si_ref, bu_ref, *, tl):
    t = pl.program_id(2)

    @pl.when(t == 0)
    def _():
        sr_ref[...] = h0r_ref[...]
        si_ref[...] = h0i_ref[...]

    u = u_ref[...].reshape(tl * SUBLANE, LANE)
    bu_ref[...] = _dot(u.astype(BF16), wb_ref[0])
    a_re = [jnp.broadcast_to(ar_ref[0, p:p + 1, :], (SUBLANE, LANE)) for p in range(S5_PAIRS)]
    a_im = [jnp.broadcast_to(ai_ref[0, p:p + 1, :], (SUBLANE, LANE)) for p in range(S5_PAIRS)]

    def step(l, carry):
        row = pl.multiple_of(l * SUBLANE, SUBLANE)
        out = []
        for p in range(S5_PAIRS):
            s_re, s_im = carry[2 * p], carry[2 * p + 1]
            c0 = 2 * p * LANE
            n_re = a_re[p] * s_re - a_im[p] * s_im + bu_ref[pl.ds(row, SUBLANE), c0:c0 + LANE]
            n_im = a_re[p] * s_im + a_im[p] * s_re + bu_ref[pl.ds(row, SUBLANE), c0 + LANE:c0 + 2 * LANE]
            bu_ref[pl.ds(row, SUBLANE), c0:c0 + LANE] = n_re
            bu_ref[pl.ds(row, SUBLANE), c0 + LANE:c0 + 2 * LANE] = n_im
            out += [n_re, n_im]
        return tuple(out)

    init = []
    for p in range(S5_PAIRS):
        init += [sr_ref[:, p * LANE:(p + 1) * LANE], si_ref[:, p * LANE:(p + 1) * LANE]]
    fin = lax.fori_loop(0, tl, step, tuple(init), unroll=4)
    for p in range(S5_PAIRS):
        sr_ref[:, p * LANE:(p + 1) * LANE] = fin[2 * p]
        si_ref[:, p * LANE:(p + 1) * LANE] = fin[2 * p + 1]
    y = _dot(bu_ref[...].astype(BF16), wc_ref[0]) + d_ref[...] * u
    z_ref[...] = jax.nn.gelu(y).reshape(tl, SUBLANE, LANE)


def _s5(u_tm, h0_re, h0_im, a_re, a_im, wb, wc, d_skip):
    l, b, w = u_tm.shape
    nblk = w // LANE
    sw = h0_re.shape[1] // nblk
    tl = _pick(l, (256, 128, 64))
    body = functools.partial(_s5_body, tl=tl)
    return pl.pallas_call(
        body,
        grid=(b // SUBLANE, nblk, l // tl),
        in_specs=[
            pl.BlockSpec((tl, SUBLANE, LANE), lambda g, q, t: (t, g, q)),
            pl.BlockSpec((SUBLANE, sw), lambda g, q, t: (g, q)),
            pl.BlockSpec((SUBLANE, sw), lambda g, q, t: (g, q)),
            pl.BlockSpec((1, S5_PAIRS, LANE), lambda g, q, t: (q, 0, 0)),
            pl.BlockSpec((1, S5_PAIRS, LANE), lambda g, q, t: (q, 0, 0)),
            pl.BlockSpec((1, LANE, 2 * sw), lambda g, q, t: (q, 0, 0)),
            pl.BlockSpec((1, 2 * sw, LANE), lambda g, q, t: (q, 0, 0)),
            pl.BlockSpec((1, LANE), lambda g, q, t: (0, q)),
        ],
        out_specs=[
            pl.BlockSpec((tl, SUBLANE, LANE), lambda g, q, t: (t, g, q)),
            pl.BlockSpec((SUBLANE, sw), lambda g, q, t: (g, q)),
            pl.BlockSpec((SUBLANE, sw), lambda g, q, t: (g, q)),
        ],
        out_shape=[
            jax.ShapeDtypeStruct((l, b, w), F32),
            jax.ShapeDtypeStruct(h0_re.shape, F32),
            jax.ShapeDtypeStruct(h0_im.shape, F32),
        ],
        scratch_shapes=[pltpu.VMEM((tl * SUBLANE, 2 * sw), F32)],
        compiler_params=_params("parallel", "parallel", "arbitrary"),
        name="s5",
    )(u_tm, h0_re, h0_im, a_re, a_im, wb, wc, d_skip)


def _s5_weights(a_re, a_im, log_dt, b_re, b_im, c_re, c_im):
    g, p = a_re.shape
    grp = b_re.shape[2]
    gpb = LANE // grp
    nblk = g // gpb
    a_re, a_im = a_re.astype(F32), a_im.astype(F32)
    dt = jnp.exp(log_dt.astype(F32))[:, None]
    mag = jnp.exp(a_re * dt)
    ab_re, ab_im = mag * jnp.cos(a_im * dt), mag * jnp.sin(a_im * dt)
    den = a_re * a_re + a_im * a_im
    q_re = ((ab_re - 1.0) * a_re + ab_im * a_im) / den
    q_im = (ab_im * a_re - (ab_re - 1.0) * a_im) / den
    b_re, b_im = b_re.astype(F32), b_im.astype(F32)
    bb_re = q_re[..., None] * b_re - q_im[..., None] * b_im
    bb_im = q_re[..., None] * b_im + q_im[..., None] * b_re
    eye = jnp.eye(gpb, dtype=F32)

    def blockdiag_in(m):
        m = m.reshape(nblk, gpb, p, grp)
        return jnp.einsum("qjpc,jk->qjckp", m, eye).reshape(nblk, gpb * grp, gpb * p)

    def blockdiag_out(m):
        m = m.reshape(nblk, gpb, grp, p)
        return jnp.einsum("qjcp,jk->qjpkc", m, eye).reshape(nblk, gpb * p, gpb * grp)

    def interleave(re, im, axis):
        shp = list(re.shape)
        n = shp[axis] // LANE
        shp[axis:axis + 1] = [n, LANE]
        st = jnp.stack([re.reshape(shp), im.reshape(shp)], axis=axis + 1)
        shp[axis:axis + 2] = [2 * n * LANE]
        return st.reshape(shp)

    wb = interleave(blockdiag_in(bb_re), blockdiag_in(bb_im), 2).astype(BF16)
    wc = interleave(blockdiag_out(c_re.astype(F32)), blockdiag_out(-c_im.astype(F32)), 1).astype(BF16)
    pairs = g * p // LANE
    ar = ab_re.reshape(pairs // S5_PAIRS, S5_PAIRS, LANE)
    ai = ab_im.reshape(pairs // S5_PAIRS, S5_PAIRS, LANE)
    return ar, ai, wb, wc


def _pool_body(x_ref, g_ref, hist_ref, w_ref, sc_ref, o_ref, tail_ref, ext_ref, *, tl, nt, pos0, gc):
    t = pl.program_id(1)

    @pl.when(t == 0)
    def _():
        ext_ref[0:POOL_PAD, :] = hist_ref[...]

    @pl.when(t > 0)
    def _():
        ext_ref[0:POOL_PAD, :] = ext_ref[tl:tl + POOL_PAD, :]

    x = x_ref[...]
    ext_ref[POOL_PAD:POOL_PAD + tl, :] = _rms(x, g_ref[...])
    pos = pos0 + t * tl + lax.broadcasted_iota(jnp.int32, (tl, 1), 0)
    for gi, win in enumerate(POOL_WINDOWS):
        cols = slice(gi * gc, (gi + 1) * gc)
        cur = ext_ref[POOL_PAD:POOL_PAD + tl, cols]
        wsum = cur
        for k in range(1, win):
            wsum = wsum + ext_ref[POOL_PAD - k:POOL_PAD - k + tl, cols]
        count = jnp.minimum(pos + 1, win).astype(F32)
        delta = (wsum / count - cur).astype(BF16)
        o_ref[:, cols] = x[:, cols] + _dot(delta, w_ref[gi]) * sc_ref[:, cols]

    @pl.when(t == nt - 1)
    def _():
        tail_ref[...] = ext_ref[tl:tl + POOL_PAD, :]


def _pool(x, g, hist, w, scale, *, row0, batch, seq, pos0):
    t, d = x.shape
    gc = d // len(POOL_WINDOWS)
    tl = _pick(seq, (256, 128, 64))
    nt = seq // tl
    blk0 = row0 // tl
    body = functools.partial(_pool_body, tl=tl, nt=nt, pos0=pos0, gc=gc)
    return pl.pallas_call(
        body,
        grid=(batch, nt),
        in_specs=[
            pl.BlockSpec((tl, d), lambda b, i: (blk0 + b * nt + i, 0)),
            pl.BlockSpec((1, d), lambda b, i: (0, 0)),
            pl.BlockSpec((None, POOL_PAD, d), lambda b, i: (b, 0, 0)),
            pl.BlockSpec(w.shape, lambda b, i: (0, 0, 0)),
            pl.BlockSpec((1, d), lambda b, i: (0, 0)),
        ],
        out_specs=[
            pl.BlockSpec((tl, d), lambda b, i: (blk0 + b * nt + i, 0)),
            pl.BlockSpec((None, POOL_PAD, d), lambda b, i: (b, 0, 0)),
        ],
        out_shape=[
            jax.ShapeDtypeStruct((t, d), F32),
            jax.ShapeDtypeStruct((batch, POOL_PAD, d), F32),
        ],
        scratch_shapes=[pltpu.VMEM((POOL_PAD + tl, d), F32)],
        input_output_aliases={0: 0},
        compiler_params=_params("parallel", "arbitrary"),
        name="pool",
    )(x, g, hist, w, scale)


def _rot_cols(w):
    half = w.shape[-1] // 2
    return jnp.concatenate([-w[..., half:], w[..., :half]], axis=-1)


def _pad_cols(w, n):
    return jnp.pad(w, [(0, 0)] * (w.ndim - 1) + [(0, n - w.shape[-1])])


def kernel(x_prompt, x_sample, cache_mla_ckv, cache_mla_krope, state_s5_re, state_s5_im, cache_pool, ffn1_g, ffn1_w1, ffn1_w3, ffn1_w2, mix_g, ffn2_g, ffn2_w1, ffn2_w3, ffn2_w2, final_g, ab_w_in, mla_g_q, mla_g_kv, mla_w_uq, mla_w_uk, mla_w_uv, s5_a_re, s5_a_im, s5_log_dt, s5_b_re, s5_b_im, s5_c_re, s5_c_im, s5_d, s5_w_glu, ab_w_o, pool_w, pool_scale):
    bp, seq, d = x_prompt.shape
    bs, ls, _ = x_sample.shape
    tp, ts = bp * seq, bs * ls
    tt = tp + ts
    depth = ffn1_g.shape[0]
    past = cache_mla_ckv.shape[2]
    q_lora, heads, qk_head = mla_w_uq.shape[1:]
    kvl, _, nope = mla_w_uk.shape[1:]
    vdim = mla_w_uv.shape[3]
    rope = qk_head - nope
    s5w = s5_d.shape[1]
    groups, nstate = s5_a_re.shape[1:]
    hist_rows = cache_pool.shape[2]
    scale = qk_head ** -0.5
    assert nope == LANE and vdim == LANE and 2 * rope == LANE and kvl + 2 * LANE <= q_lora
    assert bp % SUBLANE == 0 and bs % SUBLANE == 0 and hist_rows == POOL_PAD - 1

    pos = jnp.concatenate([jnp.tile(jnp.arange(seq), bp), past + jnp.tile(jnp.arange(ls), bs)])
    inv = 1.0 / (ROPE_THETA ** (jnp.arange(0, rope, 2, dtype=F32) / rope))
    ang = pos.astype(F32)[:, None] * inv[None, :]
    cos = _pad_cols(jnp.tile(jnp.cos(ang), (1, 2)), LANE)
    sin = _pad_cols(jnp.tile(jnp.sin(ang), (1, 2)), LANE)

    row = lambda v: v.reshape(1, -1).astype(F32)
    fg = row(final_g)

    ffns = []
    for layer in range(depth):
        ffns.append((ffn1_g[layer], (ffn1_w1, ffn1_w3, ffn1_w2), layer))
        ffns.append((ffn2_g[layer], (ffn2_w1, ffn2_w3, ffn2_w2), layer))
    w_next = tuple(_cast(ws, ffns[0][2]) for ws in ffns[0][1])

    def ffn_pair(k, xs, final):
        nonlocal w_next
        g = row(ffns[k][0])
        w = w_next
        ride = (*ffns[k + 1][1], ffns[k + 1][2]) if k + 1 < len(ffns) else None
        (xp, rp), (xs_, rs) = xs
        if final:
            yp, _ = _ffn(xp, g, w, fg, row0=rp, nrows=tp, final_norm=True)
            ys, _ = _ffn(xs_, g, w, fg, row0=rs, nrows=ts, final_norm=True)
            return yp, ys
        y, w_next = _ffn(xp, g, w, fg, row0=rp, nrows=tp, out_rows=tt, ride=ride)
        y, _ = _ffn(xs_, g, w, fg, row0=rs, nrows=ts, out_rows=tt, out_row0=tp, out_alias=y)
        return y

    ckv_out, kr_out, s5re_out, s5im_out, pool_out = [], [], [], [], []
    x = None
    for layer in range(depth):
        src = ((x_prompt.reshape(tp, d), 0), (x_sample.reshape(ts, d), 0)) if x is None else ((x, 0), (x, tp))
        x = ffn_pair(2 * layer, src, False)
        if layer % 2 == 0:
            i = layer // 2
            w_in = ab_w_in[i]
            w_kr = w_in[:, q_lora + kvl:q_lora + kvl + rope]
            w_mid = jnp.concatenate([w_in[:, q_lora:q_lora + kvl], _pad_cols(w_kr, LANE),
                                     _pad_cols(_rot_cols(w_kr), LANE)], axis=1)
            w_proj = jnp.concatenate([w_in[:, :q_lora], _pad_cols(w_mid, q_lora),
                                      w_in[:, q_lora + kvl + rope:]], axis=1).astype(BF16)
            lat, u = _inproj(x, row(mix_g[layer]), w_proj, n_lat=2 * q_lora)
            g_kv = row(mla_g_kv[i])
            ckv_p, kr_p = _kvpost(lat, g_kv, cos, sin, row0=0, nrows=tp, q_lora=q_lora, kv=kvl, rope=rope)
            ckv_s, kr_s = _kvpost(lat, g_kv, cos, sin, row0=tp, nrows=ts, q_lora=q_lora, kv=kvl, rope=rope)

            w_uq = mla_w_uq[i]
            wq = jnp.concatenate([w_uq[..., :nope], _pad_cols(w_uq[..., nope:], LANE)], axis=-1)
            wrot = _pad_cols(_rot_cols(w_uq[..., nope:]), LANE)
            q_cat = _qproj(lat, row(mla_g_q[i]), wq.reshape(q_lora, -1).astype(BF16),
                           wrot.reshape(q_lora, -1).astype(BF16), cos, sin,
                           q_lora=q_lora, heads=heads, nope=nope)

            wuk = mla_w_uk[i].reshape(kvl, heads * nope).astype(BF16)
            wuv = mla_w_uv[i].reshape(kvl, heads * vdim).astype(BF16)
            kv_exp = _mm(ckv_p, jnp.concatenate([wuk, wuv], axis=1), out_dtype=BF16, name="kv_expand")
            o_a = _attn_prompt(q_cat, kv_exp, kr_p, out_rows=tt, batch=bp, seq=seq, heads=heads, nope=nope,
                               rope=rope, vdim=vdim, scale=scale)
            o_a = _attn_sample(q_cat, cache_mla_ckv, cache_mla_krope, ckv_s, kr_s, wuk, wuv, o_a, layer=i,
                               row0=tp, batch=bs, ls=ls, heads=heads, nope=nope, rope=rope, vdim=vdim,
                               scale=scale)

            ar, ai, wb, wc = _s5_weights(s5_a_re[i], s5_a_im[i], s5_log_dt[i], s5_b_re[i], s5_b_im[i],
                                         s5_c_re[i], s5_c_im[i])
            u_p = u[:tp].reshape(bp, seq, s5w).swapaxes(0, 1)
            u_s = u[tp:].reshape(bs, ls, s5w).swapaxes(0, 1)
            zeros = jnp.zeros((bp, groups * nstate), F32)
            d_row = row(s5_d[i])
            z_p, pre, pim = _s5(u_p, zeros, zeros, ar, ai, wb, wc, d_row)
            z_s, sre, sim = _s5(u_s, state_s5_re[i].reshape(bs, -1).astype(F32),
                                state_s5_im[i].reshape(bs, -1).astype(F32), ar, ai, wb, wc, d_row)
            wglu = s5_w_glu[i].astype(BF16)
            z_p = z_p.reshape(tp, s5w)
            z_s = z_s.reshape(ts, s5w)
            ob_p = _mm(z_p, wglu, out_dtype=BF16, epilogue="glu", extra=z_p, name="s5_glu")
            ob_s = _mm(z_s, wglu, out_dtype=BF16, epilogue="glu", extra=z_s, name="s5_glu")
            o_b = jnp.concatenate([ob_p.reshape(seq, bp, s5w).swapaxes(0, 1).reshape(tp, s5w),
                                   ob_s.reshape(ls, bs, s5w).swapaxes(0, 1).reshape(ts, s5w)], axis=0)
            x = _outproj(o_a, o_b, ab_w_o[i].astype(BF16), x)

            ckv_out.append((ckv_p.reshape(bp, seq, kvl), ckv_s.reshape(bs, ls, kvl)))
            kr_out.append((kr_p.reshape(bp, seq, rope), kr_s.reshape(bs, ls, rope)))
            s5re_out.append((pre.reshape(bp, groups, nstate), sre.reshape(bs, groups, nstate)))
            s5im_out.append((pim.reshape(bp, groups, nstate), sim.reshape(bs, groups, nstate)))
        else:
            j = layer // 2
            wp = pool_w[j].astype(BF16)
            g_mix, sc = row(mix_g[layer]), row(pool_scale[j])
            hist_p = jnp.zeros((bp, POOL_PAD, d), F32)
            hist_s = jnp.pad(cache_pool[j].astype(F32), ((0, 0), (1, 0), (0, 0)))
            x, tail_p = _pool(x, g_mix, hist_p, wp, sc, row0=0, batch=bp, seq=seq, pos0=0)
            x, tail_s = _pool(x, g_mix, hist_s, wp, sc, row0=tp, batch=bs, seq=ls, pos0=past)
            pool_out.append((tail_p[:, 1:], tail_s[:, 1:]))
        last = layer == depth - 1
        y = ffn_pair(2 * layer + 1, ((x, 0), (x, tp)), last)
        if last:
            y_p, y_s = y
        else:
            x = y

    stack = lambda items, k: jnp.stack([it[k] for it in items])
    return (y_p.reshape(bp, seq, d), y_s.reshape(bs, ls, d),
            stack(ckv_out, 0), stack(kr_out, 0), stack(s5re_out, 0), stack(s5im_out, 0), stack(pool_out, 0),
            stack(ckv_out, 1), stack(kr_out, 1), stack(s5re_out, 1), stack(s5im_out, 1), stack(pool_out, 1))
```

```python
import functools
import math

import jax
import jax.numpy as jnp
from jax import lax
from jax.experimental import pallas as pl
from jax.experimental.pallas import tpu as pltpu

F32 = jnp.float32
BF16 = jnp.bfloat16
NORM_EPS = 1e-6
CHUNK = 64
ROPE_THETA = 10000.0
NEG_INF = -1e30
POOL_WINDOWS = (2, 4, 8, 16)
POOL_PAD = 16
LANE = 128
SUBLANE = 8
VMEM_LIMIT = 56 * 1024 * 1024


def _params(*sem):
    return pltpu.CompilerParams(dimension_semantics=sem, vmem_limit_bytes=VMEM_LIMIT)


def _pick(n, candidates):
    for c in candidates:
        if n % c == 0:
            return c
    return n


def _rms(x, g):
    y = x * lax.rsqrt(jnp.mean(x * x, axis=-1, keepdims=True) + NORM_EPS)
    return y * g


def _dot(a, b):
    return jnp.dot(a, b, preferred_element_type=F32)


def _dot_nt(a, b):
    return lax.dot_general(a, b, (((1,), (1,)), ((), ())), preferred_element_type=F32)


def _cast_tile(src_ref, dst_ref, row0, col0, nrow, ncol):
    v = src_ref[...]
    rows = row0 + lax.broadcasted_iota(jnp.int32, v.shape, 0)
    cols = col0 + lax.broadcasted_iota(jnp.int32, v.shape, 1)
    dst_ref[...] = jnp.where((rows < nrow) & (cols < ncol), v, 0.0).astype(BF16)


def _cast_body(w_ref, o_ref, *, nrow, ncol):
    br, bc = o_ref.shape
    _cast_tile(w_ref, o_ref, pl.program_id(0) * br, pl.program_id(1) * bc, nrow, ncol)


def _cast_pad(w_stack, layer, rows_p, cols_p):
    _, r, c = w_stack.shape
    br = _pick(rows_p, (512, 256, 128, 64, 16))
    bc = _pick(cols_p, (2048, 1024, 512, 256, 128))
    body = functools.partial(_cast_body, nrow=r, ncol=c)
    return pl.pallas_call(
        body,
        grid=(rows_p // br, cols_p // bc),
        in_specs=[pl.BlockSpec((None, br, bc), lambda i, j: (layer, i, j))],
        out_specs=pl.BlockSpec((br, bc), lambda i, j: (i, j)),
        out_shape=jax.ShapeDtypeStruct((rows_p, cols_p), BF16),
        compiler_params=_params("parallel", "parallel"),
        name="cast",
    )(w_stack)


FF_TILE = 512


def _ffn_up_body(x_ref, g_ref, w1_ref, w3_ref, o_ref, h_ref, *, halves):
    hr = o_ref.shape[0] // halves

    @pl.when(pl.program_id(1) == 0)
    def _():
        for r0 in range(0, o_ref.shape[0], hr):
            h_ref[r0:r0 + hr, :] = _rms(x_ref[r0:r0 + hr, :], g_ref[...]).astype(BF16)

    for r0 in range(0, o_ref.shape[0], hr):
        h = h_ref[r0:r0 + hr, :]
        a = _dot(h, w1_ref[...])
        b = _dot(h, w3_ref[...])
        o_ref[r0:r0 + hr, :] = (a * jax.nn.sigmoid(a) * b).astype(BF16)


def _ffn_up(x, g, w1, w3, *, row0, nrows):
    d = x.shape[1]
    fp = w1.shape[1]
    tm = _pick(nrows, (1024, 512, 256, 128, 64))
    tn = FF_TILE
    rb0 = row0 // tm
    body = functools.partial(_ffn_up_body, halves=max(1, tm // 256))
    return pl.pallas_call(
        body,
        grid=(nrows // tm, fp // tn),
        in_specs=[
            pl.BlockSpec((tm, d), lambda i, j: (rb0 + i, 0), pipeline_mode=pl.Buffered(1)),
            pl.BlockSpec((1, d), lambda i, j: (0, 0)),
            pl.BlockSpec((d, tn), lambda i, j: (0, j)),
            pl.BlockSpec((d, tn), lambda i, j: (0, j)),
        ],
        out_specs=pl.BlockSpec((tm, tn), lambda i, j: (i, j)),
        out_shape=jax.ShapeDtypeStruct((nrows, fp), BF16),
        scratch_shapes=[pltpu.VMEM((tm, d), BF16)],
        compiler_params=_params("parallel", "arbitrary"),
        name="ffn_up",
    )(x, g, w1, w3)


def _ffn_down_body(*refs, ride, aliased, dff, d):
    refs = list(refs)
    act_ref, w2_ref, x_ref = refs[:3]
    del refs[:3]
    if ride:
        nxt = refs[:3]
        del refs[:3]
    if aliased:
        del refs[:1]
    o_ref = refs.pop(0)
    o_ref[...] = x_ref[...] + 0.5 * _dot(act_ref[...], w2_ref[...])
    if ride:
        j, i = pl.program_id(0), pl.program_id(1)
        c1, c3, c2 = refs
        pr, pc = c1.shape
        _cast_tile(nxt[0], c1, i * pr, j * pc, d, dff)
        _cast_tile(nxt[1], c3, i * pr, j * pc, d, dff)
        pr, pc = c2.shape
        _cast_tile(nxt[2], c2, i * pr, j * pc, dff, d)


def _ffn_down(act, w2, x, *, row0, nrows, out_rows=None, out_row0=0, out_alias=None, ride=None):
    fp, d = w2.shape
    tm = _pick(nrows, (512, 256, 128, 64))
    tn = _pick(d, (512, 256, 128))
    ni, nj = nrows // tm, d // tn
    out_rows = nrows if out_rows is None else out_rows
    rb0, ob0 = row0 // tm, out_row0 // tm
    in_specs = [
        pl.BlockSpec((tm, fp), lambda j, i: (i, 0)),
        pl.BlockSpec((fp, tn), lambda j, i: (0, j), pipeline_mode=pl.Buffered(1)),
        pl.BlockSpec((tm, tn), lambda j, i: (rb0 + i, j)),
    ]
    args = [act, w2, x]
    out_specs = [pl.BlockSpec((tm, tn), lambda j, i: (ob0 + i, j))]
    out_shape = [jax.ShapeDtypeStruct((out_rows, d), F32)]
    dff = 0
    if ride is not None:
        n1, n3, n2, layer, dff = ride
        assert d % ni == 0 and fp % nj == 0 and fp % ni == 0
        t13 = (d // ni, fp // nj)
        t2 = (fp // ni, d // nj)
        assert t13[0] % 16 == 0 and t13[1] % LANE == 0 and t2[0] % 16 == 0 and t2[1] % LANE == 0
        in_specs += [
            pl.BlockSpec((None,) + t13, lambda j, i: (layer, i, j)),
            pl.BlockSpec((None,) + t13, lambda j, i: (layer, i, j)),
            pl.BlockSpec((None,) + t2, lambda j, i: (layer, i, j)),
        ]
        args += [n1, n3, n2]
        out_specs += [pl.BlockSpec(t13, lambda j, i: (i, j)), pl.BlockSpec(t13, lambda j, i: (i, j)),
                      pl.BlockSpec(t2, lambda j, i: (i, j))]
        out_shape += [jax.ShapeDtypeStruct((d, fp), BF16), jax.ShapeDtypeStruct((d, fp), BF16),
                      jax.ShapeDtypeStruct((fp, d), BF16)]
    aliases = {}
    if out_alias is not None:
        aliases = {len(args): 0}
        in_specs.append(pl.BlockSpec(memory_space=pl.ANY))
        args.append(out_alias)
    body = functools.partial(_ffn_down_body, ride=ride is not None, aliased=out_alias is not None,
                             dff=dff, d=d)
    outs = pl.pallas_call(
        body,
        grid=(nj, ni),
        in_specs=in_specs,
        out_specs=out_specs,
        out_shape=out_shape,
        input_output_aliases=aliases,
        compiler_params=_params("arbitrary", "arbitrary"),
        name="ffn_down",
    )(*args)
    return outs[0], tuple(outs[1:])


def _final_norm_body(x_ref, g_ref, o_ref):
    o_ref[...] = _rms(x_ref[...], g_ref[...])


def _final_norm(x, g, *, row0, nrows):
    d = x.shape[1]
    tm = _pick(nrows, (512, 256, 128, 64))
    rb0 = row0 // tm
    return pl.pallas_call(
        _final_norm_body,
        grid=(nrows // tm,),
        in_specs=[pl.BlockSpec((tm, d), lambda i: (rb0 + i, 0)), pl.BlockSpec((1, d), lambda i: (0, 0))],
        out_specs=pl.BlockSpec((tm, d), lambda i: (i, 0)),
        out_shape=jax.ShapeDtypeStruct((nrows, d), F32),
        compiler_params=_params("parallel"),
        name="final_norm",
    )(x, g)


def _mm_body(*refs, use_scratch, epilogue):
    refs = list(refs)
    lhs_ref = refs.pop(0)
    w_ref = refs.pop(0)
    e_ref = refs.pop(0) if epilogue is not None else None
    o_ref = refs.pop(0)
    if use_scratch:
        s_ref = refs.pop(0)

        @pl.when(pl.program_id(1) == 0)
        def _():
            s_ref[...] = lhs_ref[...].astype(BF16)

        lhs = s_ref[...]
    else:
        lhs = lhs_ref[...]
    acc = _dot(lhs, w_ref[...])
    if epilogue == "glu":
        acc = e_ref[...] * jax.nn.sigmoid(acc)
    o_ref[...] = acc.astype(o_ref.dtype)


def _mm(lhs, w, *, out_dtype, epilogue=None, extra=None, name):
    t = lhs.shape[0]
    k, n = w.shape
    tm = _pick(t, (512, 256, 128, 64))
    tn = _pick(n, (1024, 512, 256, 128))
    use_scratch = lhs.dtype != BF16
    in_specs = [pl.BlockSpec((tm, k), lambda i, j: (i, 0)),
                pl.BlockSpec((k, tn), lambda i, j: (0, j))]
    args = [lhs, w]
    if epilogue is not None:
        in_specs.append(pl.BlockSpec((tm, tn), lambda i, j: (i, j)))
        args.append(extra)
    body = functools.partial(_mm_body, use_scratch=use_scratch, epilogue=epilogue)
    return pl.pallas_call(
        body,
        grid=(t // tm, n // tn),
        in_specs=in_specs,
        out_specs=pl.BlockSpec((tm, tn), lambda i, j: (i, j)),
        out_shape=jax.ShapeDtypeStruct((t, n), out_dtype),
        scratch_shapes=[pltpu.VMEM((tm, k), BF16)] if use_scratch else [],
        compiler_params=_params("parallel", "arbitrary"),
        name=name,
    )(*args)


def _inproj_body(x_ref, g_ref, w_ref, a_ref, u_ref, h_ref, *, na):
    j = pl.program_id(1)

    @pl.when(j == 0)
    def _():
        h_ref[...] = _rms(x_ref[...], g_ref[...]).astype(BF16)

    @pl.when(j < na)
    def _():
        a_ref[...] = _dot(h_ref[...], w_ref[...])

    @pl.when(j >= na)
    def _():
        u_ref[...] = _dot(h_ref[...], w_ref[...])


def _inproj(x, g, w, *, n_lat):
    t, d = x.shape
    n = w.shape[1]
    tm = _pick(t, (512, 256, 128, 64))
    tn = math.gcd(_pick(n_lat, (1024, 512, 256, 128)), _pick(n - n_lat, (1024, 512, 256, 128)))
    na = n_lat // tn
    body = functools.partial(_inproj_body, na=na)
    return pl.pallas_call(
        body,
        grid=(t // tm, n // tn),
        in_specs=[
            pl.BlockSpec((tm, d), lambda i, j: (i, 0), pipeline_mode=pl.Buffered(1)),
            pl.BlockSpec((1, d), lambda i, j: (0, 0)),
            pl.BlockSpec((d, tn), lambda i, j: (0, j)),
        ],
        out_specs=[
            pl.BlockSpec((tm, tn), lambda i, j: (i, jnp.minimum(j, na - 1))),
            pl.BlockSpec((tm, tn), lambda i, j: (i, jnp.maximum(j - na, 0))),
        ],
        out_shape=[
            jax.ShapeDtypeStruct((t, n_lat), F32),
            jax.ShapeDtypeStruct((t, n - n_lat), F32),
        ],
        scratch_shapes=[pltpu.VMEM((tm, d), BF16)],
        compiler_params=_params("parallel", "arbitrary"),
        name="in_proj",
    )(x, g, w)


def _outproj_body(oa_ref, ob_ref, wa_ref, wb_ref, r_ref, o_ref):
    o_ref[...] = r_ref[...] + _dot(oa_ref[...], wa_ref[...]) + _dot(ob_ref[...], wb_ref[...])


def _outproj(oa, ob, w, res):
    t, ka = oa.shape
    kb = ob.shape[1]
    n = w.shape[1]
    assert ka == kb
    tm = _pick(t, (512, 256, 128, 64))
    tn = _pick(n, (1024, 512, 256, 128))
    return pl.pallas_call(
        _outproj_body,
        grid=(t // tm, n // tn),
        in_specs=[
            pl.BlockSpec((tm, ka), lambda i, j: (i, 0)),
            pl.BlockSpec((tm, kb), lambda i, j: (i, 0)),
            pl.BlockSpec((ka, tn), lambda i, j: (0, j)),
            pl.BlockSpec((kb, tn), lambda i, j: (1, j)),
            pl.BlockSpec((tm, tn), lambda i, j: (i, j)),
        ],
        out_specs=pl.BlockSpec((tm, tn), lambda i, j: (i, j)),
        out_shape=jax.ShapeDtypeStruct((t, n), F32),
        compiler_params=_params("parallel", "arbitrary"),
        name="out_proj",
    )(oa, ob, w, w, res)


def _kvpost_body(p_ref, g_ref, cos_ref, sin_ref, ckv_ref, kr_ref, *, kv, rope):
    ckv_ref[...] = _rms(p_ref[:, :kv], g_ref[...])
    k = p_ref[:, kv:kv + rope]
    k_rot = p_ref[:, kv + LANE:kv + LANE + rope]
    kr_ref[...] = k * cos_ref[:, :rope] + k_rot * sin_ref[:, :rope]


def _kvpost(lat, g_kv, cos, sin, *, row0, nrows, q_lora, kv, rope):
    tm = _pick(nrows, (512, 256, 128, 64))
    rb0 = row0 // tm
    body = functools.partial(_kvpost_body, kv=kv, rope=rope)
    return pl.pallas_call(
        body,
        grid=(nrows // tm,),
        in_specs=[
            pl.BlockSpec((tm, q_lora), lambda i: (rb0 + i, 1)),
            pl.BlockSpec((1, kv), lambda i: (0, 0)),
            pl.BlockSpec((tm, LANE), lambda i: (rb0 + i, 0)),
            pl.BlockSpec((tm, LANE), lambda i: (rb0 + i, 0)),
        ],
        out_specs=[
            pl.BlockSpec((tm, kv), lambda i: (i, 0)),
            pl.BlockSpec((tm, rope), lambda i: (i, 0)),
        ],
        out_shape=[
            jax.ShapeDtypeStruct((nrows, kv), F32),
            jax.ShapeDtypeStruct((nrows, rope), F32),
        ],
        compiler_params=_params("parallel"),
        name="kv_post",
    )(lat, g_kv, cos, sin)


def _qproj_body(c_ref, g_ref, wq_ref, wrot_ref, cos_ref, sin_ref, o_ref, *, heads, nope):
    c = _rms(c_ref[...], g_ref[...]).astype(BF16)
    cos = cos_ref[...]
    sin = sin_ref[...]
    hw = nope + LANE
    for h in range(heads):
        main = _dot(c, wq_ref[:, h * hw:(h + 1) * hw])
        rot = _dot(c, wrot_ref[:, h * LANE:(h + 1) * LANE])
        o_ref[:, h * hw:h * hw + nope] = main[:, :nope].astype(BF16)
        o_ref[:, h * hw + nope:(h + 1) * hw] = (main[:, nope:] * cos + rot * sin).astype(BF16)


def _qproj(lat, g_q, wq, wrot, cos, sin, *, q_lora, heads, nope):
    t = lat.shape[0]
    tm = _pick(t, (512, 256, 128, 64))
    hw = nope + LANE
    body = functools.partial(_qproj_body, heads=heads, nope=nope)
    return pl.pallas_call(
        body,
        grid=(t // tm,),
        in_specs=[
            pl.BlockSpec((tm, q_lora), lambda i: (i, 0)),
            pl.BlockSpec((1, q_lora), lambda i: (0, 0)),
            pl.BlockSpec((q_lora, heads * hw), lambda i: (0, 0)),
            pl.BlockSpec((q_lora, heads * LANE), lambda i: (0, 0)),
            pl.BlockSpec((tm, LANE), lambda i: (i, 0)),
            pl.BlockSpec((tm, LANE), lambda i: (i, 0)),
        ],
        out_specs=pl.BlockSpec((tm, heads * hw), lambda i: (i, 0)),
        out_shape=jax.ShapeDtypeStruct((t, heads * hw), BF16),
        compiler_params=_params("parallel"),
        name="q_proj",
    )(lat, g_q, wq, wrot, cos, sin)


def _attn_prompt_body(q_ref, k_ref, v_ref, kr_ref, o_ref, kc_ref, *, tq, nope, rope, scale):
    seq = k_ref.shape[0]
    kc_ref[:, :nope] = k_ref[...]
    kc_ref[:, nope:nope + rope] = kr_ref[...].astype(BF16)
    kc_ref[:, nope + rope:] = jnp.zeros((seq, kc_ref.shape[1] - nope - rope), BF16)
    q_chunk = lax.broadcasted_iota(jnp.int32, (tq, tq), 0) // CHUNK
    k_chunk = lax.broadcasted_iota(jnp.int32, (tq, tq), 1) // CHUNK
    visible = k_chunk <= q_chunk
    los = [qi * tq for qi in range(seq // tq)]
    s_diag = [jnp.where(visible, _dot_nt(q_ref[lo:lo + tq, :], kc_ref[lo:lo + tq, :]) * scale, NEG_INF)
              for lo in los]
    s_off = [_dot_nt(q_ref[lo:lo + tq, :], kc_ref[0:lo, :]) * scale if lo else None for lo in los]
    probs = []
    for s_d, s_o in zip(s_diag, s_off):
        m = jnp.max(s_d, axis=-1, keepdims=True)
        if s_o is not None:
            m = jnp.maximum(m, jnp.max(s_o, axis=-1, keepdims=True))
        p_d = jnp.exp(s_d - m)
        l = jnp.sum(p_d, axis=-1, keepdims=True)
        p_o = None
        if s_o is not None:
            p_o = jnp.exp(s_o - m)
            l = l + jnp.sum(p_o, axis=-1, keepdims=True)
            p_o = p_o.astype(BF16)
        probs.append((p_d.astype(BF16), p_o, l))
    for lo, (p_d, p_o, l) in zip(los, probs):
        acc = _dot(p_d, v_ref[lo:lo + tq, :])
        if p_o is not None:
            acc = acc + _dot(p_o, v_ref[0:lo, :])
        o_ref[lo:lo + tq, :] = (acc / l).astype(o_ref.dtype)


def _attn_prompt(q_cat, kv_exp, kr, *, out_rows, batch, seq, heads, nope, rope, vdim, scale):
    tq = _pick(seq, (512, 256, 128, 64))
    hw = nope + LANE
    body = functools.partial(_attn_prompt_body, tq=tq, nope=nope, rope=rope, scale=scale)
    return pl.pallas_call(
        body,
        grid=(batch, heads),
        in_specs=[
            pl.BlockSpec((seq, hw), lambda b, h: (b, h)),
            pl.BlockSpec((seq, nope), lambda b, h: (b, h)),
            pl.BlockSpec((seq, vdim), lambda b, h: (b, heads * nope // vdim + h)),
            pl.BlockSpec((seq, rope), lambda b, h: (b, 0)),
        ],
        out_specs=pl.BlockSpec((seq, vdim), lambda b, h: (b, h)),
        out_shape=jax.ShapeDtypeStruct((out_rows, heads * vdim), BF16),
        scratch_shapes=[pltpu.VMEM((seq, hw), BF16)],
        compiler_params=_params("parallel", "arbitrary"),
        name="attn_prompt",
    )(q_cat, kv_exp, kv_exp, kr)


def _attn_sample_body(q_ref, cc_ref, ck_ref, nc_ref, nk_ref, wuk_ref, wuv_ref, prev_ref, o_ref,
                      qlat_ref, qr_ref, m_ref, l_ref, acc_ref, *, heads, nope, rope, vdim, tk, scale):
    del prev_ref
    ls = q_ref.shape[0]
    past = cc_ref.shape[0]
    hw = nope + LANE
    for h in range(heads):
        qh = q_ref[:, h * hw:(h + 1) * hw]
        qlat_ref[h * ls:(h + 1) * ls, :] = _dot_nt(qh[:, :nope], wuk_ref[:, h * nope:(h + 1) * nope]).astype(BF16)
        qr_ref[h * ls:(h + 1) * ls, :] = qh[:, nope:nope + rope]
    m_ref[...] = jnp.full(m_ref.shape, NEG_INF, F32)
    l_ref[...] = jnp.zeros(l_ref.shape, F32)
    acc_ref[...] = jnp.zeros(acc_ref.shape, F32)

    nrow = heads * ls
    hr = _pick(nrow, (256, 128, 64))

    def step(k, kr):
        groups = [slice(r0, r0 + hr) for r0 in range(0, nrow, hr)]
        scores = [(_dot_nt(qlat_ref[rows, :], k) + _dot_nt(qr_ref[rows, :], kr)) * scale for rows in groups]
        probs, alphas = [], []
        for rows, s in zip(groups, scores):
            m = m_ref[rows, :]
            m_new = jnp.maximum(m, jnp.max(s, axis=-1, keepdims=True))
            alpha = jnp.exp(m - m_new)
            p = jnp.exp(s - m_new)
            l_ref[rows, :] = alpha * l_ref[rows, :] + jnp.sum(p, axis=-1, keepdims=True)
            m_ref[rows, :] = m_new
            probs.append(p.astype(BF16))
            alphas.append(alpha)
        for rows, p, alpha in zip(groups, probs, alphas):
            acc_ref[rows, :] = alpha * acc_ref[rows, :] + _dot(p, k)

    def body(kb, carry):
        off = pl.multiple_of(kb * tk, tk)
        step(cc_ref[pl.ds(off, tk), :].astype(BF16), ck_ref[pl.ds(off, tk), :].astype(BF16))
        return carry

    lax.fori_loop(0, past // tk, body, 0)
    step(nc_ref[...].astype(BF16), nk_ref[...].astype(BF16))
    o_lat = (acc_ref[...] / l_ref[...]).astype(BF16)
    for h in range(heads):
        o_ref[:, h * vdim:(h + 1) * vdim] = _dot(
            o_lat[h * ls:(h + 1) * ls, :], wuv_ref[:, h * vdim:(h + 1) * vdim]).astype(o_ref.dtype)


def _attn_sample(q_cat, cache_ckv, cache_kr, ckv, kr, wuk, wuv, o_prev, *, layer, row0, batch, ls, heads,
                 nope, rope, vdim, scale):
    past, kvl = cache_ckv.shape[2], cache_ckv.shape[3]
    hw = nope + LANE
    tk = _pick(past, (512, 256, 128, 64))
    blk0 = row0 // ls
    body = functools.partial(_attn_sample_body, heads=heads, nope=nope, rope=rope, vdim=vdim, tk=tk,
                             scale=scale)
    return pl.pallas_call(
        body,
        grid=(batch,),
        in_specs=[
            pl.BlockSpec((ls, heads * hw), lambda b: (blk0 + b, 0)),
            pl.BlockSpec((None, None, past, kvl), lambda b: (layer, b, 0, 0)),
            pl.BlockSpec((None, None, past, rope), lambda b: (layer, b, 0, 0)),
            pl.BlockSpec((ls, kvl), lambda b: (b, 0)),
            pl.BlockSpec((ls, rope), lambda b: (b, 0)),
            pl.BlockSpec((kvl, heads * nope), lambda b: (0, 0)),
            pl.BlockSpec((kvl, heads * vdim), lambda b: (0, 0)),
            pl.BlockSpec(memory_space=pl.ANY),
        ],
        out_specs=pl.BlockSpec((ls, heads * vdim), lambda b: (blk0 + b, 0)),
        out_shape=jax.ShapeDtypeStruct(o_prev.shape, BF16),
        scratch_shapes=[
            pltpu.VMEM((heads * ls, kvl), BF16),
            pltpu.VMEM((heads * ls, rope), BF16),
            pltpu.VMEM((heads * ls, 1), F32),
            pltpu.VMEM((heads * ls, 1), F32),
            pltpu.VMEM((heads * ls, kvl), F32),
        ],
        input_output_aliases={7: 0},
        compiler_params=_params("parallel"),
        name="attn_sample",
    )(q_cat, cache_ckv, cache_kr, ckv, kr, wuk, wuv, o_prev)


S5_PAIRS = 4


def _s5_body(u_ref, h0r_ref, h0i_ref, ar_ref, ai_ref, wb_ref, wc_ref, d_ref,
             z_ref, sr_ref, si_ref, bu_ref, *, tl):
    t = pl.program_id(2)

    @pl.when(t == 0)
    def _():
        sr_ref[...] = h0r_ref[...]
        si_ref[...] = h0i_ref[...]

    u = u_ref[...].reshape(tl * SUBLANE, LANE)
    bu_ref[...] = _dot(u.astype(BF16), wb_ref[0])
    a_re = [jnp.broadcast_to(ar_ref[0, p:p + 1, :], (SUBLANE, LANE)) for p in range(S5_PAIRS)]
    a_im = [jnp.broadcast_to(ai_ref[0, p:p + 1, :], (SUBLANE, LANE)) for p in range(S5_PAIRS)]

    def step(l, carry):
        row = pl.multiple_of(l * SUBLANE, SUBLANE)
        out = []
        for p in range(S5_PAIRS):
            s_re, s_im = carry[2 * p], carry[2 * p + 1]
            c0 = 2 * p * LANE
            n_re = a_re[p] * s_re - a_im[p] * s_im + bu_ref[pl.ds(row, SUBLANE), c0:c0 + LANE]
            n_im = a_re[p] * s_im + a_im[p] * s_re + bu_ref[pl.ds(row, SUBLANE), c0 + LANE:c0 + 2 * LANE]
            bu_ref[pl.ds(row, SUBLANE), c0:c0 + LANE] = n_re
            bu_ref[pl.ds(row, SUBLANE), c0 + LANE:c0 + 2 * LANE] = n_im
            out += [n_re, n_im]
        return tuple(out)

    init = []
    for p in range(S5_PAIRS):
        init += [sr_ref[:, p * LANE:(p + 1) * LANE], si_ref[:, p * LANE:(p + 1) * LANE]]
    fin = lax.fori_loop(0, tl, step, tuple(init), unroll=4)
    for p in range(S5_PAIRS):
        sr_ref[:, p * LANE:(p + 1) * LANE] = fin[2 * p]
        si_ref[:, p * LANE:(p + 1) * LANE] = fin[2 * p + 1]
    y = _dot(bu_ref[...].astype(BF16), wc_ref[0]) + d_ref[...] * u
    z_ref[...] = jax.nn.gelu(y).reshape(tl, SUBLANE, LANE)


def _s5(u_tm, h0_re, h0_im, a_re, a_im, wb, wc, d_skip):
    l, b, w = u_tm.shape
    nblk = w // LANE
    sw = h0_re.shape[1] // nblk
    tl = _pick(l, (256, 128, 64))
    body = functools.partial(_s5_body, tl=tl)
    return pl.pallas_call(
        body,
        grid=(b // SUBLANE, nblk, l // tl),
        in_specs=[
            pl.BlockSpec((tl, SUBLANE, LANE), lambda g, q, t: (t, g, q)),
            pl.BlockSpec((SUBLANE, sw), lambda g, q, t: (g, q)),
            pl.BlockSpec((SUBLANE, sw), lambda g, q, t: (g, q)),
            pl.BlockSpec((1, S5_PAIRS, LANE), lambda g, q, t: (q, 0, 0)),
            pl.BlockSpec((1, S5_PAIRS, LANE), lambda g, q, t: (q, 0, 0)),
            pl.BlockSpec((1, LANE, 2 * sw), lambda g, q, t: (q, 0, 0)),
            pl.BlockSpec((1, 2 * sw, LANE), lambda g, q, t: (q, 0, 0)),
            pl.BlockSpec((1, LANE), lambda g, q, t: (0, q)),
        ],
        out_specs=[
            pl.BlockSpec((tl, SUBLANE, LANE), lambda g, q, t: (t, g, q)),
            pl.BlockSpec((SUBLANE, sw), lambda g, q, t: (g, q)),
            pl.BlockSpec((SUBLANE, sw), lambda g, q, t: (g, q)),
        ],
        out_shape=[
            jax.ShapeDtypeStruct((l, b, w), F32),
            jax.ShapeDtypeStruct(h0_re.shape, F32),
            jax.ShapeDtypeStruct(h0_im.shape, F32),
        ],
        scratch_shapes=[pltpu.VMEM((tl * SUBLANE, 2 * sw), F32)],
        compiler_params=_params("parallel", "parallel", "arbitrary"),
        name="s5",
    )(u_tm, h0_re, h0_im, a_re, a_im, wb, wc, d_skip)


def _s5_weights(a_re, a_im, log_dt, b_re, b_im, c_re, c_im):
    g, p = a_re.shape
    grp = b_re.shape[2]
    gpb = LANE // grp
    nblk = g // gpb
    a_re, a_im = a_re.astype(F32), a_im.astype(F32)
    dt = jnp.exp(log_dt.astype(F32))[:, None]
    mag = jnp.exp(a_re * dt)
    ab_re, ab_im = mag * jnp.cos(a_im * dt), mag * jnp.sin(a_im * dt)
    den = a_re * a_re + a_im * a_im
    q_re = ((ab_re - 1.0) * a_re + ab_im * a_im) / den
    q_im = (ab_im * a_re - (ab_re - 1.0) * a_im) / den
    b_re, b_im = b_re.astype(F32), b_im.astype(F32)
    bb_re = q_re[..., None] * b_re - q_im[..., None] * b_im
    bb_im = q_re[..., None] * b_im + q_im[..., None] * b_re
    eye = jnp.eye(gpb, dtype=F32)

    def blockdiag_in(m):
        m = m.reshape(nblk, gpb, p, grp)
        return jnp.einsum("qjpc,jk->qjckp", m, eye).reshape(nblk, gpb * grp, gpb * p)

    def blockdiag_out(m):
        m = m.reshape(nblk, gpb, grp, p)
        return jnp.einsum("qjcp,jk->qjpkc", m, eye).reshape(nblk, gpb * p, gpb * grp)

    def interleave(re, im, axis):
        shp = list(re.shape)
        n = shp[axis] // LANE
        shp[axis:axis + 1] = [n, LANE]
        st = jnp.stack([re.reshape(shp), im.reshape(shp)], axis=axis + 1)
        shp[axis:axis + 2] = [2 * n * LANE]
        return st.reshape(shp)

    wb = interleave(blockdiag_in(bb_re), blockdiag_in(bb_im), 2).astype(BF16)
    wc = interleave(blockdiag_out(c_re.astype(F32)), blockdiag_out(-c_im.astype(F32)), 1).astype(BF16)
    pairs = g * p // LANE
    ar = ab_re.reshape(pairs // S5_PAIRS, S5_PAIRS, LANE)
    ai = ab_im.reshape(pairs // S5_PAIRS, S5_PAIRS, LANE)
    return ar, ai, wb, wc


def _pool_body(x_ref, g_ref, hist_ref, w_ref, sc_ref, o_ref, tail_ref, ext_ref, *, tl, nt, pos0, gc):
    t = pl.program_id(1)

    @pl.when(t == 0)
    def _():
        ext_ref[0:POOL_PAD, :] = hist_ref[...]

    @pl.when(t > 0)
    def _():
        ext_ref[0:POOL_PAD, :] = ext_ref[tl:tl + POOL_PAD, :]

    x = x_ref[...]
    ext_ref[POOL_PAD:POOL_PAD + tl, :] = _rms(x, g_ref[...])
    pos = pos0 + t * tl + lax.broadcasted_iota(jnp.int32, (tl, 1), 0)
    for gi, win in enumerate(POOL_WINDOWS):
        cols = slice(gi * gc, (gi + 1) * gc)
        cur = ext_ref[POOL_PAD:POOL_PAD + tl, cols]
        wsum = cur
        for k in range(1, win):
            wsum = wsum + ext_ref[POOL_PAD - k:POOL_PAD - k + tl, cols]
        count = jnp.minimum(pos + 1, win).astype(F32)
        delta = (wsum / count - cur).astype(BF16)
        o_ref[:, cols] = x[:, cols] + _dot(delta, w_ref[gi]) * sc_ref[:, cols]

    @pl.when(t == nt - 1)
    def _():
        tail_ref[...] = ext_ref[tl:tl + POOL_PAD, :]


def _pool(x, g, hist, w, scale, *, row0, batch, seq, pos0):
    t, d = x.shape
    gc = d // len(POOL_WINDOWS)
    tl = _pick(seq, (256, 128, 64))
    nt = seq // tl
    blk0 = row0 // tl
    body = functools.partial(_pool_body, tl=tl, nt=nt, pos0=pos0, gc=gc)
    return pl.pallas_call(
        body,
        grid=(batch, nt),
        in_specs=[
            pl.BlockSpec((tl, d), lambda b, i: (blk0 + b * nt + i, 0)),
            pl.BlockSpec((1, d), lambda b, i: (0, 0)),
            pl.BlockSpec((None, POOL_PAD, d), lambda b, i: (b, 0, 0)),
            pl.BlockSpec(w.shape, lambda b, i: (0, 0, 0)),
            pl.BlockSpec((1, d), lambda b, i: (0, 0)),
        ],
        out_specs=[
            pl.BlockSpec((tl, d), lambda b, i: (blk0 + b * nt + i, 0)),
            pl.BlockSpec((None, POOL_PAD, d), lambda b, i: (b, 0, 0)),
        ],
        out_shape=[
            jax.ShapeDtypeStruct((t, d), F32),
            jax.ShapeDtypeStruct((batch, POOL_PAD, d), F32),
        ],
        scratch_shapes=[pltpu.VMEM((POOL_PAD + tl, d), F32)],
        input_output_aliases={0: 0},
        compiler_params=_params("parallel", "arbitrary"),
        name="pool",
    )(x, g, hist, w, scale)


def _rot_cols(w):
    half = w.shape[-1] // 2
    return jnp.concatenate([-w[..., half:], w[..., :half]], axis=-1)


def _pad_cols(w, n):
    return jnp.pad(w, [(0, 0)] * (w.ndim - 1) + [(0, n - w.shape[-1])])


def kernel(x_prompt, x_sample, cache_mla_ckv, cache_mla_krope, state_s5_re, state_s5_im, cache_pool, ffn1_g, ffn1_w1, ffn1_w3, ffn1_w2, mix_g, ffn2_g, ffn2_w1, ffn2_w3, ffn2_w2, final_g, ab_w_in, mla_g_q, mla_g_kv, mla_w_uq, mla_w_uk, mla_w_uv, s5_a_re, s5_a_im, s5_log_dt, s5_b_re, s5_b_im, s5_c_re, s5_c_im, s5_d, s5_w_glu, ab_w_o, pool_w, pool_scale):
    bp, seq, d = x_prompt.shape
    bs, ls, _ = x_sample.shape
    tp, ts = bp * seq, bs * ls
    tt = tp + ts
    depth = ffn1_g.shape[0]
    past = cache_mla_ckv.shape[2]
    q_lora, heads, qk_head = mla_w_uq.shape[1:]
    kvl, _, nope = mla_w_uk.shape[1:]
    vdim = mla_w_uv.shape[3]
    rope = qk_head - nope
    s5w = s5_d.shape[1]
    groups, nstate = s5_a_re.shape[1:]
    hist_rows = cache_pool.shape[2]
    scale = qk_head ** -0.5
    assert nope == LANE and vdim == LANE and 2 * rope == LANE and kvl + 2 * LANE <= q_lora
    assert bp % SUBLANE == 0 and bs % SUBLANE == 0 and hist_rows == POOL_PAD - 1

    pos = jnp.concatenate([jnp.tile(jnp.arange(seq), bp), past + jnp.tile(jnp.arange(ls), bs)])
    inv = 1.0 / (ROPE_THETA ** (jnp.arange(0, rope, 2, dtype=F32) / rope))
    ang = pos.astype(F32)[:, None] * inv[None, :]
    cos = _pad_cols(jnp.tile(jnp.cos(ang), (1, 2)), LANE)
    sin = _pad_cols(jnp.tile(jnp.sin(ang), (1, 2)), LANE)

    row = lambda v: v.reshape(1, -1).astype(F32)
    fg = row(final_g)

    ffns = []
    for layer in range(depth):
        ffns.append((ffn1_g[layer], (ffn1_w1, ffn1_w3, ffn1_w2), layer))
        ffns.append((ffn2_g[layer], (ffn2_w1, ffn2_w3, ffn2_w2), layer))
    dff = ffn1_w1.shape[2]
    fp = -(-dff // FF_TILE) * FF_TILE
    s1, s3, s2 = ffns[0][1]
    w_next = (_cast_pad(s1, ffns[0][2], d, fp), _cast_pad(s3, ffns[0][2], d, fp),
              _cast_pad(s2, ffns[0][2], fp, d))

    def ffn_pair(k, xs):
        nonlocal w_next
        g = row(ffns[k][0])
        w1, w3, w2 = w_next
        ride = (*ffns[k + 1][1], ffns[k + 1][2], dff) if k + 1 < len(ffns) else None
        (xp, rp), (xs_, rs) = xs
        act_p = _ffn_up(xp, g, w1, w3, row0=rp, nrows=tp)
        act_s = _ffn_up(xs_, g, w1, w3, row0=rs, nrows=ts)
        y, w_next = _ffn_down(act_p, w2, xp, row0=rp, nrows=tp, out_rows=tt, ride=ride)
        y, _ = _ffn_down(act_s, w2, xs_, row0=rs, nrows=ts, out_rows=tt, out_row0=tp, out_alias=y)
        return y

    ckv_out, kr_out, s5re_out, s5im_out, pool_out = [], [], [], [], []
    x = None
    for layer in range(depth):
        src = ((x_prompt.reshape(tp, d), 0), (x_sample.reshape(ts, d), 0)) if x is None else ((x, 0), (x, tp))
        x = ffn_pair(2 * layer, src)
        if layer % 2 == 0:
            i = layer // 2
            w_in = ab_w_in[i]
            w_kr = w_in[:, q_lora + kvl:q_lora + kvl + rope]
            w_mid = jnp.concatenate([w_in[:, q_lora:q_lora + kvl], _pad_cols(w_kr, LANE),
                                     _pad_cols(_rot_cols(w_kr), LANE)], axis=1)
            w_proj = jnp.concatenate([w_in[:, :q_lora], _pad_cols(w_mid, q_lora),
                                      w_in[:, q_lora + kvl + rope:]], axis=1).astype(BF16)
            lat, u = _inproj(x, row(mix_g[layer]), w_proj, n_lat=2 * q_lora)
            g_kv = row(mla_g_kv[i])
            ckv_p, kr_p = _kvpost(lat, g_kv, cos, sin, row0=0, nrows=tp, q_lora=q_lora, kv=kvl, rope=rope)
            ckv_s, kr_s = _kvpost(lat, g_kv, cos, sin, row0=tp, nrows=ts, q_lora=q_lora, kv=kvl, rope=rope)

            w_uq = mla_w_uq[i]
            wq = jnp.concatenate([w_uq[..., :nope], _pad_cols(w_uq[..., nope:], LANE)], axis=-1)
            wrot = _pad_cols(_rot_cols(w_uq[..., nope:]), LANE)
            q_cat = _qproj(lat, row(mla_g_q[i]), wq.reshape(q_lora, -1).astype(BF16),
                           wrot.reshape(q_lora, -1).astype(BF16), cos, sin,
                           q_lora=q_lora, heads=heads, nope=nope)

            wuk = mla_w_uk[i].reshape(kvl, heads * nope).astype(BF16)
            wuv = mla_w_uv[i].reshape(kvl, heads * vdim).astype(BF16)
            kv_exp = _mm(ckv_p, jnp.concatenate([wuk, wuv], axis=1), out_dtype=BF16, name="kv_expand")
            o_a = _attn_prompt(q_cat, kv_exp, kr_p, out_rows=tt, batch=bp, seq=seq, heads=heads, nope=nope,
                               rope=rope, vdim=vdim, scale=scale)
            o_a = _attn_sample(q_cat, cache_mla_ckv, cache_mla_krope, ckv_s, kr_s, wuk, wuv, o_a, layer=i,
                               row0=tp, batch=bs, ls=ls, heads=heads, nope=nope, rope=rope, vdim=vdim,
                               scale=scale)

            ar, ai, wb, wc = _s5_weights(s5_a_re[i], s5_a_im[i], s5_log_dt[i], s5_b_re[i], s5_b_im[i],
                                         s5_c_re[i], s5_c_im[i])
            u_p = u[:tp].reshape(bp, seq, s5w).swapaxes(0, 1)
            u_s = u[tp:].reshape(bs, ls, s5w).swapaxes(0, 1)
            zeros = jnp.zeros((bp, groups * nstate), F32)
            d_row = row(s5_d[i])
            z_p, pre, pim = _s5(u_p, zeros, zeros, ar, ai, wb, wc, d_row)
            z_s, sre, sim = _s5(u_s, state_s5_re[i].reshape(bs, -1).astype(F32),
                                state_s5_im[i].reshape(bs, -1).astype(F32), ar, ai, wb, wc, d_row)
            wglu = s5_w_glu[i].astype(BF16)
            z_p = z_p.reshape(tp, s5w)
            z_s = z_s.reshape(ts, s5w)
            ob_p = _mm(z_p, wglu, out_dtype=BF16, epilogue="glu", extra=z_p, name="s5_glu")
            ob_s = _mm(z_s, wglu, out_dtype=BF16, epilogue="glu", extra=z_s, name="s5_glu")
            o_b = jnp.concatenate([ob_p.reshape(seq, bp, s5w).swapaxes(0, 1).reshape(tp, s5w),
                                   ob_s.reshape(ls, bs, s5w).swapaxes(0, 1).reshape(ts, s5w)], axis=0)
            x = _outproj(o_a, o_b, ab_w_o[i].astype(BF16), x)

            ckv_out.append((ckv_p.reshape(bp, seq, kvl), ckv_s.reshape(bs, ls, kvl)))
            kr_out.append((kr_p.reshape(bp, seq, rope), kr_s.reshape(bs, ls, rope)))
            s5re_out.append((pre.reshape(bp, groups, nstate), sre.reshape(bs, groups, nstate)))
            s5im_out.append((pim.reshape(bp, groups, nstate), sim.reshape(bs, groups, nstate)))
        else:
            j = layer // 2
            wp = pool_w[j].astype(BF16)
            g_mix, sc = row(mix_g[layer]), row(pool_scale[j])
            hist_p = jnp.zeros((bp, POOL_PAD, d), F32)
            hist_s = jnp.pad(cache_pool[j].astype(F32), ((0, 0), (1, 0), (0, 0)))
            x, tail_p = _pool(x, g_mix, hist_p, wp, sc, row0=0, batch=bp, seq=seq, pos0=0)
            x, tail_s = _pool(x, g_mix, hist_s, wp, sc, row0=tp, batch=bs, seq=ls, pos0=past)
            pool_out.append((tail_p[:, 1:], tail_s[:, 1:]))
        x = ffn_pair(2 * layer + 1, ((x, 0), (x, tp)))

    y_p = _final_norm(x, fg, row0=0, nrows=tp)
    y_s = _final_norm(x, fg, row0=tp, nrows=ts)
    stack = lambda items, k: jnp.stack([it[k] for it in items])
    return (y_p.reshape(bp, seq, d), y_s.reshape(bs, ls, d),
            stack(ckv_out, 0), stack(kr_out, 0), stack(s5re_out, 0), stack(s5im_out, 0), stack(pool_out, 0),
            stack(ckv_out, 1), stack(kr_out, 1), stack(s5re_out, 1), stack(s5im_out, 1), stack(pool_out, 1))
```

```python
import functools
import math

import jax
import jax.numpy as jnp
from jax import lax
from jax.experimental import pallas as pl
from jax.experimental.pallas import tpu as pltpu

F32 = jnp.float32
BF16 = jnp.bfloat16
NORM_EPS = 1e-6
CHUNK = 64
ROPE_THETA = 10000.0
NEG_INF = -1e30
POOL_WINDOWS = (2, 4, 8, 16)
POOL_PAD = 16
LANE = 128
SUBLANE = 8
VMEM_LIMIT = 56 * 1024 * 1024


def _params(*sem):
    return pltpu.CompilerParams(dimension_semantics=sem, vmem_limit_bytes=VMEM_LIMIT)


def _pick(n, candidates):
    for c in candidates:
        if n % c == 0:
            return c
    return n


def _rms(x, g):
    y = x * lax.rsqrt(jnp.mean(x * x, axis=-1, keepdims=True) + NORM_EPS)
    return y * g


def _dot(a, b):
    return jnp.dot(a, b, preferred_element_type=F32)


def _dot_nt(a, b):
    return lax.dot_general(a, b, (((1,), (1,)), ((), ())), preferred_element_type=F32)


def _cast_tile(src_ref, dst_ref, row0, col0, nrow, ncol):
    v = src_ref[...]
    rows = row0 + lax.broadcasted_iota(jnp.int32, v.shape, 0)
    cols = col0 + lax.broadcasted_iota(jnp.int32, v.shape, 1)
    dst_ref[...] = jnp.where((rows < nrow) & (cols < ncol), v, 0.0).astype(BF16)


def _cast_body(w_ref, o_ref, *, nrow, ncol):
    br, bc = o_ref.shape
    _cast_tile(w_ref, o_ref, pl.program_id(0) * br, pl.program_id(1) * bc, nrow, ncol)


def _cast_pad(w_stack, layer, rows_p, cols_p):
    _, r, c = w_stack.shape
    br = _pick(rows_p, (512, 256, 128, 64, 16))
    bc = _pick(cols_p, (2048, 1024, 512, 256, 128))
    body = functools.partial(_cast_body, nrow=r, ncol=c)
    return pl.pallas_call(
        body,
        grid=(rows_p // br, cols_p // bc),
        in_specs=[pl.BlockSpec((None, br, bc), lambda i, j: (layer, i, j))],
        out_specs=pl.BlockSpec((br, bc), lambda i, j: (i, j)),
        out_shape=jax.ShapeDtypeStruct((rows_p, cols_p), BF16),
        compiler_params=_params("parallel", "parallel"),
        name="cast",
    )(w_stack)


FF_TILE = 512


def _ffn_up_body(x_ref, g_ref, w1_ref, w3_ref, o_ref, h_ref, *, halves):
    hr = o_ref.shape[0] // halves

    @pl.when(pl.program_id(1) == 0)
    def _():
        for r0 in range(0, o_ref.shape[0], hr):
            h_ref[r0:r0 + hr, :] = _rms(x_ref[r0:r0 + hr, :], g_ref[...]).astype(BF16)

    for r0 in range(0, o_ref.shape[0], hr):
        h = h_ref[r0:r0 + hr, :]
        a = _dot(h, w1_ref[...])
        b = _dot(h, w3_ref[...])
        o_ref[r0:r0 + hr, :] = (a * jax.nn.sigmoid(a) * b).astype(BF16)


def _ffn_up(x, g, w1, w3, *, row0, nrows):
    d = x.shape[1]
    fp = w1.shape[1]
    tm = _pick(nrows, (1024, 512, 256, 128, 64))
    tn = FF_TILE
    rb0 = row0 // tm
    body = functools.partial(_ffn_up_body, halves=max(1, tm // 256))
    return pl.pallas_call(
        body,
        grid=(nrows // tm, fp // tn),
        in_specs=[
            pl.BlockSpec((tm, d), lambda i, j: (rb0 + i, 0), pipeline_mode=pl.Buffered(1)),
            pl.BlockSpec((1, d), lambda i, j: (0, 0)),
            pl.BlockSpec((d, tn), lambda i, j: (0, j)),
            pl.BlockSpec((d, tn), lambda i, j: (0, j)),
        ],
        out_specs=pl.BlockSpec((tm, tn), lambda i, j: (i, j)),
        out_shape=jax.ShapeDtypeStruct((nrows, fp), BF16),
        scratch_shapes=[pltpu.VMEM((tm, d), BF16)],
        compiler_params=_params("parallel", "arbitrary"),
        name="ffn_up",
    )(x, g, w1, w3)


def _ffn_down_body(*refs, ride, aliased, dff, d):
    refs = list(refs)
    act_ref, w2_ref, x_ref = refs[:3]
    del refs[:3]
    if ride:
        nxt = refs[:3]
        del refs[:3]
    if aliased:
        del refs[:1]
    o_ref = refs.pop(0)
    o_ref[...] = x_ref[...] + 0.5 * _dot(act_ref[...], w2_ref[...])
    if ride:
        j, i = pl.program_id(0), pl.program_id(1)
        c1, c3, c2 = refs
        edge = (j == pl.num_programs(0) - 1) | (i == pl.num_programs(1) - 1)

        @pl.when(edge)
        def _():
            pr, pc = c1.shape
            _cast_tile(nxt[0], c1, i * pr, j * pc, d, dff)
            _cast_tile(nxt[1], c3, i * pr, j * pc, d, dff)
            pr, pc = c2.shape
            _cast_tile(nxt[2], c2, i * pr, j * pc, dff, d)

        @pl.when(jnp.logical_not(edge))
        def _():
            for src, dst in zip(nxt, refs):
                dst[...] = src[...].astype(BF16)


def _ffn_down(act, w2, x, *, row0, nrows, out_rows=None, out_row0=0, out_alias=None, ride=None):
    fp, d = w2.shape
    tm = _pick(nrows, (512, 256, 128, 64))
    tn = _pick(d, (512, 256, 128))
    ni, nj = nrows // tm, d // tn
    out_rows = nrows if out_rows is None else out_rows
    rb0, ob0 = row0 // tm, out_row0 // tm
    in_specs = [
        pl.BlockSpec((tm, fp), lambda j, i: (i, 0)),
        pl.BlockSpec((fp, tn), lambda j, i: (0, j), pipeline_mode=pl.Buffered(1)),
        pl.BlockSpec((tm, tn), lambda j, i: (rb0 + i, j)),
    ]
    args = [act, w2, x]
    out_specs = [pl.BlockSpec((tm, tn), lambda j, i: (ob0 + i, j))]
    out_shape = [jax.ShapeDtypeStruct((out_rows, d), F32)]
    dff = 0
    if ride is not None:
        n1, n3, n2, layer, dff = ride
        assert d % ni == 0 and fp % nj == 0 and fp % ni == 0
        t13 = (d // ni, fp // nj)
        t2 = (fp // ni, d // nj)
        assert t13[0] % 16 == 0 and t13[1] % LANE == 0 and t2[0] % 16 == 0 and t2[1] % LANE == 0
        in_specs += [
            pl.BlockSpec((None,) + t13, lambda j, i: (layer, i, j)),
            pl.BlockSpec((None,) + t13, lambda j, i: (layer, i, j)),
            pl.BlockSpec((None,) + t2, lambda j, i: (layer, i, j)),
        ]
        args += [n1, n3, n2]
        out_specs += [pl.BlockSpec(t13, lambda j, i: (i, j)), pl.BlockSpec(t13, lambda j, i: (i, j)),
                      pl.BlockSpec(t2, lambda j, i: (i, j))]
        out_shape += [jax.ShapeDtypeStruct((d, fp), BF16), jax.ShapeDtypeStruct((d, fp), BF16),
                      jax.ShapeDtypeStruct((fp, d), BF16)]
    aliases = {}
    if out_alias is not None:
        aliases = {len(args): 0}
        in_specs.append(pl.BlockSpec(memory_space=pl.ANY))
        args.append(out_alias)
    body = functools.partial(_ffn_down_body, ride=ride is not None, aliased=out_alias is not None,
                             dff=dff, d=d)
    outs = pl.pallas_call(
        body,
        grid=(nj, ni),
        in_specs=in_specs,
        out_specs=out_specs,
        out_shape=out_shape,
        input_output_aliases=aliases,
        compiler_params=_params("arbitrary", "arbitrary"),
        name="ffn_down",
    )(*args)
    return outs[0], tuple(outs[1:])


def _final_norm_body(x_ref, g_ref, o_ref):
    o_ref[...] = _rms(x_ref[...], g_ref[...])


def _final_norm(x, g, *, row0, nrows):
    d = x.shape[1]
    tm = _pick(nrows, (512, 256, 128, 64))
    rb0 = row0 // tm
    return pl.pallas_call(
        _final_norm_body,
        grid=(nrows // tm,),
        in_specs=[pl.BlockSpec((tm, d), lambda i: (rb0 + i, 0)), pl.BlockSpec((1, d), lambda i: (0, 0))],
        out_specs=pl.BlockSpec((tm, d), lambda i: (i, 0)),
        out_shape=jax.ShapeDtypeStruct((nrows, d), F32),
        compiler_params=_params("parallel"),
        name="final_norm",
    )(x, g)


def _mm_body(*refs, use_scratch, epilogue):
    refs = list(refs)
    lhs_ref = refs.pop(0)
    w_ref = refs.pop(0)
    e_ref = refs.pop(0) if epilogue is not None else None
    o_ref = refs.pop(0)
    if use_scratch:
        s_ref = refs.pop(0)

        @pl.when(pl.program_id(1) == 0)
        def _():
            s_ref[...] = lhs_ref[...].astype(BF16)

        lhs = s_ref[...]
    else:
        lhs = lhs_ref[...]
    acc = _dot(lhs, w_ref[...])
    if epilogue == "glu":
        acc = e_ref[...] * jax.nn.sigmoid(acc)
    o_ref[...] = acc.astype(o_ref.dtype)


def _mm(lhs, w, *, out_dtype, epilogue=None, extra=None, name):
    t = lhs.shape[0]
    k, n = w.shape
    tm = _pick(t, (512, 256, 128, 64))
    tn = _pick(n, (1024, 512, 256, 128))
    use_scratch = lhs.dtype != BF16
    in_specs = [pl.BlockSpec((tm, k), lambda i, j: (i, 0)),
                pl.BlockSpec((k, tn), lambda i, j: (0, j))]
    args = [lhs, w]
    if epilogue is not None:
        in_specs.append(pl.BlockSpec((tm, tn), lambda i, j: (i, j)))
        args.append(extra)
    body = functools.partial(_mm_body, use_scratch=use_scratch, epilogue=epilogue)
    return pl.pallas_call(
        body,
        grid=(t // tm, n // tn),
        in_specs=in_specs,
        out_specs=pl.BlockSpec((tm, tn), lambda i, j: (i, j)),
        out_shape=jax.ShapeDtypeStruct((t, n), out_dtype),
        scratch_shapes=[pltpu.VMEM((tm, k), BF16)] if use_scratch else [],
        compiler_params=_params("parallel", "arbitrary"),
        name=name,
    )(*args)


def _inproj_body(x_ref, g_ref, w_ref, a_ref, u_ref, h_ref, *, na):
    j = pl.program_id(1)

    @pl.when(j == 0)
    def _():
        h_ref[...] = _rms(x_ref[...], g_ref[...]).astype(BF16)

    @pl.when(j < na)
    def _():
        a_ref[...] = _dot(h_ref[...], w_ref[...])

    @pl.when(j >= na)
    def _():
        u_ref[...] = _dot(h_ref[...], w_ref[...])


def _inproj(x, g, w, *, n_lat, row0, nrows):
    d = x.shape[1]
    n = w.shape[1]
    t = nrows
    tm = _pick(t, (512, 256, 128, 64))
    tn = math.gcd(_pick(n_lat, (512, 256, 128)), _pick(n - n_lat, (512, 256, 128)))
    na = n_lat // tn
    rb0 = row0 // tm
    body = functools.partial(_inproj_body, na=na)
    return pl.pallas_call(
        body,
        grid=(t // tm, n // tn),
        in_specs=[
            pl.BlockSpec((tm, d), lambda i, j: (rb0 + i, 0)),
            pl.BlockSpec((1, d), lambda i, j: (0, 0)),
            pl.BlockSpec((d, tn), lambda i, j: (0, j)),
        ],
        out_specs=[
            pl.BlockSpec((tm, tn), lambda i, j: (i, jnp.minimum(j, na - 1))),
            pl.BlockSpec((tm, tn), lambda i, j: (i, jnp.maximum(j - na, 0))),
        ],
        out_shape=[
            jax.ShapeDtypeStruct((t, n_lat), F32),
            jax.ShapeDtypeStruct((t, n - n_lat), F32),
        ],
        scratch_shapes=[pltpu.VMEM((tm, d), BF16)],
        compiler_params=_params("parallel", "arbitrary"),
        name="in_proj",
    )(x, g, w)


def _outproj_body(oa_ref, ob_ref, wa_ref, wb_ref, r_ref, o_ref):
    o_ref[...] = r_ref[...] + _dot(oa_ref[...], wa_ref[...]) + _dot(ob_ref[...], wb_ref[...])


def _outproj(oa, ob, w, res):
    t, ka = oa.shape
    kb = ob.shape[1]
    n = w.shape[1]
    assert ka == kb
    tm = _pick(t, (512, 256, 128, 64))
    tn = _pick(n, (1024, 512, 256, 128))
    return pl.pallas_call(
        _outproj_body,
        grid=(t // tm, n // tn),
        in_specs=[
            pl.BlockSpec((tm, ka), lambda i, j: (i, 0)),
            pl.BlockSpec((tm, kb), lambda i, j: (i, 0)),
            pl.BlockSpec((ka, tn), lambda i, j: (0, j)),
            pl.BlockSpec((kb, tn), lambda i, j: (1, j)),
            pl.BlockSpec((tm, tn), lambda i, j: (i, j)),
        ],
        out_specs=pl.BlockSpec((tm, tn), lambda i, j: (i, j)),
        out_shape=jax.ShapeDtypeStruct((t, n), F32),
        compiler_params=_params("parallel", "arbitrary"),
        name="out_proj",
    )(oa, ob, w, w, res)


def _kvpost_body(p_ref, g_ref, cos_ref, sin_ref, ckv_ref, kr_ref, *, kv, rope):
    ckv_ref[...] = _rms(p_ref[:, :kv], g_ref[...])
    k = p_ref[:, kv:kv + rope]
    k_rot = p_ref[:, kv + LANE:kv + LANE + rope]
    kr_ref[...] = k * cos_ref[:, :rope] + k_rot * sin_ref[:, :rope]


def _kvpost(lat, g_kv, cos, sin, *, q_lora, kv, rope):
    nrows = lat.shape[0]
    tm = _pick(nrows, (512, 256, 128, 64))
    body = functools.partial(_kvpost_body, kv=kv, rope=rope)
    return pl.pallas_call(
        body,
        grid=(nrows // tm,),
        in_specs=[
            pl.BlockSpec((tm, q_lora), lambda i: (i, 1)),
            pl.BlockSpec((1, kv), lambda i: (0, 0)),
            pl.BlockSpec((tm, LANE), lambda i: (i, 0)),
            pl.BlockSpec((tm, LANE), lambda i: (i, 0)),
        ],
        out_specs=[
            pl.BlockSpec((tm, kv), lambda i: (i, 0)),
            pl.BlockSpec((tm, rope), lambda i: (i, 0)),
        ],
        out_shape=[
            jax.ShapeDtypeStruct((nrows, kv), F32),
            jax.ShapeDtypeStruct((nrows, rope), F32),
        ],
        compiler_params=_params("parallel"),
        name="kv_post",
    )(lat, g_kv, cos, sin)


def _qproj_body(c_ref, g_ref, wq_ref, wrot_ref, cos_ref, sin_ref, o_ref, *, heads, nope):
    c = _rms(c_ref[...], g_ref[...]).astype(BF16)
    cos = cos_ref[...]
    sin = sin_ref[...]
    hw = nope + LANE
    for h in range(heads):
        main = _dot(c, wq_ref[:, h * hw:(h + 1) * hw])
        rot = _dot(c, wrot_ref[:, h * LANE:(h + 1) * LANE])
        o_ref[:, h * hw:h * hw + nope] = main[:, :nope].astype(BF16)
        o_ref[:, h * hw + nope:(h + 1) * hw] = (main[:, nope:] * cos + rot * sin).astype(BF16)


def _qproj(lat, g_q, wq, wrot, cos, sin, *, q_lora, heads, nope):
    t = lat.shape[0]
    tm = _pick(t, (512, 256, 128, 64))
    hw = nope + LANE
    body = functools.partial(_qproj_body, heads=heads, nope=nope)
    return pl.pallas_call(
        body,
        grid=(t // tm,),
        in_specs=[
            pl.BlockSpec((tm, q_lora), lambda i: (i, 0)),
            pl.BlockSpec((1, q_lora), lambda i: (0, 0)),
            pl.BlockSpec((q_lora, heads * hw), lambda i: (0, 0)),
            pl.BlockSpec((q_lora, heads * LANE), lambda i: (0, 0)),
            pl.BlockSpec((tm, LANE), lambda i: (i, 0)),
            pl.BlockSpec((tm, LANE), lambda i: (i, 0)),
        ],
        out_specs=pl.BlockSpec((tm, heads * hw), lambda i: (i, 0)),
        out_shape=jax.ShapeDtypeStruct((t, heads * hw), BF16),
        compiler_params=_params("parallel"),
        name="q_proj",
    )(lat, g_q, wq, wrot, cos, sin)


def _attn_prompt_body(q_ref, k_ref, v_ref, kr_ref, o_ref, kc_ref, *, tq, nope, rope, scale):
    seq = k_ref.shape[0]
    kc_ref[:, :nope] = k_ref[...]
    kc_ref[:, nope:nope + rope] = kr_ref[...].astype(BF16)
    kc_ref[:, nope + rope:] = jnp.zeros((seq, kc_ref.shape[1] - nope - rope), BF16)
    q_chunk = lax.broadcasted_iota(jnp.int32, (tq, tq), 0) // CHUNK
    k_chunk = lax.broadcasted_iota(jnp.int32, (tq, tq), 1) // CHUNK
    visible = k_chunk <= q_chunk
    los = [qi * tq for qi in range(seq // tq)]
    s_diag = [jnp.where(visible, _dot_nt(q_ref[lo:lo + tq, :], kc_ref[lo:lo + tq, :]) * scale, NEG_INF)
              for lo in los]
    s_off = [_dot_nt(q_ref[lo:lo + tq, :], kc_ref[0:lo, :]) * scale if lo else None for lo in los]
    probs = []
    for s_d, s_o in zip(s_diag, s_off):
        m = jnp.max(s_d, axis=-1, keepdims=True)
        if s_o is not None:
            m = jnp.maximum(m, jnp.max(s_o, axis=-1, keepdims=True))
        p_d = jnp.exp(s_d - m)
        l = jnp.sum(p_d, axis=-1, keepdims=True)
        p_o = None
        if s_o is not None:
            p_o = jnp.exp(s_o - m)
            l = l + jnp.sum(p_o, axis=-1, keepdims=True)
            p_o = p_o.astype(BF16)
        probs.append((p_d.astype(BF16), p_o, l))
    for lo, (p_d, p_o, l) in zip(los, probs):
        acc = _dot(p_d, v_ref[lo:lo + tq, :])
        if p_o is not None:
            acc = acc + _dot(p_o, v_ref[0:lo, :])
        o_ref[lo:lo + tq, :] = (acc / l).astype(o_ref.dtype)


def _attn_prompt(q_cat, kv_exp, kr, *, out_rows, batch, seq, heads, nope, rope, vdim, scale):
    tq = _pick(seq, (512, 256, 128, 64))
    hw = nope + LANE
    body = functools.partial(_attn_prompt_body, tq=tq, nope=nope, rope=rope, scale=scale)
    return pl.pallas_call(
        body,
        grid=(batch, heads),
        in_specs=[
            pl.BlockSpec((seq, hw), lambda b, h: (b, h)),
            pl.BlockSpec((seq, nope), lambda b, h: (b, h)),
            pl.BlockSpec((seq, vdim), lambda b, h: (b, heads * nope // vdim + h)),
            pl.BlockSpec((seq, rope), lambda b, h: (b, 0)),
        ],
        out_specs=pl.BlockSpec((seq, vdim), lambda b, h: (b, h)),
        out_shape=jax.ShapeDtypeStruct((out_rows, heads * vdim), BF16),
        scratch_shapes=[pltpu.VMEM((seq, hw), BF16)],
        compiler_params=_params("parallel", "arbitrary"),
        name="attn_prompt",
    )(q_cat, kv_exp, kv_exp, kr)


def _attn_sample_body(q_ref, cc_ref, ck_ref, nc_ref, nk_ref, wuk_ref, wuv_ref, prev_ref, o_ref,
                      qlat_ref, qr_ref, m_ref, l_ref, acc_ref, *, heads, nope, rope, vdim, tk, scale):
    del prev_ref
    ls = q_ref.shape[0]
    past = cc_ref.shape[0]
    hw = nope + LANE
    for h in range(heads):
        qh = q_ref[:, h * hw:(h + 1) * hw]
        qlat_ref[h * ls:(h + 1) * ls, :] = _dot_nt(qh[:, :nope], wuk_ref[:, h * nope:(h + 1) * nope]).astype(BF16)
        qr_ref[h * ls:(h + 1) * ls, :] = qh[:, nope:nope + rope]
    m_ref[...] = jnp.full(m_ref.shape, NEG_INF, F32)
    l_ref[...] = jnp.zeros(l_ref.shape, F32)
    acc_ref[...] = jnp.zeros(acc_ref.shape, F32)

    nrow = heads * ls
    hr = _pick(nrow, (256, 128, 64))

    def step(k, kr):
        groups = [slice(r0, r0 + hr) for r0 in range(0, nrow, hr)]
        scores = [(_dot_nt(qlat_ref[rows, :], k) + _dot_nt(qr_ref[rows, :], kr)) * scale for rows in groups]
        probs, alphas = [], []
        for rows, s in zip(groups, scores):
            m = m_ref[rows, :]
            m_new = jnp.maximum(m, jnp.max(s, axis=-1, keepdims=True))
            alpha = jnp.exp(m - m_new)
            p = jnp.exp(s - m_new)
            l_ref[rows, :] = alpha * l_ref[rows, :] + jnp.sum(p, axis=-1, keepdims=True)
            m_ref[rows, :] = m_new
            probs.append(p.astype(BF16))
            alphas.append(alpha)
        for rows, p, alpha in zip(groups, probs, alphas):
            acc_ref[rows, :] = alpha * acc_ref[rows, :] + _dot(p, k)

    def body(kb, carry):
        off = pl.multiple_of(kb * tk, tk)
        step(cc_ref[pl.ds(off, tk), :].astype(BF16), ck_ref[pl.ds(off, tk), :].astype(BF16))
        return carry

    lax.fori_loop(0, past // tk, body, 0)
    step(nc_ref[...].astype(BF16), nk_ref[...].astype(BF16))
    o_lat = (acc_ref[...] / l_ref[...]).astype(BF16)
    for h in range(heads):
        o_ref[:, h * vdim:(h + 1) * vdim] = _dot(
            o_lat[h * ls:(h + 1) * ls, :], wuv_ref[:, h * vdim:(h + 1) * vdim]).astype(o_ref.dtype)


def _attn_sample(q_cat, cache_ckv, cache_kr, ckv, kr, wuk, wuv, o_prev, *, layer, row0, batch, ls, heads,
                 nope, rope, vdim, scale):
    past, kvl = cache_ckv.shape[2], cache_ckv.shape[3]
    hw = nope + LANE
    tk = _pick(past, (512, 256, 128, 64))
    blk0 = row0 // ls
    body = functools.partial(_attn_sample_body, heads=heads, nope=nope, rope=rope, vdim=vdim, tk=tk,
                             scale=scale)
    return pl.pallas_call(
        body,
        grid=(batch,),
        in_specs=[
            pl.BlockSpec((ls, heads * hw), lambda b: (b, 0)),
            pl.BlockSpec((None, None, past, kvl), lambda b: (layer, b, 0, 0)),
            pl.BlockSpec((None, None, past, rope), lambda b: (layer, b, 0, 0)),
            pl.BlockSpec((ls, kvl), lambda b: (b, 0)),
            pl.BlockSpec((ls, rope), lambda b: (b, 0)),
            pl.BlockSpec((kvl, heads * nope), lambda b: (0, 0)),
            pl.BlockSpec((kvl, heads * vdim), lambda b: (0, 0)),
            pl.BlockSpec(memory_space=pl.ANY),
        ],
        out_specs=pl.BlockSpec((ls, heads * vdim), lambda b: (blk0 + b, 0)),
        out_shape=jax.ShapeDtypeStruct(o_prev.shape, BF16),
        scratch_shapes=[
            pltpu.VMEM((heads * ls, kvl), BF16),
            pltpu.VMEM((heads * ls, rope), BF16),
            pltpu.VMEM((heads * ls, 1), F32),
            pltpu.VMEM((heads * ls, 1), F32),
            pltpu.VMEM((heads * ls, kvl), F32),
        ],
        input_output_aliases={7: 0},
        compiler_params=_params("parallel"),
        name="attn_sample",
    )(q_cat, cache_ckv, cache_kr, ckv, kr, wuk, wuv, o_prev)


S5_PAIRS = 4


def _s5_body(u_ref, h0r_ref, h0i_ref, ar_ref, ai_ref, wb_ref, wc_ref, d_ref,
             z_ref, sr_ref, si_ref, bu_ref, *, tl):
    t = pl.program_id(2)

    @pl.when(t == 0)
    def _():
        sr_ref[...] = h0r_ref[...]
        si_ref[...] = h0i_ref[...]

    u = u_ref[...].reshape(tl * SUBLANE, LANE)
    bu_ref[...] = _dot(u.astype(BF16), wb_ref[0])
    a_re = [jnp.broadcast_to(ar_ref[0, p:p + 1, :], (SUBLANE, LANE)) for p in range(S5_PAIRS)]
    a_im = [jnp.broadcast_to(ai_ref[0, p:p + 1, :], (SUBLANE, LANE)) for p in range(S5_PAIRS)]

    def step(l, carry):
        row = pl.multiple_of(l * SUBLANE, SUBLANE)
        out = []
        for p in range(S5_PAIRS):
            s_re, s_im = carry[2 * p], carry[2 * p + 1]
            c0 = 2 * p * LANE
            n_re = a_re[p] * s_re - a_im[p] * s_im + bu_ref[pl.ds(row, SUBLANE), c0:c0 + LANE]
            n_im = a_re[p] * s_im + a_im[p] * s_re + bu_ref[pl.ds(row, SUBLANE), c0 + LANE:c0 + 2 * LANE]
            bu_ref[pl.ds(row, SUBLANE), c0:c0 + LANE] = n_re
            bu_ref[pl.ds(row, SUBLANE), c0 + LANE:c0 + 2 * LANE] = n_im
            out += [n_re, n_im]
        return tuple(out)

    init = []
    for p in range(S5_PAIRS):
        init += [sr_ref[:, p * LANE:(p + 1) * LANE], si_ref[:, p * LANE:(p + 1) * LANE]]
    fin = lax.fori_loop(0, tl, step, tuple(init), unroll=4)
    for p in range(S5_PAIRS):
        sr_ref[:, p * LANE:(p + 1) * LANE] = fin[2 * p]
        si_ref[:, p * LANE:(p + 1) * LANE] = fin[2 * p + 1]
    y = _dot(bu_ref[...].astype(BF16), wc_ref[0]) + d_ref[...] * u
    z_ref[...] = jax.nn.gelu(y).reshape(tl, SUBLANE, LANE)


def _s5(u_tm, h0_re, h0_im, a_re, a_im, wb, wc, d_skip):
    l, b, w = u_tm.shape
    nblk = w // LANE
    sw = h0_re.shape[1] // nblk
    tl = _pick(l, (256, 128, 64))
    body = functools.partial(_s5_body, tl=tl)
    return pl.pallas_call(
        body,
        grid=(b // SUBLANE, nblk, l // tl),
        in_specs=[
            pl.BlockSpec((tl, SUBLANE, LANE), lambda g, q, t: (t, g, q)),
            pl.BlockSpec((SUBLANE, sw), lambda g, q, t: (g, q)),
            pl.BlockSpec((SUBLANE, sw), lambda g, q, t: (g, q)),
            pl.BlockSpec((1, S5_PAIRS, LANE), lambda g, q, t: (q, 0, 0)),
            pl.BlockSpec((1, S5_PAIRS, LANE), lambda g, q, t: (q, 0, 0)),
            pl.BlockSpec((1, LANE, 2 * sw), lambda g, q, t: (q, 0, 0)),
            pl.BlockSpec((1, 2 * sw, LANE), lambda g, q, t: (q, 0, 0)),
            pl.BlockSpec((1, LANE), lambda g, q, t: (0, q)),
        ],
        out_specs=[
            pl.BlockSpec((tl, SUBLANE, LANE), lambda g, q, t: (t, g, q)),
            pl.BlockSpec((SUBLANE, sw), lambda g, q, t: (g, q)),
            pl.BlockSpec((SUBLANE, sw), lambda g, q, t: (g, q)),
        ],
        out_shape=[
            jax.ShapeDtypeStruct((l, b, w), F32),
            jax.ShapeDtypeStruct(h0_re.shape, F32),
            jax.ShapeDtypeStruct(h0_im.shape, F32),
        ],
        scratch_shapes=[pltpu.VMEM((tl * SUBLANE, 2 * sw), F32)],
        compiler_params=_params("parallel", "parallel", "arbitrary"),
        name="s5",
    )(u_tm, h0_re, h0_im, a_re, a_im, wb, wc, d_skip)


def _s5_weights(a_re, a_im, log_dt, b_re, b_im, c_re, c_im):
    g, p = a_re.shape
    grp = b_re.shape[2]
    gpb = LANE // grp
    nblk = g // gpb
    a_re, a_im = a_re.astype(F32), a_im.astype(F32)
    dt = jnp.exp(log_dt.astype(F32))[:, None]
    mag = jnp.exp(a_re * dt)
    ab_re, ab_im = mag * jnp.cos(a_im * dt), mag * jnp.sin(a_im * dt)
    den = a_re * a_re + a_im * a_im
    q_re = ((ab_re - 1.0) * a_re + ab_im * a_im) / den
    q_im = (ab_im * a_re - (ab_re - 1.0) * a_im) / den
    b_re, b_im = b_re.astype(F32), b_im.astype(F32)
    bb_re = q_re[..., None] * b_re - q_im[..., None] * b_im
    bb_im = q_re[..., None] * b_im + q_im[..., None] * b_re
    eye = jnp.eye(gpb, dtype=F32)

    def blockdiag_in(m):
        m = m.reshape(nblk, gpb, p, grp)
        return jnp.einsum("qjpc,jk->qjckp", m, eye).reshape(nblk, gpb * grp, gpb * p)

    def blockdiag_out(m):
        m = m.reshape(nblk, gpb, grp, p)
        return jnp.einsum("qjcp,jk->qjpkc", m, eye).reshape(nblk, gpb * p, gpb * grp)

    def interleave(re, im, axis):
        shp = list(re.shape)
        n = shp[axis] // LANE
        shp[axis:axis + 1] = [n, LANE]
        st = jnp.stack([re.reshape(shp), im.reshape(shp)], axis=axis + 1)
        shp[axis:axis + 2] = [2 * n * LANE]
        return st.reshape(shp)

    wb = interleave(blockdiag_in(bb_re), blockdiag_in(bb_im), 2).astype(BF16)
    wc = interleave(blockdiag_out(c_re.astype(F32)), blockdiag_out(-c_im.astype(F32)), 1).astype(BF16)
    pairs = g * p // LANE
    ar = ab_re.reshape(pairs // S5_PAIRS, S5_PAIRS, LANE)
    ai = ab_im.reshape(pairs // S5_PAIRS, S5_PAIRS, LANE)
    return ar, ai, wb, wc


def _pool_body(x_ref, g_ref, hist_ref, w_ref, sc_ref, o_ref, tail_ref, ext_ref, *, tl, nt, pos0, gc):
    t = pl.program_id(1)

    @pl.when(t == 0)
    def _():
        ext_ref[0:POOL_PAD, :] = hist_ref[...]

    @pl.when(t > 0)
    def _():
        ext_ref[0:POOL_PAD, :] = ext_ref[tl:tl + POOL_PAD, :]

    x = x_ref[...]
    ext_ref[POOL_PAD:POOL_PAD + tl, :] = _rms(x, g_ref[...])
    pos = pos0 + t * tl + lax.broadcasted_iota(jnp.int32, (tl, 1), 0)
    for gi, win in enumerate(POOL_WINDOWS):
        cols = slice(gi * gc, (gi + 1) * gc)
        cur = ext_ref[POOL_PAD:POOL_PAD + tl, cols]
        wsum = cur
        for k in range(1, win):
            wsum = wsum + ext_ref[POOL_PAD - k:POOL_PAD - k + tl, cols]
        count = jnp.minimum(pos + 1, win).astype(F32)
        delta = (wsum / count - cur).astype(BF16)
        o_ref[:, cols] = x[:, cols] + _dot(delta, w_ref[gi]) * sc_ref[:, cols]

    @pl.when(t == nt - 1)
    def _():
        tail_ref[...] = ext_ref[tl:tl + POOL_PAD, :]


def _pool(x, g, hist, w, scale, *, row0, batch, seq, pos0):
    t, d = x.shape
    gc = d // len(POOL_WINDOWS)
    tl = _pick(seq, (256, 128, 64))
    nt = seq // tl
    blk0 = row0 // tl
    body = functools.partial(_pool_body, tl=tl, nt=nt, pos0=pos0, gc=gc)
    return pl.pallas_call(
        body,
        grid=(batch, nt),
        in_specs=[
            pl.BlockSpec((tl, d), lambda b, i: (blk0 + b * nt + i, 0)),
            pl.BlockSpec((1, d), lambda b, i: (0, 0)),
            pl.BlockSpec((None, POOL_PAD, d), lambda b, i: (b, 0, 0)),
            pl.BlockSpec(w.shape, lambda b, i: (0, 0, 0)),
            pl.BlockSpec((1, d), lambda b, i: (0, 0)),
        ],
        out_specs=[
            pl.BlockSpec((tl, d), lambda b, i: (blk0 + b * nt + i, 0)),
            pl.BlockSpec((None, POOL_PAD, d), lambda b, i: (b, 0, 0)),
        ],
        out_shape=[
            jax.ShapeDtypeStruct((t, d), F32),
            jax.ShapeDtypeStruct((batch, POOL_PAD, d), F32),
        ],
        scratch_shapes=[pltpu.VMEM((POOL_PAD + tl, d), F32)],
        input_output_aliases={0: 0},
        compiler_params=_params("parallel", "arbitrary"),
        name="pool",
    )(x, g, hist, w, scale)


def _rot_cols(w):
    half = w.shape[-1] // 2
    return jnp.concatenate([-w[..., half:], w[..., :half]], axis=-1)


def _pad_cols(w, n):
    return jnp.pad(w, [(0, 0)] * (w.ndim - 1) + [(0, n - w.shape[-1])])


def kernel(x_prompt, x_sample, cache_mla_ckv, cache_mla_krope, state_s5_re, state_s5_im, cache_pool, ffn1_g, ffn1_w1, ffn1_w3, ffn1_w2, mix_g, ffn2_g, ffn2_w1, ffn2_w3, ffn2_w2, final_g, ab_w_in, mla_g_q, mla_g_kv, mla_w_uq, mla_w_uk, mla_w_uv, s5_a_re, s5_a_im, s5_log_dt, s5_b_re, s5_b_im, s5_c_re, s5_c_im, s5_d, s5_w_glu, ab_w_o, pool_w, pool_scale):
    bp, seq, d = x_prompt.shape
    bs, ls, _ = x_sample.shape
    tp, ts = bp * seq, bs * ls
    tt = tp + ts
    depth = ffn1_g.shape[0]
    past = cache_mla_ckv.shape[2]
    q_lora, heads, qk_head = mla_w_uq.shape[1:]
    kvl, _, nope = mla_w_uk.shape[1:]
    vdim = mla_w_uv.shape[3]
    rope = qk_head - nope
    s5w = s5_d.shape[1]
    groups, nstate = s5_a_re.shape[1:]
    hist_rows = cache_pool.shape[2]
    scale = qk_head ** -0.5
    assert nope == LANE and vdim == LANE and 2 * rope == LANE and kvl + 2 * LANE <= q_lora
    assert bp % SUBLANE == 0 and bs % SUBLANE == 0 and hist_rows == POOL_PAD - 1

    pos = jnp.concatenate([jnp.tile(jnp.arange(seq), bp), past + jnp.tile(jnp.arange(ls), bs)])
    inv = 1.0 / (ROPE_THETA ** (jnp.arange(0, rope, 2, dtype=F32) / rope))
    ang = pos.astype(F32)[:, None] * inv[None, :]
    cos = _pad_cols(jnp.tile(jnp.cos(ang), (1, 2)), LANE)
    sin = _pad_cols(jnp.tile(jnp.sin(ang), (1, 2)), LANE)

    row = lambda v: v.reshape(1, -1).astype(F32)
    fg = row(final_g)

    ffns = []
    for layer in range(depth):
        ffns.append((ffn1_g[layer], (ffn1_w1, ffn1_w3, ffn1_w2), layer))
        ffns.append((ffn2_g[layer], (ffn2_w1, ffn2_w3, ffn2_w2), layer))
    dff = ffn1_w1.shape[2]
    fp = -(-dff // FF_TILE) * FF_TILE
    s1, s3, s2 = ffns[0][1]
    w_next = (_cast_pad(s1, ffns[0][2], d, fp), _cast_pad(s3, ffns[0][2], d, fp),
              _cast_pad(s2, ffns[0][2], fp, d))

    def ffn_pair(k, xs):
        nonlocal w_next
        g = row(ffns[k][0])
        w1, w3, w2 = w_next
        ride = (*ffns[k + 1][1], ffns[k + 1][2], dff) if k + 1 < len(ffns) else None
        (xp, rp), (xs_, rs) = xs
        act_p = _ffn_up(xp, g, w1, w3, row0=rp, nrows=tp)
        act_s = _ffn_up(xs_, g, w1, w3, row0=rs, nrows=ts)
        y, w_next = _ffn_down(act_p, w2, xp, row0=rp, nrows=tp, out_rows=tt, ride=ride)
        y, _ = _ffn_down(act_s, w2, xs_, row0=rs, nrows=ts, out_rows=tt, out_row0=tp, out_alias=y)
        return y

    ckv_out, kr_out, s5re_out, s5im_out, pool_out = [], [], [], [], []
    x = None
    for layer in range(depth):
        src = ((x_prompt.reshape(tp, d), 0), (x_sample.reshape(ts, d), 0)) if x is None else ((x, 0), (x, tp))
        x = ffn_pair(2 * layer, src)
        if layer % 2 == 0:
            i = layer // 2
            w_in = ab_w_in[i]
            w_kr = w_in[:, q_lora + kvl:q_lora + kvl + rope]
            w_mid = jnp.concatenate([w_in[:, q_lora:q_lora + kvl], _pad_cols(w_kr, LANE),
                                     _pad_cols(_rot_cols(w_kr), LANE)], axis=1)
            w_proj = jnp.concatenate([w_in[:, :q_lora], _pad_cols(w_mid, q_lora),
                                      w_in[:, q_lora + kvl + rope:]], axis=1).astype(BF16)
            g_mix = row(mix_g[layer])
            lat_p, u_p = _inproj(x, g_mix, w_proj, n_lat=2 * q_lora, row0=0, nrows=tp)
            lat_s, u_s = _inproj(x, g_mix, w_proj, n_lat=2 * q_lora, row0=tp, nrows=ts)
            g_kv = row(mla_g_kv[i])
            cos_p, sin_p, cos_s, sin_s = cos[:tp], sin[:tp], cos[tp:], sin[tp:]
            ckv_p, kr_p = _kvpost(lat_p, g_kv, cos_p, sin_p, q_lora=q_lora, kv=kvl, rope=rope)
            ckv_s, kr_s = _kvpost(lat_s, g_kv, cos_s, sin_s, q_lora=q_lora, kv=kvl, rope=rope)

            w_uq = mla_w_uq[i]
            wq = jnp.concatenate([w_uq[..., :nope], _pad_cols(w_uq[..., nope:], LANE)], axis=-1)
            wq = wq.reshape(q_lora, -1).astype(BF16)
            wrot = _pad_cols(_rot_cols(w_uq[..., nope:]), LANE).reshape(q_lora, -1).astype(BF16)
            g_q = row(mla_g_q[i])
            q_p = _qproj(lat_p, g_q, wq, wrot, cos_p, sin_p, q_lora=q_lora, heads=heads, nope=nope)
            q_s = _qproj(lat_s, g_q, wq, wrot, cos_s, sin_s, q_lora=q_lora, heads=heads, nope=nope)

            wuk = mla_w_uk[i].reshape(kvl, heads * nope).astype(BF16)
            wuv = mla_w_uv[i].reshape(kvl, heads * vdim).astype(BF16)
            kv_exp = _mm(ckv_p, jnp.concatenate([wuk, wuv], axis=1), out_dtype=BF16, name="kv_expand")
            o_a = _attn_prompt(q_p, kv_exp, kr_p, out_rows=tt, batch=bp, seq=seq, heads=heads, nope=nope,
                               rope=rope, vdim=vdim, scale=scale)
            o_a = _attn_sample(q_s, cache_mla_ckv, cache_mla_krope, ckv_s, kr_s, wuk, wuv, o_a, layer=i,
                               row0=tp, batch=bs, ls=ls, heads=heads, nope=nope, rope=rope, vdim=vdim,
                               scale=scale)

            ar, ai, wb, wc = _s5_weights(s5_a_re[i], s5_a_im[i], s5_log_dt[i], s5_b_re[i], s5_b_im[i],
                                         s5_c_re[i], s5_c_im[i])
            u_p = u_p.reshape(bp, seq, s5w).swapaxes(0, 1)
            u_s = u_s.reshape(bs, ls, s5w).swapaxes(0, 1)
            zeros = jnp.zeros((bp, groups * nstate), F32)
            d_row = row(s5_d[i])
            z_p, pre, pim = _s5(u_p, zeros, zeros, ar, ai, wb, wc, d_row)
            z_s, sre, sim = _s5(u_s, state_s5_re[i].reshape(bs, -1).astype(F32),
                                state_s5_im[i].reshape(bs, -1).astype(F32), ar, ai, wb, wc, d_row)
            wglu = s5_w_glu[i].astype(BF16)
            z_p = z_p.reshape(tp, s5w)
            z_s = z_s.reshape(ts, s5w)
            ob_p = _mm(z_p, wglu, out_dtype=BF16, epilogue="glu", extra=z_p, name="s5_glu")
            ob_s = _mm(z_s, wglu, out_dtype=BF16, epilogue="glu", extra=z_s, name="s5_glu")
            o_b = jnp.concatenate([ob_p.reshape(seq, bp, s5w).swapaxes(0, 1).reshape(tp, s5w),
                                   ob_s.reshape(ls, bs, s5w).swapaxes(0, 1).reshape(ts, s5w)], axis=0)
            x = _outproj(o_a, o_b, ab_w_o[i].astype(BF16), x)

            ckv_out.append((ckv_p.reshape(bp, seq, kvl), ckv_s.reshape(bs, ls, kvl)))
            kr_out.append((kr_p.reshape(bp, seq, rope), kr_s.reshape(bs, ls, rope)))
            s5re_out.append((pre.reshape(bp, groups, nstate), sre.reshape(bs, groups, nstate)))
            s5im_out.append((pim.reshape(bp, groups, nstate), sim.reshape(bs, groups, nstate)))
        else:
            j = layer // 2
            wp = pool_w[j].astype(BF16)
            g_mix, sc = row(mix_g[layer]), row(pool_scale[j])
            hist_p = jnp.zeros((bp, POOL_PAD, d), F32)
            hist_s = jnp.pad(cache_pool[j].astype(F32), ((0, 0), (1, 0), (0, 0)))
            x, tail_p = _pool(x, g_mix, hist_p, wp, sc, row0=0, batch=bp, seq=seq, pos0=0)
            x, tail_s = _pool(x, g_mix, hist_s, wp, sc, row0=tp, batch=bs, seq=ls, pos0=past)
            pool_out.append((tail_p[:, 1:], tail_s[:, 1:]))
        x = ffn_pair(2 * layer + 1, ((x, 0), (x, tp)))

    y_p = _final_norm(x, fg, row0=0, nrows=tp)
    y_s = _final_norm(x, fg, row0=tp, nrows=ts)
    stack = lambda items, k: jnp.stack([it[k] for it in items])
    return (y_p.reshape(bp, seq, d), y_s.reshape(bs, ls, d),
            stack(ckv_out, 0), stack(kr_out, 0), stack(s5re_out, 0), stack(s5im_out, 0), stack(pool_out, 0),
            stack(ckv_out, 1), stack(kr_out, 1), stack(s5re_out, 1), stack(s5im_out, 1), stack(pool_out, 1))
```

```python
import functools
import math

import jax
import jax.numpy as jnp
from jax import lax
from jax.experimental import pallas as pl
from jax.experimental.pallas import tpu as pltpu

F32 = jnp.float32
BF16 = jnp.bfloat16
NORM_EPS = 1e-6
CHUNK = 64
ROPE_THETA = 10000.0
NEG_INF = -1e30
POOL_WINDOWS = (2, 4, 8, 16)
POOL_PAD = 16
LANE = 128
SUBLANE = 8
VMEM_LIMIT = 56 * 1024 * 1024


def _params(*sem):
    return pltpu.CompilerParams(dimension_semantics=sem, vmem_limit_bytes=VMEM_LIMIT)


def _pick(n, candidates):
    for c in candidates:
        if n % c == 0:
            return c
    return n


def _rms(x, g):
    y = x * lax.rsqrt(jnp.mean(x * x, axis=-1, keepdims=True) + NORM_EPS)
    return y * g


def _dot(a, b):
    return jnp.dot(a, b, preferred_element_type=F32)


def _dot_nt(a, b):
    return lax.dot_general(a, b, (((1,), (1,)), ((), ())), preferred_element_type=F32)


def _cast_body(w_ref, o_ref):
    o_ref[...] = w_ref[...].astype(BF16)


def _cast(w_stack, layer):
    _, r, c = w_stack.shape
    br = _pick(r, (256, 128, 64, 16))
    return pl.pallas_call(
        _cast_body,
        grid=(r // br,),
        in_specs=[pl.BlockSpec((None, br, c), lambda i: (layer, i, 0))],
        out_specs=pl.BlockSpec((br, c), lambda i: (i, 0)),
        out_shape=jax.ShapeDtypeStruct((r, c), BF16),
        compiler_params=_params("parallel"),
        name="cast",
    )(w_stack)


FF_TILE = 512


def _ffn_up_body(x_ref, g_ref, w1_ref, w3_ref, o_ref, h_ref, *, halves, last_cols):
    hr = o_ref.shape[0] // halves
    tn = o_ref.shape[1]

    @pl.when(pl.program_id(1) == 0)
    def _():
        for r0 in range(0, o_ref.shape[0], hr):
            h_ref[r0:r0 + hr, :] = _rms(x_ref[r0:r0 + hr, :], g_ref[...]).astype(BF16)

    def block(cols):
        for r0 in range(0, o_ref.shape[0], hr):
            h = h_ref[r0:r0 + hr, :]
            a = _dot(h, w1_ref[:, :cols])
            b = _dot(h, w3_ref[:, :cols])
            o_ref[r0:r0 + hr, :cols] = (a * jax.nn.sigmoid(a) * b).astype(BF16)

    if last_cols == tn:
        block(tn)
    else:
        last = pl.program_id(1) == pl.num_programs(1) - 1
        pl.when(jnp.logical_not(last))(lambda: block(tn))
        pl.when(last)(lambda: block(last_cols))


def _ffn_up(x, g, w1, w3, *, row0, nrows):
    d = x.shape[1]
    dff = w1.shape[1]
    tm = _pick(nrows, (1024, 512, 256, 128, 64))
    tn = FF_TILE
    nj = pl.cdiv(dff, tn)
    rb0 = row0 // tm
    body = functools.partial(_ffn_up_body, halves=max(1, tm // 256), last_cols=dff - (nj - 1) * tn)
    return pl.pallas_call(
        body,
        grid=(nrows // tm, nj),
        in_specs=[
            pl.BlockSpec((tm, d), lambda i, j: (rb0 + i, 0), pipeline_mode=pl.Buffered(1)),
            pl.BlockSpec((1, d), lambda i, j: (0, 0)),
            pl.BlockSpec((d, tn), lambda i, j: (0, j)),
            pl.BlockSpec((d, tn), lambda i, j: (0, j)),
        ],
        out_specs=pl.BlockSpec((tm, tn), lambda i, j: (i, j)),
        out_shape=jax.ShapeDtypeStruct((nrows, dff), BF16),
        scratch_shapes=[pltpu.VMEM((tm, d), BF16)],
        compiler_params=_params("parallel", "arbitrary"),
        name="ffn_up",
    )(x, g, w1, w3)


def _ffn_down_body(*refs, ride, aliased):
    refs = list(refs)
    act_ref, w2_ref, x_ref = refs[:3]
    del refs[:3]
    if ride:
        nxt = refs[:3]
        del refs[:3]
    if aliased:
        del refs[:1]
    o_ref = refs.pop(0)
    o_ref[...] = x_ref[...] + 0.5 * _dot(act_ref[...], w2_ref[...])
    if ride:
        for src, dst in zip(nxt, refs):
            dst[...] = src[...].astype(BF16)


def _round_up(n, m):
    return -(-n // m) * m


def _ffn_down(act, w2, x, *, row0, nrows, out_rows=None, out_row0=0, out_alias=None, ride=None):
    dff, d = w2.shape
    tm = _pick(nrows, (512, 256, 128, 64))
    tn = _pick(d, (512, 256, 128))
    ni, nj = nrows // tm, d // tn
    out_rows = nrows if out_rows is None else out_rows
    rb0, ob0 = row0 // tm, out_row0 // tm
    in_specs = [
        pl.BlockSpec((tm, dff), lambda j, i: (i, 0)),
        pl.BlockSpec((dff, tn), lambda j, i: (0, j), pipeline_mode=pl.Buffered(1)),
        pl.BlockSpec((tm, tn), lambda j, i: (rb0 + i, j)),
    ]
    args = [act, w2, x]
    out_specs = [pl.BlockSpec((tm, tn), lambda j, i: (ob0 + i, j))]
    out_shape = [jax.ShapeDtypeStruct((out_rows, d), F32)]
    if ride is not None:
        n1, n3, n2, layer = ride
        assert d % ni == 0 and d % nj == 0
        t13 = (d // ni, _round_up(pl.cdiv(dff, nj), LANE))
        t2 = (_round_up(pl.cdiv(dff, ni), 16), d // nj)
        assert t13[0] % 16 == 0 and t2[1] % LANE == 0
        assert (nj - 1) * t13[1] < dff and (ni - 1) * t2[0] < dff
        in_specs += [
            pl.BlockSpec((None,) + t13, lambda j, i: (layer, i, j)),
            pl.BlockSpec((None,) + t13, lambda j, i: (layer, i, j)),
            pl.BlockSpec((None,) + t2, lambda j, i: (layer, i, j)),
        ]
        args += [n1, n3, n2]
        out_specs += [pl.BlockSpec(t13, lambda j, i: (i, j)), pl.BlockSpec(t13, lambda j, i: (i, j)),
                      pl.BlockSpec(t2, lambda j, i: (i, j))]
        out_shape += [jax.ShapeDtypeStruct((d, dff), BF16), jax.ShapeDtypeStruct((d, dff), BF16),
                      jax.ShapeDtypeStruct((dff, d), BF16)]
    aliases = {}
    if out_alias is not None:
        aliases = {len(args): 0}
        in_specs.append(pl.BlockSpec(memory_space=pl.ANY))
        args.append(out_alias)
    body = functools.partial(_ffn_down_body, ride=ride is not None, aliased=out_alias is not None)
    outs = pl.pallas_call(
        body,
        grid=(nj, ni),
        in_specs=in_specs,
        out_specs=out_specs,
        out_shape=out_shape,
        input_output_aliases=aliases,
        compiler_params=_params("arbitrary", "arbitrary"),
        name="ffn_down",
    )(*args)
    return outs[0], tuple(outs[1:])


def _final_norm_body(x_ref, g_ref, o_ref):
    o_ref[...] = _rms(x_ref[...], g_ref[...])


def _final_norm(x, g, *, row0, nrows):
    d = x.shape[1]
    tm = _pick(nrows, (512, 256, 128, 64))
    rb0 = row0 // tm
    return pl.pallas_call(
        _final_norm_body,
        grid=(nrows // tm,),
        in_specs=[pl.BlockSpec((tm, d), lambda i: (rb0 + i, 0)), pl.BlockSpec((1, d), lambda i: (0, 0))],
        out_specs=pl.BlockSpec((tm, d), lambda i: (i, 0)),
        out_shape=jax.ShapeDtypeStruct((nrows, d), F32),
        compiler_params=_params("parallel"),
        name="final_norm",
    )(x, g)


def _mm_body(*refs, use_scratch, epilogue):
    refs = list(refs)
    lhs_ref = refs.pop(0)
    w_ref = refs.pop(0)
    e_ref = refs.pop(0) if epilogue is not None else None
    o_ref = refs.pop(0)
    if use_scratch:
        s_ref = refs.pop(0)

        @pl.when(pl.program_id(1) == 0)
        def _():
            s_ref[...] = lhs_ref[...].astype(BF16)

        lhs = s_ref[...]
    else:
        lhs = lhs_ref[...]
    acc = _dot(lhs, w_ref[...])
    if epilogue == "glu":
        acc = e_ref[...] * jax.nn.sigmoid(acc)
    o_ref[...] = acc.astype(o_ref.dtype)


def _mm(lhs, w, *, out_dtype, epilogue=None, extra=None, name):
    t = lhs.shape[0]
    k, n = w.shape
    tm = _pick(t, (512, 256, 128, 64))
    tn = _pick(n, (1024, 512, 256, 128))
    use_scratch = lhs.dtype != BF16
    in_specs = [pl.BlockSpec((tm, k), lambda i, j: (i, 0)),
                pl.BlockSpec((k, tn), lambda i, j: (0, j))]
    args = [lhs, w]
    if epilogue is not None:
        in_specs.append(pl.BlockSpec((tm, tn), lambda i, j: (i, j)))
        args.append(extra)
    body = functools.partial(_mm_body, use_scratch=use_scratch, epilogue=epilogue)
    return pl.pallas_call(
        body,
        grid=(t // tm, n // tn),
        in_specs=in_specs,
        out_specs=pl.BlockSpec((tm, tn), lambda i, j: (i, j)),
        out_shape=jax.ShapeDtypeStruct((t, n), out_dtype),
        scratch_shapes=[pltpu.VMEM((tm, k), BF16)] if use_scratch else [],
        compiler_params=_params("parallel", "arbitrary"),
        name=name,
    )(*args)


def _inproj_body(x_ref, g_ref, w_ref, a_ref, u_ref, h_ref, *, na):
    j = pl.program_id(1)

    @pl.when(j == 0)
    def _():
        h_ref[...] = _rms(x_ref[...], g_ref[...]).astype(BF16)

    @pl.when(j < na)
    def _():
        a_ref[...] = _dot(h_ref[...], w_ref[...])

    @pl.when(j >= na)
    def _():
        u_ref[...] = _dot(h_ref[...], w_ref[...])


def _inproj(x, g, w, *, n_lat, row0, nrows):
    d = x.shape[1]
    n = w.shape[1]
    t = nrows
    tm = _pick(t, (512, 256, 128, 64))
    tn = math.gcd(_pick(n_lat, (512, 256, 128)), _pick(n - n_lat, (512, 256, 128)))
    na = n_lat // tn
    rb0 = row0 // tm
    body = functools.partial(_inproj_body, na=na)
    return pl.pallas_call(
        body,
        grid=(t // tm, n // tn),
        in_specs=[
            pl.BlockSpec((tm, d), lambda i, j: (rb0 + i, 0)),
            pl.BlockSpec((1, d), lambda i, j: (0, 0)),
            pl.BlockSpec((d, tn), lambda i, j: (0, j)),
        ],
        out_specs=[
            pl.BlockSpec((tm, tn), lambda i, j: (i, jnp.minimum(j, na - 1))),
            pl.BlockSpec((tm, tn), lambda i, j: (i, jnp.maximum(j - na, 0))),
        ],
        out_shape=[
            jax.ShapeDtypeStruct((t, n_lat), F32),
            jax.ShapeDtypeStruct((t, n - n_lat), F32),
        ],
        scratch_shapes=[pltpu.VMEM((tm, d), BF16)],
        compiler_params=_params("parallel", "arbitrary"),
        name="in_proj",
    )(x, g, w)


def _outproj_body(oa_ref, ob_ref, wa_ref, wb_ref, r_ref, o_ref):
    o_ref[...] = r_ref[...] + _dot(oa_ref[...], wa_ref[...]) + _dot(ob_ref[...], wb_ref[...])


def _outproj(oa, ob, w, res):
    t, ka = oa.shape
    kb = ob.shape[1]
    n = w.shape[1]
    assert ka == kb
    tm = _pick(t, (512, 256, 128, 64))
    tn = _pick(n, (1024, 512, 256, 128))
    return pl.pallas_call(
        _outproj_body,
        grid=(t // tm, n // tn),
        in_specs=[
            pl.BlockSpec((tm, ka), lambda i, j: (i, 0)),
            pl.BlockSpec((tm, kb), lambda i, j: (i, 0)),
            pl.BlockSpec((ka, tn), lambda i, j: (0, j)),
            pl.BlockSpec((kb, tn), lambda i, j: (1, j)),
            pl.BlockSpec((tm, tn), lambda i, j: (i, j)),
        ],
        out_specs=pl.BlockSpec((tm, tn), lambda i, j: (i, j)),
        out_shape=jax.ShapeDtypeStruct((t, n), F32),
        compiler_params=_params("parallel", "arbitrary"),
        name="out_proj",
    )(oa, ob, w, w, res)


def _kvpost_body(p_ref, g_ref, cos_ref, sin_ref, ckv_ref, kr_ref, *, kv, rope):
    ckv_ref[...] = _rms(p_ref[:, :kv], g_ref[...])
    k = p_ref[:, kv:kv + rope]
    k_rot = p_ref[:, kv + LANE:kv + LANE + rope]
    kr_ref[...] = k * cos_ref[:, :rope] + k_rot * sin_ref[:, :rope]


def _kvpost(lat, g_kv, cos, sin, *, q_lora, kv, rope):
    nrows = lat.shape[0]
    tm = _pick(nrows, (512, 256, 128, 64))
    body = functools.partial(_kvpost_body, kv=kv, rope=rope)
    return pl.pallas_call(
        body,
        grid=(nrows // tm,),
        in_specs=[
            pl.BlockSpec((tm, q_lora), lambda i: (i, 1)),
            pl.BlockSpec((1, kv), lambda i: (0, 0)),
            pl.BlockSpec((tm, LANE), lambda i: (i, 0)),
            pl.BlockSpec((tm, LANE), lambda i: (i, 0)),
        ],
        out_specs=[
            pl.BlockSpec((tm, kv), lambda i: (i, 0)),
            pl.BlockSpec((tm, rope), lambda i: (i, 0)),
        ],
        out_shape=[
            jax.ShapeDtypeStruct((nrows, kv), F32),
            jax.ShapeDtypeStruct((nrows, rope), F32),
        ],
        compiler_params=_params("parallel"),
        name="kv_post",
    )(lat, g_kv, cos, sin)


def _qproj_body(c_ref, g_ref, wq_ref, wrot_ref, cos_ref, sin_ref, o_ref, *, heads, nope):
    c = _rms(c_ref[...], g_ref[...]).astype(BF16)
    cos = cos_ref[...]
    sin = sin_ref[...]
    hw = nope + LANE
    for h in range(heads):
        main = _dot(c, wq_ref[:, h * hw:(h + 1) * hw])
        rot = _dot(c, wrot_ref[:, h * LANE:(h + 1) * LANE])
        o_ref[:, h * hw:h * hw + nope] = main[:, :nope].astype(BF16)
        o_ref[:, h * hw + nope:(h + 1) * hw] = (main[:, nope:] * cos + rot * sin).astype(BF16)


def _qproj(lat, g_q, wq, wrot, cos, sin, *, q_lora, heads, nope):
    t = lat.shape[0]
    tm = _pick(t, (512, 256, 128, 64))
    hw = nope + LANE
    body = functools.partial(_qproj_body, heads=heads, nope=nope)
    return pl.pallas_call(
        body,
        grid=(t // tm,),
        in_specs=[
            pl.BlockSpec((tm, q_lora), lambda i: (i, 0)),
            pl.BlockSpec((1, q_lora), lambda i: (0, 0)),
            pl.BlockSpec((q_lora, heads * hw), lambda i: (0, 0)),
            pl.BlockSpec((q_lora, heads * LANE), lambda i: (0, 0)),
            pl.BlockSpec((tm, LANE), lambda i: (i, 0)),
            pl.BlockSpec((tm, LANE), lambda i: (i, 0)),
        ],
        out_specs=pl.BlockSpec((tm, heads * hw), lambda i: (i, 0)),
        out_shape=jax.ShapeDtypeStruct((t, heads * hw), BF16),
        compiler_params=_params("parallel"),
        name="q_proj",
    )(lat, g_q, wq, wrot, cos, sin)


def _attn_prompt_body(q_ref, k_ref, v_ref, kr_ref, o_ref, kc_ref, *, tq, nope, rope, scale):
    seq = k_ref.shape[0]
    kc_ref[:, :nope] = k_ref[...]
    kc_ref[:, nope:nope + rope] = kr_ref[...].astype(BF16)
    kc_ref[:, nope + rope:] = jnp.zeros((seq, kc_ref.shape[1] - nope - rope), BF16)
    q_chunk = lax.broadcasted_iota(jnp.int32, (tq, tq), 0) // CHUNK
    k_chunk = lax.broadcasted_iota(jnp.int32, (tq, tq), 1) // CHUNK
    visible = k_chunk <= q_chunk
    los = [qi * tq for qi in range(seq // tq)]
    s_diag = [jnp.where(visible, _dot_nt(q_ref[lo:lo + tq, :], kc_ref[lo:lo + tq, :]) * scale, NEG_INF)
              for lo in los]
    s_off = [_dot_nt(q_ref[lo:lo + tq, :], kc_ref[0:lo, :]) * scale if lo else None for lo in los]
    probs = []
    for s_d, s_o in zip(s_diag, s_off):
        m = jnp.max(s_d, axis=-1, keepdims=True)
        if s_o is not None:
            m = jnp.maximum(m, jnp.max(s_o, axis=-1, keepdims=True))
        p_d = jnp.exp(s_d - m)
        l = jnp.sum(p_d, axis=-1, keepdims=True)
        p_o = None
        if s_o is not None:
            p_o = jnp.exp(s_o - m)
            l = l + jnp.sum(p_o, axis=-1, keepdims=True)
            p_o = p_o.astype(BF16)
        probs.append((p_d.astype(BF16), p_o, l))
    for lo, (p_d, p_o, l) in zip(los, probs):
        acc = _dot(p_d, v_ref[lo:lo + tq, :])
        if p_o is not None:
            acc = acc + _dot(p_o, v_ref[0:lo, :])
        o_ref[lo:lo + tq, :] = (acc / l).astype(o_ref.dtype)


def _attn_prompt(q_cat, kv_exp, kr, *, out_rows, batch, seq, heads, nope, rope, vdim, scale):
    tq = _pick(seq, (512, 256, 128, 64))
    hw = nope + LANE
    body = functools.partial(_attn_prompt_body, tq=tq, nope=nope, rope=rope, scale=scale)
    return pl.pallas_call(
        body,
        grid=(batch, heads),
        in_specs=[
            pl.BlockSpec((seq, hw), lambda b, h: (b, h)),
            pl.BlockSpec((seq, nope), lambda b, h: (b, h)),
            pl.BlockSpec((seq, vdim), lambda b, h: (b, heads * nope // vdim + h)),
            pl.BlockSpec((seq, rope), lambda b, h: (b, 0)),
        ],
        out_specs=pl.BlockSpec((seq, vdim), lambda b, h: (b, h)),
        out_shape=jax.ShapeDtypeStruct((out_rows, heads * vdim), BF16),
        scratch_shapes=[pltpu.VMEM((seq, hw), BF16)],
        compiler_params=_params("parallel", "arbitrary"),
        name="attn_prompt",
    )(q_cat, kv_exp, kv_exp, kr)


def _attn_sample_body(q_ref, cc_ref, ck_ref, nc_ref, nk_ref, wuk_ref, wuv_ref, prev_ref, o_ref,
                      qlat_ref, qr_ref, m_ref, l_ref, acc_ref, *, heads, nope, rope, vdim, tk, scale):
    del prev_ref
    ls = q_ref.shape[0]
    past = cc_ref.shape[0]
    hw = nope + LANE
    for h in range(heads):
        qh = q_ref[:, h * hw:(h + 1) * hw]
        qlat_ref[h * ls:(h + 1) * ls, :] = _dot_nt(qh[:, :nope], wuk_ref[:, h * nope:(h + 1) * nope]).astype(BF16)
        qr_ref[h * ls:(h + 1) * ls, :] = qh[:, nope:nope + rope]
    m_ref[...] = jnp.full(m_ref.shape, NEG_INF, F32)
    l_ref[...] = jnp.zeros(l_ref.shape, F32)
    acc_ref[...] = jnp.zeros(acc_ref.shape, F32)

    nrow = heads * ls
    hr = _pick(nrow, (256, 128, 64))

    def step(k, kr):
        groups = [slice(r0, r0 + hr) for r0 in range(0, nrow, hr)]
        scores = [(_dot_nt(qlat_ref[rows, :], k) + _dot_nt(qr_ref[rows, :], kr)) * scale for rows in groups]
        probs, alphas = [], []
        for rows, s in zip(groups, scores):
            m = m_ref[rows, :]
            m_new = jnp.maximum(m, jnp.max(s, axis=-1, keepdims=True))
            alpha = jnp.exp(m - m_new)
            p = jnp.exp(s - m_new)
            l_ref[rows, :] = alpha * l_ref[rows, :] + jnp.sum(p, axis=-1, keepdims=True)
            m_ref[rows, :] = m_new
            probs.append(p.astype(BF16))
            alphas.append(alpha)
        for rows, p, alpha in zip(groups, probs, alphas):
            acc_ref[rows, :] = alpha * acc_ref[rows, :] + _dot(p, k)

    def body(kb, carry):
        off = pl.multiple_of(kb * tk, tk)
        step(cc_ref[pl.ds(off, tk), :].astype(BF16), ck_ref[pl.ds(off, tk), :].astype(BF16))
        return carry

    lax.fori_loop(0, past // tk, body, 0)
    step(nc_ref[...].astype(BF16), nk_ref[...].astype(BF16))
    o_lat = (acc_ref[...] / l_ref[...]).astype(BF16)
    for h in range(heads):
        o_ref[:, h * vdim:(h + 1) * vdim] = _dot(
            o_lat[h * ls:(h + 1) * ls, :], wuv_ref[:, h * vdim:(h + 1) * vdim]).astype(o_ref.dtype)


def _attn_sample(q_cat, cache_ckv, cache_kr, ckv, kr, wuk, wuv, o_prev, *, layer, row0, batch, ls, heads,
                 nope, rope, vdim, scale):
    past, kvl = cache_ckv.shape[2], cache_ckv.shape[3]
    hw = nope + LANE
    tk = _pick(past, (512, 256, 128, 64))
    blk0 = row0 // ls
    body = functools.partial(_attn_sample_body, heads=heads, nope=nope, rope=rope, vdim=vdim, tk=tk,
                             scale=scale)
    return pl.pallas_call(
        body,
        grid=(batch,),
        in_specs=[
            pl.BlockSpec((ls, heads * hw), lambda b: (b, 0)),
            pl.BlockSpec((None, None, past, kvl), lambda b: (layer, b, 0, 0)),
            pl.BlockSpec((None, None, past, rope), lambda b: (layer, b, 0, 0)),
            pl.BlockSpec((ls, kvl), lambda b: (b, 0)),
            pl.BlockSpec((ls, rope), lambda b: (b, 0)),
            pl.BlockSpec((kvl, heads * nope), lambda b: (0, 0)),
            pl.BlockSpec((kvl, heads * vdim), lambda b: (0, 0)),
            pl.BlockSpec(memory_space=pl.ANY),
        ],
        out_specs=pl.BlockSpec((ls, heads * vdim), lambda b: (blk0 + b, 0)),
        out_shape=jax.ShapeDtypeStruct(o_prev.shape, BF16),
        scratch_shapes=[
            pltpu.VMEM((heads * ls, kvl), BF16),
            pltpu.VMEM((heads * ls, rope), BF16),
            pltpu.VMEM((heads * ls, 1), F32),
            pltpu.VMEM((heads * ls, 1), F32),
            pltpu.VMEM((heads * ls, kvl), F32),
        ],
        input_output_aliases={7: 0},
        compiler_params=_params("parallel"),
        name="attn_sample",
    )(q_cat, cache_ckv, cache_kr, ckv, kr, wuk, wuv, o_prev)


S5_PAIRS = 4


def _s5_body(u_ref, h0r_ref, h0i_ref, ar_ref, ai_ref, wb_ref, wc_ref, d_ref,
             z_ref, sr_ref, si_ref, bu_ref, *, tl):
    t = pl.program_id(2)

    @pl.when(t == 0)
    def _():
        sr_ref[...] = h0r_ref[...]
        si_ref[...] = h0i_ref[...]

    u = u_ref[...].reshape(tl * SUBLANE, LANE)
    bu_ref[...] = _dot(u.astype(BF16), wb_ref[0])
    a_re = [jnp.broadcast_to(ar_ref[0, p:p + 1, :], (SUBLANE, LANE)) for p in range(S5_PAIRS)]
    a_im = [jnp.broadcast_to(ai_ref[0, p:p + 1, :], (SUBLANE, LANE)) for p in range(S5_PAIRS)]

    def step(l, carry):
        row = pl.multiple_of(l * SUBLANE, SUBLANE)
        out = []
        for p in range(S5_PAIRS):
            s_re, s_im = carry[2 * p], carry[2 * p + 1]
            c0 = 2 * p * LANE
            n_re = a_re[p] * s_re - a_im[p] * s_im + bu_ref[pl.ds(row, SUBLANE), c0:c0 + LANE]
            n_im = a_re[p] * s_im + a_im[p] * s_re + bu_ref[pl.ds(row, SUBLANE), c0 + LANE:c0 + 2 * LANE]
            bu_ref[pl.ds(row, SUBLANE), c0:c0 + LANE] = n_re
            bu_ref[pl.ds(row, SUBLANE), c0 + LANE:c0 + 2 * LANE] = n_im
            out += [n_re, n_im]
        return tuple(out)

    init = []
    for p in range(S5_PAIRS):
        init += [sr_ref[:, p * LANE:(p + 1) * LANE], si_ref[:, p * LANE:(p + 1) * LANE]]
    fin = lax.fori_loop(0, tl, step, tuple(init), unroll=4)
    for p in range(S5_PAIRS):
        sr_ref[:, p * LANE:(p + 1) * LANE] = fin[2 * p]
        si_ref[:, p * LANE:(p + 1) * LANE] = fin[2 * p + 1]
    y = _dot(bu_ref[...].astype(BF16), wc_ref[0]) + d_ref[...] * u
    z_ref[...] = jax.nn.gelu(y).reshape(tl, SUBLANE, LANE)


def _s5(u_tm, h0_re, h0_im, a_re, a_im, wb, wc, d_skip):
    l, b, w = u_tm.shape
    nblk = w // LANE
    sw = h0_re.shape[1] // nblk
    tl = _pick(l, (256, 128, 64))
    body = functools.partial(_s5_body, tl=tl)
    return pl.pallas_call(
        body,
        grid=(b // SUBLANE, nblk, l // tl),
        in_specs=[
            pl.BlockSpec((tl, SUBLANE, LANE), lambda g, q, t: (t, g, q)),
            pl.BlockSpec((SUBLANE, sw), lambda g, q, t: (g, q)),
            pl.BlockSpec((SUBLANE, sw), lambda g, q, t: (g, q)),
            pl.BlockSpec((1, S5_PAIRS, LANE), lambda g, q, t: (q, 0, 0)),
            pl.BlockSpec((1, S5_PAIRS, LANE), lambda g, q, t: (q, 0, 0)),
            pl.BlockSpec((1, LANE, 2 * sw), lambda g, q, t: (q, 0, 0)),
            pl.BlockSpec((1, 2 * sw, LANE), lambda g, q, t: (q, 0, 0)),
            pl.BlockSpec((1, LANE), lambda g, q, t: (0, q)),
        ],
        out_specs=[
            pl.BlockSpec((tl, SUBLANE, LANE), lambda g, q, t: (t, g, q)),
            pl.BlockSpec((SUBLANE, sw), lambda g, q, t: (g, q)),
            pl.BlockSpec((SUBLANE, sw), lambda g, q, t: (g, q)),
        ],
        out_shape=[
            jax.ShapeDtypeStruct((l, b, w), F32),
            jax.ShapeDtypeStruct(h0_re.shape, F32),
            jax.ShapeDtypeStruct(h0_im.shape, F32),
        ],
        scratch_shapes=[pltpu.VMEM((tl * SUBLANE, 2 * sw), F32)],
        compiler_params=_params("parallel", "parallel", "arbitrary"),
        name="s5",
    )(u_tm, h0_re, h0_im, a_re, a_im, wb, wc, d_skip)


def _s5_weights(a_re, a_im, log_dt, b_re, b_im, c_re, c_im):
    g, p = a_re.shape
    grp = b_re.shape[2]
    gpb = LANE // grp
    nblk = g // gpb
    a_re, a_im = a_re.astype(F32), a_im.astype(F32)
    dt = jnp.exp(log_dt.astype(F32))[:, None]
    mag = jnp.exp(a_re * dt)
    ab_re, ab_im = mag * jnp.cos(a_im * dt), mag * jnp.sin(a_im * dt)
    den = a_re * a_re + a_im * a_im
    q_re = ((ab_re - 1.0) * a_re + ab_im * a_im) / den
    q_im = (ab_im * a_re - (ab_re - 1.0) * a_im) / den
    b_re, b_im = b_re.astype(F32), b_im.astype(F32)
    bb_re = q_re[..., None] * b_re - q_im[..., None] * b_im
    bb_im = q_re[..., None] * b_im + q_im[..., None] * b_re
    eye = jnp.eye(gpb, dtype=F32)

    def blockdiag_in(m):
        m = m.reshape(nblk, gpb, p, grp)
        return jnp.einsum("qjpc,jk->qjckp", m, eye).reshape(nblk, gpb * grp, gpb * p)

    def blockdiag_out(m):
        m = m.reshape(nblk, gpb, grp, p)
        return jnp.einsum("qjcp,jk->qjpkc", m, eye).reshape(nblk, gpb * p, gpb * grp)

    def interleave(re, im, axis):
        shp = list(re.shape)
        n = shp[axis] // LANE
        shp[axis:axis + 1] = [n, LANE]
        st = jnp.stack([re.reshape(shp), im.reshape(shp)], axis=axis + 1)
        shp[axis:axis + 2] = [2 * n * LANE]
        return st.reshape(shp)

    wb = interleave(blockdiag_in(bb_re), blockdiag_in(bb_im), 2).astype(BF16)
    wc = interleave(blockdiag_out(c_re.astype(F32)), blockdiag_out(-c_im.astype(F32)), 1).astype(BF16)
    pairs = g * p // LANE
    ar = ab_re.reshape(pairs // S5_PAIRS, S5_PAIRS, LANE)
    ai = ab_im.reshape(pairs // S5_PAIRS, S5_PAIRS, LANE)
    return ar, ai, wb, wc


def _pool_body(x_ref, g_ref, hist_ref, w_ref, sc_ref, o_ref, tail_ref, ext_ref, *, tl, nt, pos0, gc):
    t = pl.program_id(1)

    @pl.when(t == 0)
    def _():
        ext_ref[0:POOL_PAD, :] = hist_ref[...]

    @pl.when(t > 0)
    def _():
        ext_ref[0:POOL_PAD, :] = ext_ref[tl:tl + POOL_PAD, :]

    x = x_ref[...]
    ext_ref[POOL_PAD:POOL_PAD + tl, :] = _rms(x, g_ref[...])
    pos = pos0 + t * tl + lax.broadcasted_iota(jnp.int32, (tl, 1), 0)
    for gi, win in enumerate(POOL_WINDOWS):
        cols = slice(gi * gc, (gi + 1) * gc)
        cur = ext_ref[POOL_PAD:POOL_PAD + tl, cols]
        wsum = cur
        for k in range(1, win):
            wsum = wsum + ext_ref[POOL_PAD - k:POOL_PAD - k + tl, cols]
        count = jnp.minimum(pos + 1, win).astype(F32)
        delta = (wsum / count - cur).astype(BF16)
        o_ref[:, cols] = x[:, cols] + _dot(delta, w_ref[gi]) * sc_ref[:, cols]

    @pl.when(t == nt - 1)
    def _():
        tail_ref[...] = ext_ref[tl:tl + POOL_PAD, :]


def _pool(x, g, hist, w, scale, *, row0, batch, seq, pos0):
    t, d = x.shape
    gc = d // len(POOL_WINDOWS)
    tl = _pick(seq, (256, 128, 64))
    nt = seq // tl
    blk0 = row0 // tl
    body = functools.partial(_pool_body, tl=tl, nt=nt, pos0=pos0, gc=gc)
    return pl.pallas_call(
        body,
        grid=(batch, nt),
        in_specs=[
            pl.BlockSpec((tl, d), lambda b, i: (blk0 + b * nt + i, 0)),
            pl.BlockSpec((1, d), lambda b, i: (0, 0)),
            pl.BlockSpec((None, POOL_PAD, d), lambda b, i: (b, 0, 0)),
            pl.BlockSpec(w.shape, lambda b, i: (0, 0, 0)),
            pl.BlockSpec((1, d), lambda b, i: (0, 0)),
        ],
        out_specs=[
            pl.BlockSpec((tl, d), lambda b, i: (blk0 + b * nt + i, 0)),
            pl.BlockSpec((None, POOL_PAD, d), lambda b, i: (b, 0, 0)),
        ],
        out_shape=[
            jax.ShapeDtypeStruct((t, d), F32),
            jax.ShapeDtypeStruct((batch, POOL_PAD, d), F32),
        ],
        scratch_shapes=[pltpu.VMEM((POOL_PAD + tl, d), F32)],
        input_output_aliases={0: 0},
        compiler_params=_params("parallel", "arbitrary"),
        name="pool",
    )(x, g, hist, w, scale)


def _rot_cols(w):
    half = w.shape[-1] // 2
    return jnp.concatenate([-w[..., half:], w[..., :half]], axis=-1)


def _pad_cols(w, n):
    return jnp.pad(w, [(0, 0)] * (w.ndim - 1) + [(0, n - w.shape[-1])])


def kernel(x_prompt, x_sample, cache_mla_ckv, cache_mla_krope, state_s5_re, state_s5_im, cache_pool, ffn1_g, ffn1_w1, ffn1_w3, ffn1_w2, mix_g, ffn2_g, ffn2_w1, ffn2_w3, ffn2_w2, final_g, ab_w_in, mla_g_q, mla_g_kv, mla_w_uq, mla_w_uk, mla_w_uv, s5_a_re, s5_a_im, s5_log_dt, s5_b_re, s5_b_im, s5_c_re, s5_c_im, s5_d, s5_w_glu, ab_w_o, pool_w, pool_scale):
    bp, seq, d = x_prompt.shape
    bs, ls, _ = x_sample.shape
    tp, ts = bp * seq, bs * ls
    tt = tp + ts
    depth = ffn1_g.shape[0]
    past = cache_mla_ckv.shape[2]
    q_lora, heads, qk_head = mla_w_uq.shape[1:]
    kvl, _, nope = mla_w_uk.shape[1:]
    vdim = mla_w_uv.shape[3]
    rope = qk_head - nope
    s5w = s5_d.shape[1]
    groups, nstate = s5_a_re.shape[1:]
    hist_rows = cache_pool.shape[2]
    scale = qk_head ** -0.5
    assert nope == LANE and vdim == LANE and 2 * rope == LANE and kvl + 2 * LANE <= q_lora
    assert bp % SUBLANE == 0 and bs % SUBLANE == 0 and hist_rows == POOL_PAD - 1

    pos = jnp.concatenate([jnp.tile(jnp.arange(seq), bp), past + jnp.tile(jnp.arange(ls), bs)])
    inv = 1.0 / (ROPE_THETA ** (jnp.arange(0, rope, 2, dtype=F32) / rope))
    ang = pos.astype(F32)[:, None] * inv[None, :]
    cos = _pad_cols(jnp.tile(jnp.cos(ang), (1, 2)), LANE)
    sin = _pad_cols(jnp.tile(jnp.sin(ang), (1, 2)), LANE)

    row = lambda v: v.reshape(1, -1).astype(F32)
    fg = row(final_g)

    ffns = []
    for layer in range(depth):
        ffns.append((ffn1_g[layer], (ffn1_w1, ffn1_w3, ffn1_w2), layer))
        ffns.append((ffn2_g[layer], (ffn2_w1, ffn2_w3, ffn2_w2), layer))
    w_next = tuple(_cast(ws, ffns[0][2]) for ws in ffns[0][1])

    def ffn_pair(k, xs):
        nonlocal w_next
        g = row(ffns[k][0])
        w1, w3, w2 = w_next
        ride = (*ffns[k + 1][1], ffns[k + 1][2]) if k + 1 < len(ffns) else None
        (xp, rp), (xs_, rs) = xs
        act_p = _ffn_up(xp, g, w1, w3, row0=rp, nrows=tp)
        act_s = _ffn_up(xs_, g, w1, w3, row0=rs, nrows=ts)
        y, w_next = _ffn_down(act_p, w2, xp, row0=rp, nrows=tp, out_rows=tt, ride=ride)
        y, _ = _ffn_down(act_s, w2, xs_, row0=rs, nrows=ts, out_rows=tt, out_row0=tp, out_alias=y)
        return y

    ckv_out, kr_out, s5re_out, s5im_out, pool_out = [], [], [], [], []
    x = None
    for layer in range(depth):
        src = ((x_prompt.reshape(tp, d), 0), (x_sample.reshape(ts, d), 0)) if x is None else ((x, 0), (x, tp))
        x = ffn_pair(2 * layer, src)
        if layer % 2 == 0:
            i = layer // 2
            w_in = ab_w_in[i]
            w_kr = w_in[:, q_lora + kvl:q_lora + kvl + rope]
            w_mid = jnp.concatenate([w_in[:, q_lora:q_lora + kvl], _pad_cols(w_kr, LANE),
                                     _pad_cols(_rot_cols(w_kr), LANE)], axis=1)
            w_proj = jnp.concatenate([w_in[:, :q_lora], _pad_cols(w_mid, q_lora),
                                      w_in[:, q_lora + kvl + rope:]], axis=1).astype(BF16)
            g_mix = row(mix_g[layer])
            lat_p, u_p = _inproj(x, g_mix, w_proj, n_lat=2 * q_lora, row0=0, nrows=tp)
            lat_s, u_s = _inproj(x, g_mix, w_proj, n_lat=2 * q_lora, row0=tp, nrows=ts)
            g_kv = row(mla_g_kv[i])
            cos_p, sin_p, cos_s, sin_s = cos[:tp], sin[:tp], cos[tp:], sin[tp:]
            ckv_p, kr_p = _kvpost(lat_p, g_kv, cos_p, sin_p, q_lora=q_lora, kv=kvl, rope=rope)
            ckv_s, kr_s = _kvpost(lat_s, g_kv, cos_s, sin_s, q_lora=q_lora, kv=kvl, rope=rope)

            w_uq = mla_w_uq[i]
            wq = jnp.concatenate([w_uq[..., :nope], _pad_cols(w_uq[..., nope:], LANE)], axis=-1)
            wq = wq.reshape(q_lora, -1).astype(BF16)
            wrot = _pad_cols(_rot_cols(w_uq[..., nope:]), LANE).reshape(q_lora, -1).astype(BF16)
            g_q = row(mla_g_q[i])
            q_p = _qproj(lat_p, g_q, wq, wrot, cos_p, sin_p, q_lora=q_lora, heads=heads, nope=nope)
            q_s = _qproj(lat_s, g_q, wq, wrot, cos_s, sin_s, q_lora=q_lora, heads=heads, nope=nope)

            wuk = mla_w_uk[i].reshape(kvl, heads * nope).astype(BF16)
            wuv = mla_w_uv[i].reshape(kvl, heads * vdim).astype(BF16)
            kv_exp = _mm(ckv_p, jnp.concatenate([wuk, wuv], axis=1), out_dtype=BF16, name="kv_expand")
            o_a = _attn_prompt(q_p, kv_exp, kr_p, out_rows=tt, batch=bp, seq=seq, heads=heads, nope=nope,
                               rope=rope, vdim=vdim, scale=scale)
            o_a = _attn_sample(q_s, cache_mla_ckv, cache_mla_krope, ckv_s, kr_s, wuk, wuv, o_a, layer=i,
                               row0=tp, batch=bs, ls=ls, heads=heads, nope=nope, rope=rope, vdim=vdim,
                               scale=scale)

            ar, ai, wb, wc = _s5_weights(s5_a_re[i], s5_a_im[i], s5_log_dt[i], s5_b_re[i], s5_b_im[i],
                                         s5_c_re[i], s5_c_im[i])
            u_p = u_p.reshape(bp, seq, s5w).swapaxes(0, 1)
            u_s = u_s.reshape(bs, ls, s5w).swapaxes(0, 1)
            zeros = jnp.zeros((bp, groups * nstate), F32)
            d_row = row(s5_d[i])
            z_p, pre, pim = _s5(u_p, zeros, zeros, ar, ai, wb, wc, d_row)
            z_s, sre, sim = _s5(u_s, state_s5_re[i].reshape(bs, -1).astype(F32),
                                state_s5_im[i].reshape(bs, -1).astype(F32), ar, ai, wb, wc, d_row)
            wglu = s5_w_glu[i].astype(BF16)
            z_p = z_p.reshape(tp, s5w)
            z_s = z_s.reshape(ts, s5w)
            ob_p = _mm(z_p, wglu, out_dtype=BF16, epilogue="glu", extra=z_p, name="s5_glu")
            ob_s = _mm(z_s, wglu, out_dtype=BF16, epilogue="glu", extra=z_s, name="s5_glu")
            o_b = jnp.concatenate([ob_p.reshape(seq, bp, s5w).swapaxes(0, 1).reshape(tp, s5w),
                                   ob_s.reshape(ls, bs, s5w).swapaxes(0, 1).reshape(ts, s5w)], axis=0)
            x = _outproj(o_a, o_b, ab_w_o[i].astype(BF16), x)

            ckv_out.append((ckv_p.reshape(bp, seq, kvl), ckv_s.reshape(bs, ls, kvl)))
            kr_out.append((kr_p.reshape(bp, seq, rope), kr_s.reshape(bs, ls, rope)))
            s5re_out.append((pre.reshape(bp, groups, nstate), sre.reshape(bs, groups, nstate)))
            s5im_out.append((pim.reshape(bp, groups, nstate), sim.reshape(bs, groups, nstate)))
        else:
            j = layer // 2
            wp = pool_w[j].astype(BF16)
            g_mix, sc = row(mix_g[layer]), row(pool_scale[j])
            hist_p = jnp.zeros((bp, POOL_PAD, d), F32)
            hist_s = jnp.pad(cache_pool[j].astype(F32), ((0, 0), (1, 0), (0, 0)))
            x, tail_p = _pool(x, g_mix, hist_p, wp, sc, row0=0, batch=bp, seq=seq, pos0=0)
            x, tail_s = _pool(x, g_mix, hist_s, wp, sc, row0=tp, batch=bs, seq=ls, pos0=past)
            pool_out.append((tail_p[:, 1:], tail_s[:, 1:]))
        x = ffn_pair(2 * layer + 1, ((x, 0), (x, tp)))

    y_p = _final_norm(x, fg, row0=0, nrows=tp)
    y_s = _final_norm(x, fg, row0=tp, nrows=ts)
    stack = lambda items, k: jnp.stack([it[k] for it in items])
    return (y_p.reshape(bp, seq, d), y_s.reshape(bs, ls, d),
            stack(ckv_out, 0), stack(kr_out, 0), stack(s5re_out, 0), stack(s5im_out, 0), stack(pool_out, 0),
            stack(ckv_out, 1), stack(kr_out, 1), stack(s5re_out, 1), stack(s5im_out, 1), stack(pool_out, 1))
```

```python
import functools
import math

import jax
import jax.numpy as jnp
from jax import lax
from jax.experimental import pallas as pl
from jax.experimental.pallas import tpu as pltpu

F32 = jnp.float32
BF16 = jnp.bfloat16
NORM_EPS = 1e-6
CHUNK = 64
ROPE_THETA = 10000.0
NEG_INF = -1e30
POOL_WINDOWS = (2, 4, 8, 16)
POOL_PAD = 16
LANE = 128
SUBLANE = 8
VMEM_LIMIT = 56 * 1024 * 1024


def _params(*sem):
    return pltpu.CompilerParams(dimension_semantics=sem, vmem_limit_bytes=VMEM_LIMIT)


def _pick(n, candidates):
    for c in candidates:
        if n % c == 0:
            return c
    return n


def _rms(x, g):
    y = x * lax.rsqrt(jnp.mean(x * x, axis=-1, keepdims=True) + NORM_EPS)
    return y * g


def _dot(a, b):
    return jnp.dot(a, b, preferred_element_type=F32)


def _dot_nt(a, b):
    return lax.dot_general(a, b, (((1,), (1,)), ((), ())), preferred_element_type=F32)


def _cast_body(w_ref, o_ref):
    o_ref[...] = w_ref[...].astype(BF16)


def _cast(w_stack, layer):
    _, r, c = w_stack.shape
    br = _pick(r, (256, 128, 64, 16))
    return pl.pallas_call(
        _cast_body,
        grid=(r // br,),
        in_specs=[pl.BlockSpec((None, br, c), lambda i: (layer, i, 0))],
        out_specs=pl.BlockSpec((br, c), lambda i: (i, 0)),
        out_shape=jax.ShapeDtypeStruct((r, c), BF16),
        compiler_params=_params("parallel"),
        name="cast",
    )(w_stack)


FF_TILE = 512


def _ffn_up_body(x_hbm, g_ref, w1_ref, w3_ref, o_ref, h_ref, x_ref, sem, *, halves, last_cols, rb0):
    tm, tn = o_ref.shape
    hr = tm // halves
    i, j = pl.program_id(0), pl.program_id(1)
    ni, nj = pl.num_programs(0), pl.num_programs(1)

    def x_copy(blk):
        start = pl.multiple_of((rb0 + blk) * tm, tm)
        return pltpu.make_async_copy(x_hbm.at[pl.ds(start, tm), :], x_ref, sem)

    @pl.when((i == 0) & (j == 0))
    def _():
        x_copy(0).start()

    @pl.when(j == 0)
    def _():
        x_copy(i).wait()
        for r0 in range(0, tm, hr):
            h_ref[r0:r0 + hr, :] = _rms(x_ref[r0:r0 + hr, :], g_ref[...]).astype(BF16)

    @pl.when((j == nj - 1) & (i + 1 < ni))
    def _():
        x_copy(i + 1).start()

    def block(cols):
        for r0 in range(0, tm, hr):
            h = h_ref[r0:r0 + hr, :]
            a = _dot(h, w1_ref[:, :cols])
            b = _dot(h, w3_ref[:, :cols])
            o_ref[r0:r0 + hr, :cols] = (a * jax.nn.sigmoid(a) * b).astype(BF16)

    if last_cols == tn:
        block(tn)
    else:
        pl.when(j < nj - 1)(lambda: block(tn))
        pl.when(j == nj - 1)(lambda: block(last_cols))


def _ffn_up(x, g, w1, w3, *, row0, nrows):
    d = x.shape[1]
    dff = w1.shape[1]
    tm = _pick(nrows, (1024, 512, 256, 128, 64))
    tn = FF_TILE
    nj = pl.cdiv(dff, tn)
    body = functools.partial(_ffn_up_body, halves=max(1, tm // 256), last_cols=dff - (nj - 1) * tn,
                             rb0=row0 // tm)
    return pl.pallas_call(
        body,
        grid=(nrows // tm, nj),
        in_specs=[
            pl.BlockSpec(memory_space=pl.ANY),
            pl.BlockSpec((1, d), lambda i, j: (0, 0)),
            pl.BlockSpec((d, tn), lambda i, j: (0, j)),
            pl.BlockSpec((d, tn), lambda i, j: (0, j)),
        ],
        out_specs=pl.BlockSpec((tm, tn), lambda i, j: (i, j)),
        out_shape=jax.ShapeDtypeStruct((nrows, dff), BF16),
        scratch_shapes=[pltpu.VMEM((tm, d), BF16), pltpu.VMEM((tm, d), F32), pltpu.SemaphoreType.DMA(())],
        compiler_params=_params("arbitrary", "arbitrary"),
        name="ffn_up",
    )(x, g, w1, w3)


def _ffn_down_body(*refs, ride, aliased):
    refs = list(refs)
    act_ref, w2_ref, x_ref = refs[:3]
    del refs[:3]
    if ride:
        nxt = refs[:3]
        del refs[:3]
    if aliased:
        del refs[:1]
    o_ref = refs.pop(0)
    o_ref[...] = x_ref[...] + 0.5 * _dot(act_ref[...], w2_ref[...])
    if ride:
        for src, dst in zip(nxt, refs):
            dst[...] = src[...].astype(BF16)


def _round_up(n, m):
    return -(-n // m) * m


def _ffn_down(act, w2, x, *, row0, nrows, out_rows=None, out_row0=0, out_alias=None, ride=None):
    dff, d = w2.shape
    tm = _pick(nrows, (512, 256, 128, 64))
    tn = _pick(d, (512, 256, 128))
    ni, nj = nrows // tm, d // tn
    out_rows = nrows if out_rows is None else out_rows
    rb0, ob0 = row0 // tm, out_row0 // tm
    in_specs = [
        pl.BlockSpec((tm, dff), lambda j, i: (i, 0)),
        pl.BlockSpec((dff, tn), lambda j, i: (0, j), pipeline_mode=pl.Buffered(1)),
        pl.BlockSpec((tm, tn), lambda j, i: (rb0 + i, j)),
    ]
    args = [act, w2, x]
    out_specs = [pl.BlockSpec((tm, tn), lambda j, i: (ob0 + i, j))]
    out_shape = [jax.ShapeDtypeStruct((out_rows, d), F32)]
    if ride is not None:
        n1, n3, n2, layer = ride
        assert d % ni == 0 and d % nj == 0
        t13 = (d // ni, _round_up(pl.cdiv(dff, nj), LANE))
        t2 = (_round_up(pl.cdiv(dff, ni), 16), d // nj)
        assert t13[0] % 16 == 0 and t2[1] % LANE == 0
        assert (nj - 1) * t13[1] < dff and (ni - 1) * t2[0] < dff
        in_specs += [
            pl.BlockSpec((None,) + t13, lambda j, i: (layer, i, j)),
            pl.BlockSpec((None,) + t13, lambda j, i: (layer, i, j)),
            pl.BlockSpec((None,) + t2, lambda j, i: (layer, i, j)),
        ]
        args += [n1, n3, n2]
        out_specs += [pl.BlockSpec(t13, lambda j, i: (i, j)), pl.BlockSpec(t13, lambda j, i: (i, j)),
                      pl.BlockSpec(t2, lambda j, i: (i, j))]
        out_shape += [jax.ShapeDtypeStruct((d, dff), BF16), jax.ShapeDtypeStruct((d, dff), BF16),
                      jax.ShapeDtypeStruct((dff, d), BF16)]
    aliases = {}
    if out_alias is not None:
        aliases = {len(args): 0}
        in_specs.append(pl.BlockSpec(memory_space=pl.ANY))
        args.append(out_alias)
    body = functools.partial(_ffn_down_body, ride=ride is not None, aliased=out_alias is not None)
    outs = pl.pallas_call(
        body,
        grid=(nj, ni),
        in_specs=in_specs,
        out_specs=out_specs,
        out_shape=out_shape,
        input_output_aliases=aliases,
        compiler_params=_params("arbitrary", "arbitrary"),
        name="ffn_down",
    )(*args)
    return outs[0], tuple(outs[1:])


def _final_norm_body(x_ref, g_ref, o_ref):
    o_ref[...] = _rms(x_ref[...], g_ref[...])


def _final_norm(x, g, *, row0, nrows):
    d = x.shape[1]
    tm = _pick(nrows, (512, 256, 128, 64))
    rb0 = row0 // tm
    return pl.pallas_call(
        _final_norm_body,
        grid=(nrows // tm,),
        in_specs=[pl.BlockSpec((tm, d), lambda i: (rb0 + i, 0)), pl.BlockSpec((1, d), lambda i: (0, 0))],
        out_specs=pl.BlockSpec((tm, d), lambda i: (i, 0)),
        out_shape=jax.ShapeDtypeStruct((nrows, d), F32),
        compiler_params=_params("parallel"),
        name="final_norm",
    )(x, g)


def _mm_body(*refs, use_scratch, epilogue):
    refs = list(refs)
    lhs_ref = refs.pop(0)
    w_ref = refs.pop(0)
    e_ref = refs.pop(0) if epilogue is not None else None
    o_ref = refs.pop(0)
    if use_scratch:
        s_ref = refs.pop(0)

        @pl.when(pl.program_id(1) == 0)
        def _():
            s_ref[...] = lhs_ref[...].astype(BF16)

        lhs = s_ref[...]
    else:
        lhs = lhs_ref[...]
    acc = _dot(lhs, w_ref[...])
    if epilogue == "glu":
        acc = e_ref[...] * jax.nn.sigmoid(acc)
    o_ref[...] = acc.astype(o_ref.dtype)


def _mm(lhs, w, *, out_dtype, epilogue=None, extra=None, name):
    t = lhs.shape[0]
    k, n = w.shape
    tm = _pick(t, (512, 256, 128, 64))
    tn = _pick(n, (1024, 512, 256, 128))
    use_scratch = lhs.dtype != BF16
    in_specs = [pl.BlockSpec((tm, k), lambda i, j: (i, 0)),
                pl.BlockSpec((k, tn), lambda i, j: (0, j))]
    args = [lhs, w]
    if epilogue is not None:
        in_specs.append(pl.BlockSpec((tm, tn), lambda i, j: (i, j)))
        args.append(extra)
    body = functools.partial(_mm_body, use_scratch=use_scratch, epilogue=epilogue)
    return pl.pallas_call(
        body,
        grid=(t // tm, n // tn),
        in_specs=in_specs,
        out_specs=pl.BlockSpec((tm, tn), lambda i, j: (i, j)),
        out_shape=jax.ShapeDtypeStruct((t, n), out_dtype),
        scratch_shapes=[pltpu.VMEM((tm, k), BF16)] if use_scratch else [],
        compiler_params=_params("parallel", "arbitrary"),
        name=name,
    )(*args)


def _inproj_body(x_ref, g_ref, w_ref, a_ref, u_ref, h_ref, *, na):
    j = pl.program_id(1)

    @pl.when(j == 0)
    def _():
        h_ref[...] = _rms(x_ref[...], g_ref[...]).astype(BF16)

    @pl.when(j < na)
    def _():
        a_ref[...] = _dot(h_ref[...], w_ref[...])

    @pl.when(j >= na)
    def _():
        u_ref[...] = _dot(h_ref[...], w_ref[...])


def _inproj(x, g, w, *, n_lat, row0, nrows):
    d = x.shape[1]
    n = w.shape[1]
    t = nrows
    tm = _pick(t, (512, 256, 128, 64))
    tn = math.gcd(_pick(n_lat, (512, 256, 128)), _pick(n - n_lat, (512, 256, 128)))
    na = n_lat // tn
    rb0 = row0 // tm
    body = functools.partial(_inproj_body, na=na)
    return pl.pallas_call(
        body,
        grid=(t // tm, n // tn),
        in_specs=[
            pl.BlockSpec((tm, d), lambda i, j: (rb0 + i, 0)),
            pl.BlockSpec((1, d), lambda i, j: (0, 0)),
            pl.BlockSpec((d, tn), lambda i, j: (0, j)),
        ],
        out_specs=[
            pl.BlockSpec((tm, tn), lambda i, j: (i, jnp.minimum(j, na - 1))),
            pl.BlockSpec((tm, tn), lambda i, j: (i, jnp.maximum(j - na, 0))),
        ],
        out_shape=[
            jax.ShapeDtypeStruct((t, n_lat), F32),
            jax.ShapeDtypeStruct((t, n - n_lat), F32),
        ],
        scratch_shapes=[pltpu.VMEM((tm, d), BF16)],
        compiler_params=_params("parallel", "arbitrary"),
        name="in_proj",
    )(x, g, w)


def _outproj_body(oa_ref, ob_ref, wa_ref, wb_ref, r_ref, o_ref):
    o_ref[...] = r_ref[...] + _dot(oa_ref[...], wa_ref[...]) + _dot(ob_ref[...], wb_ref[...])


def _outproj(oa, ob, w, res):
    t, ka = oa.shape
    kb = ob.shape[1]
    n = w.shape[1]
    assert ka == kb
    tm = _pick(t, (512, 256, 128, 64))
    tn = _pick(n, (1024, 512, 256, 128))
    return pl.pallas_call(
        _outproj_body,
        grid=(t // tm, n // tn),
        in_specs=[
            pl.BlockSpec((tm, ka), lambda i, j: (i, 0)),
            pl.BlockSpec((tm, kb), lambda i, j: (i, 0)),
            pl.BlockSpec((ka, tn), lambda i, j: (0, j)),
            pl.BlockSpec((kb, tn), lambda i, j: (1, j)),
            pl.BlockSpec((tm, tn), lambda i, j: (i, j)),
        ],
        out_specs=pl.BlockSpec((tm, tn), lambda i, j: (i, j)),
        out_shape=jax.ShapeDtypeStruct((t, n), F32),
        compiler_params=_params("parallel", "arbitrary"),
        name="out_proj",
    )(oa, ob, w, w, res)


def _kvpost_body(p_ref, g_ref, cos_ref, sin_ref, ckv_ref, kr_ref, *, kv, rope):
    ckv_ref[...] = _rms(p_ref[:, :kv], g_ref[...])
    k = p_ref[:, kv:kv + rope]
    k_rot = p_ref[:, kv + LANE:kv + LANE + rope]
    kr_ref[...] = k * cos_ref[:, :rope] + k_rot * sin_ref[:, :rope]


def _kvpost(lat, g_kv, cos, sin, *, q_lora, kv, rope):
    nrows = lat.shape[0]
    tm = _pick(nrows, (512, 256, 128, 64))
    body = functools.partial(_kvpost_body, kv=kv, rope=rope)
    return pl.pallas_call(
        body,
        grid=(nrows // tm,),
        in_specs=[
            pl.BlockSpec((tm, q_lora), lambda i: (i, 1)),
            pl.BlockSpec((1, kv), lambda i: (0, 0)),
            pl.BlockSpec((tm, LANE), lambda i: (i, 0)),
            pl.BlockSpec((tm, LANE), lambda i: (i, 0)),
        ],
        out_specs=[
            pl.BlockSpec((tm, kv), lambda i: (i, 0)),
            pl.BlockSpec((tm, rope), lambda i: (i, 0)),
        ],
        out_shape=[
            jax.ShapeDtypeStruct((nrows, kv), F32),
            jax.ShapeDtypeStruct((nrows, rope), F32),
        ],
        compiler_params=_params("parallel"),
        name="kv_post",
    )(lat, g_kv, cos, sin)


def _qproj_body(c_ref, g_ref, wq_ref, wrot_ref, cos_ref, sin_ref, o_ref, *, heads, nope):
    c = _rms(c_ref[...], g_ref[...]).astype(BF16)
    cos = cos_ref[...]
    sin = sin_ref[...]
    hw = nope + LANE
    for h in range(heads):
        main = _dot(c, wq_ref[:, h * hw:(h + 1) * hw])
        rot = _dot(c, wrot_ref[:, h * LANE:(h + 1) * LANE])
        o_ref[:, h * hw:h * hw + nope] = main[:, :nope].astype(BF16)
        o_ref[:, h * hw + nope:(h + 1) * hw] = (main[:, nope:] * cos + rot * sin).astype(BF16)


def _qproj(lat, g_q, wq, wrot, cos, sin, *, q_lora, heads, nope):
    t = lat.shape[0]
    tm = _pick(t, (512, 256, 128, 64))
    hw = nope + LANE
    body = functools.partial(_qproj_body, heads=heads, nope=nope)
    return pl.pallas_call(
        body,
        grid=(t // tm,),
        in_specs=[
            pl.BlockSpec((tm, q_lora), lambda i: (i, 0)),
            pl.BlockSpec((1, q_lora), lambda i: (0, 0)),
            pl.BlockSpec((q_lora, heads * hw), lambda i: (0, 0)),
            pl.BlockSpec((q_lora, heads * LANE), lambda i: (0, 0)),
            pl.BlockSpec((tm, LANE), lambda i: (i, 0)),
            pl.BlockSpec((tm, LANE), lambda i: (i, 0)),
        ],
        out_specs=pl.BlockSpec((tm, heads * hw), lambda i: (i, 0)),
        out_shape=jax.ShapeDtypeStruct((t, heads * hw), BF16),
        compiler_params=_params("parallel"),
        name="q_proj",
    )(lat, g_q, wq, wrot, cos, sin)


def _attn_prompt_body(q_ref, k_ref, v_ref, kr_ref, o_ref, kc_ref, *, tq, nope, rope, scale):
    seq = k_ref.shape[0]
    kc_ref[:, :nope] = k_ref[...]
    kc_ref[:, nope:nope + rope] = kr_ref[...].astype(BF16)
    kc_ref[:, nope + rope:] = jnp.zeros((seq, kc_ref.shape[1] - nope - rope), BF16)
    q_chunk = lax.broadcasted_iota(jnp.int32, (tq, tq), 0) // CHUNK
    k_chunk = lax.broadcasted_iota(jnp.int32, (tq, tq), 1) // CHUNK
    visible = k_chunk <= q_chunk
    los = [qi * tq for qi in range(seq // tq)]
    s_diag = [jnp.where(visible, _dot_nt(q_ref[lo:lo + tq, :], kc_ref[lo:lo + tq, :]) * scale, NEG_INF)
              for lo in los]
    s_off = [_dot_nt(q_ref[lo:lo + tq, :], kc_ref[0:lo, :]) * scale if lo else None for lo in los]
    probs = []
    for s_d, s_o in zip(s_diag, s_off):
        m = jnp.max(s_d, axis=-1, keepdims=True)
        if s_o is not None:
            m = jnp.maximum(m, jnp.max(s_o, axis=-1, keepdims=True))
        p_d = jnp.exp(s_d - m)
        l = jnp.sum(p_d, axis=-1, keepdims=True)
        p_o = None
        if s_o is not None:
            p_o = jnp.exp(s_o - m)
            l = l + jnp.sum(p_o, axis=-1, keepdims=True)
            p_o = p_o.astype(BF16)
        probs.append((p_d.astype(BF16), p_o, l))
    for lo, (p_d, p_o, l) in zip(los, probs):
        acc = _dot(p_d, v_ref[lo:lo + tq, :])
        if p_o is not None:
            acc = acc + _dot(p_o, v_ref[0:lo, :])
        o_ref[lo:lo + tq, :] = (acc / l).astype(o_ref.dtype)


def _attn_prompt(q_cat, kv_exp, kr, *, out_rows, batch, seq, heads, nope, rope, vdim, scale):
    tq = _pick(seq, (512, 256, 128, 64))
    hw = nope + LANE
    body = functools.partial(_attn_prompt_body, tq=tq, nope=nope, rope=rope, scale=scale)
    return pl.pallas_call(
        body,
        grid=(batch, heads),
        in_specs=[
            pl.BlockSpec((seq, hw), lambda b, h: (b, h)),
            pl.BlockSpec((seq, nope), lambda b, h: (b, h)),
            pl.BlockSpec((seq, vdim), lambda b, h: (b, heads * nope // vdim + h)),
            pl.BlockSpec((seq, rope), lambda b, h: (b, 0)),
        ],
        out_specs=pl.BlockSpec((seq, vdim), lambda b, h: (b, h)),
        out_shape=jax.ShapeDtypeStruct((out_rows, heads * vdim), BF16),
        scratch_shapes=[pltpu.VMEM((seq, hw), BF16)],
        compiler_params=_params("parallel", "arbitrary"),
        name="attn_prompt",
    )(q_cat, kv_exp, kv_exp, kr)


def _attn_sample_body(q_ref, cc_ref, ck_ref, nc_ref, nk_ref, wuk_ref, wuv_ref, prev_ref, o_ref,
                      qlat_ref, qr_ref, m_ref, l_ref, acc_ref, *, heads, nope, rope, vdim, tk, scale):
    del prev_ref
    ls = q_ref.shape[0]
    past = cc_ref.shape[0]
    hw = nope + LANE
    for h in range(heads):
        qh = q_ref[:, h * hw:(h + 1) * hw]
        qlat_ref[h * ls:(h + 1) * ls, :] = _dot_nt(qh[:, :nope], wuk_ref[:, h * nope:(h + 1) * nope]).astype(BF16)
        qr_ref[h * ls:(h + 1) * ls, :] = qh[:, nope:nope + rope]
    m_ref[...] = jnp.full(m_ref.shape, NEG_INF, F32)
    l_ref[...] = jnp.zeros(l_ref.shape, F32)
    acc_ref[...] = jnp.zeros(acc_ref.shape, F32)

    nrow = heads * ls
    hr = _pick(nrow, (256, 128, 64))

    def step(k, kr):
        groups = [slice(r0, r0 + hr) for r0 in range(0, nrow, hr)]
        scores = [(_dot_nt(qlat_ref[rows, :], k) + _dot_nt(qr_ref[rows, :], kr)) * scale for rows in groups]
        probs, alphas = [], []
        for rows, s in zip(groups, scores):
            m = m_ref[rows, :]
            m_new = jnp.maximum(m, jnp.max(s, axis=-1, keepdims=True))
            alpha = jnp.exp(m - m_new)
            p = jnp.exp(s - m_new)
            l_ref[rows, :] = alpha * l_ref[rows, :] + jnp.sum(p, axis=-1, keepdims=True)
            m_ref[rows, :] = m_new
            probs.append(p.astype(BF16))
            alphas.append(alpha)
        for rows, p, alpha in zip(groups, probs, alphas):
            acc_ref[rows, :] = alpha * acc_ref[rows, :] + _dot(p, k)

    def body(kb, carry):
        off = pl.multiple_of(kb * tk, tk)
        step(cc_ref[pl.ds(off, tk), :].astype(BF16), ck_ref[pl.ds(off, tk), :].astype(BF16))
        return carry

    lax.fori_loop(0, past // tk, body, 0)
    step(nc_ref[...].astype(BF16), nk_ref[...].astype(BF16))
    o_lat = (acc_ref[...] / l_ref[...]).astype(BF16)
    for h in range(heads):
        o_ref[:, h * vdim:(h + 1) * vdim] = _dot(
            o_lat[h * ls:(h + 1) * ls, :], wuv_ref[:, h * vdim:(h + 1) * vdim]).astype(o_ref.dtype)


def _attn_sample(q_cat, cache_ckv, cache_kr, ckv, kr, wuk, wuv, o_prev, *, layer, row0, batch, ls, heads,
                 nope, rope, vdim, scale):
    past, kvl = cache_ckv.shape[2], cache_ckv.shape[3]
    hw = nope + LANE
    tk = _pick(past, (512, 256, 128, 64))
    blk0 = row0 // ls
    body = functools.partial(_attn_sample_body, heads=heads, nope=nope, rope=rope, vdim=vdim, tk=tk,
                             scale=scale)
    return pl.pallas_call(
        body,
        grid=(batch,),
        in_specs=[
            pl.BlockSpec((ls, heads * hw), lambda b: (b, 0)),
            pl.BlockSpec((None, None, past, kvl), lambda b: (layer, b, 0, 0)),
            pl.BlockSpec((None, None, past, rope), lambda b: (layer, b, 0, 0)),
            pl.BlockSpec((ls, kvl), lambda b: (b, 0)),
            pl.BlockSpec((ls, rope), lambda b: (b, 0)),
            pl.BlockSpec((kvl, heads * nope), lambda b: (0, 0)),
            pl.BlockSpec((kvl, heads * vdim), lambda b: (0, 0)),
            pl.BlockSpec(memory_space=pl.ANY),
        ],
        out_specs=pl.BlockSpec((ls, heads * vdim), lambda b: (blk0 + b, 0)),
        out_shape=jax.ShapeDtypeStruct(o_prev.shape, BF16),
        scratch_shapes=[
            pltpu.VMEM((heads * ls, kvl), BF16),
            pltpu.VMEM((heads * ls, rope), BF16),
            pltpu.VMEM((heads * ls, 1), F32),
            pltpu.VMEM((heads * ls, 1), F32),
            pltpu.VMEM((heads * ls, kvl), F32),
        ],
        input_output_aliases={7: 0},
        compiler_params=_params("parallel"),
        name="attn_sample",
    )(q_cat, cache_ckv, cache_kr, ckv, kr, wuk, wuv, o_prev)


S5_PAIRS = 4


def _s5_body(u_ref, h0r_ref, h0i_ref, ar_ref, ai_ref, wb_ref, wc_ref, d_ref,
             z_ref, sr_ref, si_ref, bu_ref, *, tl):
    t = pl.program_id(2)

    @pl.when(t == 0)
    def _():
        sr_ref[...] = h0r_ref[...]
        si_ref[...] = h0i_ref[...]

    u = u_ref[...].reshape(tl * SUBLANE, LANE)
    bu_ref[...] = _dot(u.astype(BF16), wb_ref[0])
    a_re = [jnp.broadcast_to(ar_ref[0, p:p + 1, :], (SUBLANE, LANE)) for p in range(S5_PAIRS)]
    a_im = [jnp.broadcast_to(ai_ref[0, p:p + 1, :], (SUBLANE, LANE)) for p in range(S5_PAIRS)]

    def step(l, carry):
        row = pl.multiple_of(l * SUBLANE, SUBLANE)
        out = []
        for p in range(S5_PAIRS):
            s_re, s_im = carry[2 * p], carry[2 * p + 1]
            c0 = 2 * p * LANE
            n_re = a_re[p] * s_re - a_im[p] * s_im + bu_ref[pl.ds(row, SUBLANE), c0:c0 + LANE]
            n_im = a_re[p] * s_im + a_im[p] * s_re + bu_ref[pl.ds(row, SUBLANE), c0 + LANE:c0 + 2 * LANE]
            bu_ref[pl.ds(row, SUBLANE), c0:c0 + LANE] = n_re
            bu_ref[pl.ds(row, SUBLANE), c0 + LANE:c0 + 2 * LANE] = n_im
            out += [n_re, n_im]
        return tuple(out)

    init = []
    for p in range(S5_PAIRS):
        init += [sr_ref[:, p * LANE:(p + 1) * LANE], si_ref[:, p * LANE:(p + 1) * LANE]]
    fin = lax.fori_loop(0, tl, step, tuple(init), unroll=4)
    for p in range(S5_PAIRS):
        sr_ref[:, p * LANE:(p + 1) * LANE] = fin[2 * p]
        si_ref[:, p * LANE:(p + 1) * LANE] = fin[2 * p + 1]
    y = _dot(bu_ref[...].astype(BF16), wc_ref[0]) + d_ref[...] * u
    z_ref[...] = jax.nn.gelu(y).reshape(tl, SUBLANE, LANE)


def _s5(u_tm, h0_re, h0_im, a_re, a_im, wb, wc, d_skip):
    l, b, w = u_tm.shape
    nblk = w // LANE
    sw = h0_re.shape[1] // nblk
    tl = _pick(l, (256, 128, 64))
    body = functools.partial(_s5_body, tl=tl)
    return pl.pallas_call(
        body,
        grid=(b // SUBLANE, nblk, l // tl),
        in_specs=[
            pl.BlockSpec((tl, SUBLANE, LANE), lambda g, q, t: (t, g, q)),
            pl.BlockSpec((SUBLANE, sw), lambda g, q, t: (g, q)),
            pl.BlockSpec((SUBLANE, sw), lambda g, q, t: (g, q)),
            pl.BlockSpec((1, S5_PAIRS, LANE), lambda g, q, t: (q, 0, 0)),
            pl.BlockSpec((1, S5_PAIRS, LANE), lambda g, q, t: (q, 0, 0)),
            pl.BlockSpec((1, LANE, 2 * sw), lambda g, q, t: (q, 0, 0)),
            pl.BlockSpec((1, 2 * sw, LANE), lambda g, q, t: (q, 0, 0)),
            pl.BlockSpec((1, LANE), lambda g, q, t: (0, q)),
        ],
        out_specs=[
            pl.BlockSpec((tl, SUBLANE, LANE), lambda g, q, t: (t, g, q)),
            pl.BlockSpec((SUBLANE, sw), lambda g, q, t: (g, q)),
            pl.BlockSpec((SUBLANE, sw), lambda g, q, t: (g, q)),
        ],
        out_shape=[
            jax.ShapeDtypeStruct((l, b, w), F32),
            jax.ShapeDtypeStruct(h0_re.shape, F32),
            jax.ShapeDtypeStruct(h0_im.shape, F32),
        ],
        scratch_shapes=[pltpu.VMEM((tl * SUBLANE, 2 * sw), F32)],
        compiler_params=_params("parallel", "parallel", "arbitrary"),
        name="s5",
    )(u_tm, h0_re, h0_im, a_re, a_im, wb, wc, d_skip)


def _s5_weights(a_re, a_im, log_dt, b_re, b_im, c_re, c_im):
    g, p = a_re.shape
    grp = b_re.shape[2]
    gpb = LANE // grp
    nblk = g // gpb
    a_re, a_im = a_re.astype(F32), a_im.astype(F32)
    dt = jnp.exp(log_dt.astype(F32))[:, None]
    mag = jnp.exp(a_re * dt)
    ab_re, ab_im = mag * jnp.cos(a_im * dt), mag * jnp.sin(a_im * dt)
    den = a_re * a_re + a_im * a_im
    q_re = ((ab_re - 1.0) * a_re + ab_im * a_im) / den
    q_im = (ab_im * a_re - (ab_re - 1.0) * a_im) / den
    b_re, b_im = b_re.astype(F32), b_im.astype(F32)
    bb_re = q_re[..., None] * b_re - q_im[..., None] * b_im
    bb_im = q_re[..., None] * b_im + q_im[..., None] * b_re
    eye = jnp.eye(gpb, dtype=F32)

    def blockdiag_in(m):
        m = m.reshape(nblk, gpb, p, grp)
        return jnp.einsum("qjpc,jk->qjckp", m, eye).reshape(nblk, gpb * grp, gpb * p)

    def blockdiag_out(m):
        m = m.reshape(nblk, gpb, grp, p)
        return jnp.einsum("qjcp,jk->qjpkc", m, eye).reshape(nblk, gpb * p, gpb * grp)

    def interleave(re, im, axis):
        shp = list(re.shape)
        n = shp[axis] // LANE
        shp[axis:axis + 1] = [n, LANE]
        st = jnp.stack([re.reshape(shp), im.reshape(shp)], axis=axis + 1)
        shp[axis:axis + 2] = [2 * n * LANE]
        return st.reshape(shp)

    wb = interleave(blockdiag_in(bb_re), blockdiag_in(bb_im), 2).astype(BF16)
    wc = interleave(blockdiag_out(c_re.astype(F32)), blockdiag_out(-c_im.astype(F32)), 1).astype(BF16)
    pairs = g * p // LANE
    ar = ab_re.reshape(pairs // S5_PAIRS, S5_PAIRS, LANE)
    ai = ab_im.reshape(pairs // S5_PAIRS, S5_PAIRS, LANE)
    return ar, ai, wb, wc


def _pool_body(x_ref, g_ref, hist_ref, w_ref, sc_ref, o_ref, tail_ref, ext_ref, *, tl, nt, pos0, gc):
    t = pl.program_id(1)

    @pl.when(t == 0)
    def _():
        ext_ref[0:POOL_PAD, :] = hist_ref[...]

    @pl.when(t > 0)
    def _():
        ext_ref[0:POOL_PAD, :] = ext_ref[tl:tl + POOL_PAD, :]

    x = x_ref[...]
    ext_ref[POOL_PAD:POOL_PAD + tl, :] = _rms(x, g_ref[...])
    pos = pos0 + t * tl + lax.broadcasted_iota(jnp.int32, (tl, 1), 0)
    for gi, win in enumerate(POOL_WINDOWS):
        cols = slice(gi * gc, (gi + 1) * gc)
        cur = ext_ref[POOL_PAD:POOL_PAD + tl, cols]
        wsum = cur
        for k in range(1, win):
            wsum = wsum + ext_ref[POOL_PAD - k:POOL_PAD - k + tl, cols]
        count = jnp.minimum(pos + 1, win).astype(F32)
        delta = (wsum / count - cur).astype(BF16)
        o_ref[:, cols] = x[:, cols] + _dot(delta, w_ref[gi]) * sc_ref[:, cols]

    @pl.when(t == nt - 1)
    def _():
        tail_ref[...] = ext_ref[tl:tl + POOL_PAD, :]


def _pool(x, g, hist, w, scale, *, row0, batch, seq, pos0):
    t, d = x.shape
    gc = d // len(POOL_WINDOWS)
    tl = _pick(seq, (256, 128, 64))
    nt = seq // tl
    blk0 = row0 // tl
    body = functools.partial(_pool_body, tl=tl, nt=nt, pos0=pos0, gc=gc)
    return pl.pallas_call(
        body,
        grid=(batch, nt),
        in_specs=[
            pl.BlockSpec((tl, d), lambda b, i: (blk0 + b * nt + i, 0)),
            pl.BlockSpec((1, d), lambda b, i: (0, 0)),
            pl.BlockSpec((None, POOL_PAD, d), lambda b, i: (b, 0, 0)),
            pl.BlockSpec(w.shape, lambda b, i: (0, 0, 0)),
            pl.BlockSpec((1, d), lambda b, i: (0, 0)),
        ],
        out_specs=[
            pl.BlockSpec((tl, d), lambda b, i: (blk0 + b * nt + i, 0)),
            pl.BlockSpec((None, POOL_PAD, d), lambda b, i: (b, 0, 0)),
        ],
        out_shape=[
            jax.ShapeDtypeStruct((t, d), F32),
            jax.ShapeDtypeStruct((batch, POOL_PAD, d), F32),
        ],
        scratch_shapes=[pltpu.VMEM((POOL_PAD + tl, d), F32)],
        input_output_aliases={0: 0},
        compiler_params=_params("parallel", "arbitrary"),
        name="pool",
    )(x, g, hist, w, scale)


def _rot_cols(w):
    half = w.shape[-1] // 2
    return jnp.concatenate([-w[..., half:], w[..., :half]], axis=-1)


def _pad_cols(w, n):
    return jnp.pad(w, [(0, 0)] * (w.ndim - 1) + [(0, n - w.shape[-1])])


def kernel(x_prompt, x_sample, cache_mla_ckv, cache_mla_krope, state_s5_re, state_s5_im, cache_pool, ffn1_g, ffn1_w1, ffn1_w3, ffn1_w2, mix_g, ffn2_g, ffn2_w1, ffn2_w3, ffn2_w2, final_g, ab_w_in, mla_g_q, mla_g_kv, mla_w_uq, mla_w_uk, mla_w_uv, s5_a_re, s5_a_im, s5_log_dt, s5_b_re, s5_b_im, s5_c_re, s5_c_im, s5_d, s5_w_glu, ab_w_o, pool_w, pool_scale):
    bp, seq, d = x_prompt.shape
    bs, ls, _ = x_sample.shape
    tp, ts = bp * seq, bs * ls
    tt = tp + ts
    depth = ffn1_g.shape[0]
    past = cache_mla_ckv.shape[2]
    q_lora, heads, qk_head = mla_w_uq.shape[1:]
    kvl, _, nope = mla_w_uk.shape[1:]
    vdim = mla_w_uv.shape[3]
    rope = qk_head - nope
    s5w = s5_d.shape[1]
    groups, nstate = s5_a_re.shape[1:]
    hist_rows = cache_pool.shape[2]
    scale = qk_head ** -0.5
    assert nope == LANE and vdim == LANE and 2 * rope == LANE and kvl + 2 * LANE <= q_lora
    assert bp % SUBLANE == 0 and bs % SUBLANE == 0 and hist_rows == POOL_PAD - 1

    pos = jnp.concatenate([jnp.tile(jnp.arange(seq), bp), past + jnp.tile(jnp.arange(ls), bs)])
    inv = 1.0 / (ROPE_THETA ** (jnp.arange(0, rope, 2, dtype=F32) / rope))
    ang = pos.astype(F32)[:, None] * inv[None, :]
    cos = _pad_cols(jnp.tile(jnp.cos(ang), (1, 2)), LANE)
    sin = _pad_cols(jnp.tile(jnp.sin(ang), (1, 2)), LANE)

    row = lambda v: v.reshape(1, -1).astype(F32)
    fg = row(final_g)

    ffns = []
    for layer in range(depth):
        ffns.append((ffn1_g[layer], (ffn1_w1, ffn1_w3, ffn1_w2), layer))
        ffns.append((ffn2_g[layer], (ffn2_w1, ffn2_w3, ffn2_w2), layer))
    w_next = tuple(_cast(ws, ffns[0][2]) for ws in ffns[0][1])

    def ffn_pair(k, xs):
        nonlocal w_next
        g = row(ffns[k][0])
        w1, w3, w2 = w_next
        ride = (*ffns[k + 1][1], ffns[k + 1][2]) if k + 1 < len(ffns) else None
        (xp, rp), (xs_, rs) = xs
        act_p = _ffn_up(xp, g, w1, w3, row0=rp, nrows=tp)
        act_s = _ffn_up(xs_, g, w1, w3, row0=rs, nrows=ts)
        y, w_next = _ffn_down(act_p, w2, xp, row0=rp, nrows=tp, out_rows=tt, ride=ride)
        y, _ = _ffn_down(act_s, w2, xs_, row0=rs, nrows=ts, out_rows=tt, out_row0=tp, out_alias=y)
        return y

    ckv_out, kr_out, s5re_out, s5im_out, pool_out = [], [], [], [], []
    x = None
    for layer in range(depth):
        src = ((x_prompt.reshape(tp, d), 0), (x_sample.reshape(ts, d), 0)) if x is None else ((x, 0), (x, tp))
        x = ffn_pair(2 * layer, src)
        if layer % 2 == 0:
            i = layer // 2
            w_in = ab_w_in[i]
            w_kr = w_in[:, q_lora + kvl:q_lora + kvl + rope]
            w_mid = jnp.concatenate([w_in[:, q_lora:q_lora + kvl], _pad_cols(w_kr, LANE),
                                     _pad_cols(_rot_cols(w_kr), LANE)], axis=1)
            w_proj = jnp.concatenate([w_in[:, :q_lora], _pad_cols(w_mid, q_lora),
                                      w_in[:, q_lora + kvl + rope:]], axis=1).astype(BF16)
            g_mix = row(mix_g[layer])
            lat_p, u_p = _inproj(x, g_mix, w_proj, n_lat=2 * q_lora, row0=0, nrows=tp)
            lat_s, u_s = _inproj(x, g_mix, w_proj, n_lat=2 * q_lora, row0=tp, nrows=ts)
            g_kv = row(mla_g_kv[i])
            cos_p, sin_p, cos_s, sin_s = cos[:tp], sin[:tp], cos[tp:], sin[tp:]
            ckv_p, kr_p = _kvpost(lat_p, g_kv, cos_p, sin_p, q_lora=q_lora, kv=kvl, rope=rope)
            ckv_s, kr_s = _kvpost(lat_s, g_kv, cos_s, sin_s, q_lora=q_lora, kv=kvl, rope=rope)

            w_uq = mla_w_uq[i]
            wq = jnp.concatenate([w_uq[..., :nope], _pad_cols(w_uq[..., nope:], LANE)], axis=-1)
            wq = wq.reshape(q_lora, -1).astype(BF16)
            wrot = _pad_cols(_rot_cols(w_uq[..., nope:]), LANE).reshape(q_lora, -1).astype(BF16)
            g_q = row(mla_g_q[i])
            q_p = _qproj(lat_p, g_q, wq, wrot, cos_p, sin_p, q_lora=q_lora, heads=heads, nope=nope)
            q_s = _qproj(lat_s, g_q, wq, wrot, cos_s, sin_s, q_lora=q_lora, heads=heads, nope=nope)

            wuk = mla_w_uk[i].reshape(kvl, heads * nope).astype(BF16)
            wuv = mla_w_uv[i].reshape(kvl, heads * vdim).astype(BF16)
            kv_exp = _mm(ckv_p, jnp.concatenate([wuk, wuv], axis=1), out_dtype=BF16, name="kv_expand")
            o_a = _attn_prompt(q_p, kv_exp, kr_p, out_rows=tt, batch=bp, seq=seq, heads=heads, nope=nope,
                               rope=rope, vdim=vdim, scale=scale)
            o_a = _attn_sample(q_s, cache_mla_ckv, cache_mla_krope, ckv_s, kr_s, wuk, wuv, o_a, layer=i,
                               row0=tp, batch=bs, ls=ls, heads=heads, nope=nope, rope=rope, vdim=vdim,
                               scale=scale)

            ar, ai, wb, wc = _s5_weights(s5_a_re[i], s5_a_im[i], s5_log_dt[i], s5_b_re[i], s5_b_im[i],
                                         s5_c_re[i], s5_c_im[i])
            u_p = u_p.reshape(bp, seq, s5w).swapaxes(0, 1)
            u_s = u_s.reshape(bs, ls, s5w).swapaxes(0, 1)
            zeros = jnp.zeros((bp, groups * nstate), F32)
            d_row = row(s5_d[i])
            z_p, pre, pim = _s5(u_p, zeros, zeros, ar, ai, wb, wc, d_row)
            z_s, sre, sim = _s5(u_s, state_s5_re[i].reshape(bs, -1).astype(F32),
                                state_s5_im[i].reshape(bs, -1).astype(F32), ar, ai, wb, wc, d_row)
            wglu = s5_w_glu[i].astype(BF16)
            z_p = z_p.reshape(tp, s5w)
            z_s = z_s.reshape(ts, s5w)
            ob_p = _mm(z_p, wglu, out_dtype=BF16, epilogue="glu", extra=z_p, name="s5_glu")
            ob_s = _mm(z_s, wglu, out_dtype=BF16, epilogue="glu", extra=z_s, name="s5_glu")
            o_b = jnp.concatenate([ob_p.reshape(seq, bp, s5w).swapaxes(0, 1).reshape(tp, s5w),
                                   ob_s.reshape(ls, bs, s5w).swapaxes(0, 1).reshape(ts, s5w)], axis=0)
            x = _outproj(o_a, o_b, ab_w_o[i].astype(BF16), x)

            ckv_out.append((ckv_p.reshape(bp, seq, kvl), ckv_s.reshape(bs, ls, kvl)))
            kr_out.append((kr_p.reshape(bp, seq, rope), kr_s.reshape(bs, ls, rope)))
            s5re_out.append((pre.reshape(bp, groups, nstate), sre.reshape(bs, groups, nstate)))
            s5im_out.append((pim.reshape(bp, groups, nstate), sim.reshape(bs, groups, nstate)))
        else:
            j = layer // 2
            wp = pool_w[j].astype(BF16)
            g_mix, sc = row(mix_g[layer]), row(pool_scale[j])
            hist_p = jnp.zeros((bp, POOL_PAD, d), F32)
            hist_s = jnp.pad(cache_pool[j].astype(F32), ((0, 0), (1, 0), (0, 0)))
            x, tail_p = _pool(x, g_mix, hist_p, wp, sc, row0=0, batch=bp, seq=seq, pos0=0)
            x, tail_s = _pool(x, g_mix, hist_s, wp, sc, row0=tp, batch=bs, seq=ls, pos0=past)
            pool_out.append((tail_p[:, 1:], tail_s[:, 1:]))
        x = ffn_pair(2 * layer + 1, ((x, 0), (x, tp)))

    y_p = _final_norm(x, fg, row0=0, nrows=tp)
    y_s = _final_norm(x, fg, row0=tp, nrows=ts)
    stack = lambda items, k: jnp.stack([it[k] for it in items])
    return (y_p.reshape(bp, seq, d), y_s.reshape(bs, ls, d),
            stack(ckv_out, 0), stack(kr_out, 0), stack(s5re_out, 0), stack(s5im_out, 0), stack(pool_out, 0),
            stack(ckv_out, 1), stack(kr_out, 1), stack(s5re_out, 1), stack(s5im_out, 1), stack(pool_out, 1))
```

```python
import functools
import math

import jax
import jax.numpy as jnp
from jax import lax
from jax.experimental import pallas as pl
from jax.experimental.pallas import tpu as pltpu

F32 = jnp.float32
BF16 = jnp.bfloat16
NORM_EPS = 1e-6
CHUNK = 64
ROPE_THETA = 10000.0
NEG_INF = -1e30
POOL_WINDOWS = (2, 4, 8, 16)
POOL_PAD = 16
LANE = 128
SUBLANE = 8
VMEM_LIMIT = 56 * 1024 * 1024


def _params(*sem):
    return pltpu.CompilerParams(dimension_semantics=sem, vmem_limit_bytes=VMEM_LIMIT)


def _pick(n, candidates):
    for c in candidates:
        if n % c == 0:
            return c
    return n


def _rms(x, g):
    y = x * lax.rsqrt(jnp.mean(x * x, axis=-1, keepdims=True) + NORM_EPS)
    return y * g


def _dot(a, b):
    return jnp.dot(a, b, preferred_element_type=F32)


def _dot_nt(a, b):
    return lax.dot_general(a, b, (((1,), (1,)), ((), ())), preferred_element_type=F32)


def _cast_body(w_ref, o_ref):
    o_ref[...] = w_ref[...].astype(BF16)


def _cast(w_stack, layer):
    _, r, c = w_stack.shape
    br = _pick(r, (256, 128, 64, 16))
    return pl.pallas_call(
        _cast_body,
        grid=(r // br,),
        in_specs=[pl.BlockSpec((None, br, c), lambda i: (layer, i, 0))],
        out_specs=pl.BlockSpec((br, c), lambda i: (i, 0)),
        out_shape=jax.ShapeDtypeStruct((r, c), BF16),
        compiler_params=_params("parallel"),
        name="cast",
    )(w_stack)


FF_TILE = 512


def _ffn_up_body(x_hbm, g_ref, w1_ref, w3_ref, o_ref, h_ref, x_ref, sem, *, halves, last_cols, rb0):
    tm, tn = o_ref.shape
    hr = tm // halves
    i, j = pl.program_id(0), pl.program_id(1)
    ni, nj = pl.num_programs(0), pl.num_programs(1)

    def x_copy(blk):
        start = pl.multiple_of((rb0 + blk) * tm, tm)
        return pltpu.make_async_copy(x_hbm.at[pl.ds(start, tm), :], x_ref, sem)

    @pl.when((i == 0) & (j == 0))
    def _():
        x_copy(0).start()

    @pl.when(j == 0)
    def _():
        x_copy(i).wait()
        for r0 in range(0, tm, hr):
            h_ref[r0:r0 + hr, :] = _rms(x_ref[r0:r0 + hr, :], g_ref[...]).astype(BF16)

    @pl.when((j == nj - 1) & (i + 1 < ni))
    def _():
        x_copy(i + 1).start()

    def block(cols):
        for r0 in range(0, tm, hr):
            h = h_ref[r0:r0 + hr, :]
            a = _dot(h, w1_ref[:, :cols])
            b = _dot(h, w3_ref[:, :cols])
            o_ref[r0:r0 + hr, :cols] = (a * jax.nn.sigmoid(a) * b).astype(BF16)

    if last_cols == tn:
        block(tn)
    else:
        pl.when(j < nj - 1)(lambda: block(tn))
        pl.when(j == nj - 1)(lambda: block(last_cols))


def _ffn_up(x, g, w1, w3, *, row0, nrows):
    d = x.shape[1]
    dff = w1.shape[1]
    tm = _pick(nrows, (1024, 512, 256, 128, 64))
    tn = FF_TILE
    nj = pl.cdiv(dff, tn)
    body = functools.partial(_ffn_up_body, halves=max(1, tm // 256), last_cols=dff - (nj - 1) * tn,
                             rb0=row0 // tm)
    return pl.pallas_call(
        body,
        grid=(nrows // tm, nj),
        in_specs=[
            pl.BlockSpec(memory_space=pl.ANY),
            pl.BlockSpec((1, d), lambda i, j: (0, 0)),
            pl.BlockSpec((d, tn), lambda i, j: (0, j)),
            pl.BlockSpec((d, tn), lambda i, j: (0, j)),
        ],
        out_specs=pl.BlockSpec((tm, tn), lambda i, j: (i, j)),
        out_shape=jax.ShapeDtypeStruct((nrows, dff), BF16),
        scratch_shapes=[pltpu.VMEM((tm, d), BF16), pltpu.VMEM((tm, d), F32), pltpu.SemaphoreType.DMA(())],
        compiler_params=_params("arbitrary", "arbitrary"),
        name="ffn_up",
    )(x, g, w1, w3)


def _ffn_down_body(*refs, ride, aliased):
    refs = list(refs)
    act_ref, w2_ref, x_ref = refs[:3]
    del refs[:3]
    if ride:
        nxt = refs[:3]
        del refs[:3]
    if aliased:
        del refs[:1]
    o_ref = refs.pop(0)
    o_ref[...] = x_ref[...] + 0.5 * _dot(act_ref[...], w2_ref[...])
    if ride:
        for src, dst in zip(nxt, refs):
            dst[...] = src[...].astype(BF16)


def _round_up(n, m):
    return -(-n // m) * m


def _ffn_down(act, w2, x, *, row0, nrows, out_rows=None, out_row0=0, out_alias=None, ride=None):
    dff, d = w2.shape
    tm = _pick(nrows, (512, 256, 128, 64))
    tn = _pick(d, (512, 256, 128))
    ni, nj = nrows // tm, d // tn
    out_rows = nrows if out_rows is None else out_rows
    rb0, ob0 = row0 // tm, out_row0 // tm
    in_specs = [
        pl.BlockSpec((tm, dff), lambda j, i: (i, 0)),
        pl.BlockSpec((dff, tn), lambda j, i: (0, j), pipeline_mode=pl.Buffered(1)),
        pl.BlockSpec((tm, tn), lambda j, i: (rb0 + i, j)),
    ]
    args = [act, w2, x]
    out_specs = [pl.BlockSpec((tm, tn), lambda j, i: (ob0 + i, j))]
    out_shape = [jax.ShapeDtypeStruct((out_rows, d), F32)]
    if ride is not None:
        n1, n3, n2, layer = ride
        assert d % ni == 0 and d % nj == 0
        t13 = (d // ni, _round_up(pl.cdiv(dff, nj), LANE))
        t2 = (_round_up(pl.cdiv(dff, ni), 16), d // nj)
        assert t13[0] % 16 == 0 and t2[1] % LANE == 0
        assert (nj - 1) * t13[1] < dff and (ni - 1) * t2[0] < dff
        in_specs += [
            pl.BlockSpec((None,) + t13, lambda j, i: (layer, i, j)),
            pl.BlockSpec((None,) + t13, lambda j, i: (layer, i, j)),
            pl.BlockSpec((None,) + t2, lambda j, i: (layer, i, j)),
        ]
        args += [n1, n3, n2]
        out_specs += [pl.BlockSpec(t13, lambda j, i: (i, j)), pl.BlockSpec(t13, lambda j, i: (i, j)),
                      pl.BlockSpec(t2, lambda j, i: (i, j))]
        out_shape += [jax.ShapeDtypeStruct((d, dff), BF16), jax.ShapeDtypeStruct((d, dff), BF16),
                      jax.ShapeDtypeStruct((dff, d), BF16)]
    aliases = {}
    if out_alias is not None:
        aliases = {len(args): 0}
        in_specs.append(pl.BlockSpec(memory_space=pl.ANY))
        args.append(out_alias)
    body = functools.partial(_ffn_down_body, ride=ride is not None, aliased=out_alias is not None)
    outs = pl.pallas_call(
        body,
        grid=(nj, ni),
        in_specs=in_specs,
        out_specs=out_specs,
        out_shape=out_shape,
        input_output_aliases=aliases,
        compiler_params=_params("arbitrary", "arbitrary"),
        name="ffn_down",
    )(*args)
    return outs[0], tuple(outs[1:])


def _final_norm_body(x_ref, g_ref, o_ref):
    o_ref[...] = _rms(x_ref[...], g_ref[...])


def _final_norm(x, g, *, row0, nrows):
    d = x.shape[1]
    tm = _pick(nrows, (512, 256, 128, 64))
    rb0 = row0 // tm
    return pl.pallas_call(
        _final_norm_body,
        grid=(nrows // tm,),
        in_specs=[pl.BlockSpec((tm, d), lambda i: (rb0 + i, 0)), pl.BlockSpec((1, d), lambda i: (0, 0))],
        out_specs=pl.BlockSpec((tm, d), lambda i: (i, 0)),
        out_shape=jax.ShapeDtypeStruct((nrows, d), F32),
        compiler_params=_params("parallel"),
        name="final_norm",
    )(x, g)


def _mm_body(*refs, use_scratch, epilogue):
    refs = list(refs)
    lhs_ref = refs.pop(0)
    w_ref = refs.pop(0)
    e_ref = refs.pop(0) if epilogue is not None else None
    o_ref = refs.pop(0)
    if use_scratch:
        s_ref = refs.pop(0)

        @pl.when(pl.program_id(1) == 0)
        def _():
            s_ref[...] = lhs_ref[...].astype(BF16)

        lhs = s_ref[...]
    else:
        lhs = lhs_ref[...]
    acc = _dot(lhs, w_ref[...])
    if epilogue == "glu":
        acc = e_ref[...] * jax.nn.sigmoid(acc)
    o_ref[...] = acc.astype(o_ref.dtype)


def _mm(lhs, w, *, out_dtype, epilogue=None, extra=None, name):
    t = lhs.shape[0]
    k, n = w.shape
    tm = _pick(t, (512, 256, 128, 64))
    tn = _pick(n, (1024, 512, 256, 128))
    use_scratch = lhs.dtype != BF16
    in_specs = [pl.BlockSpec((tm, k), lambda i, j: (i, 0)),
                pl.BlockSpec((k, tn), lambda i, j: (0, j))]
    args = [lhs, w]
    if epilogue is not None:
        in_specs.append(pl.BlockSpec((tm, tn), lambda i, j: (i, j)))
        args.append(extra)
    body = functools.partial(_mm_body, use_scratch=use_scratch, epilogue=epilogue)
    return pl.pallas_call(
        body,
        grid=(t // tm, n // tn),
        in_specs=in_specs,
        out_specs=pl.BlockSpec((tm, tn), lambda i, j: (i, j)),
        out_shape=jax.ShapeDtypeStruct((t, n), out_dtype),
        scratch_shapes=[pltpu.VMEM((tm, k), BF16)] if use_scratch else [],
        compiler_params=_params("parallel", "arbitrary"),
        name=name,
    )(*args)


def _inproj_body(x_ref, g_ref, w_ref, a_ref, u_ref, h_ref, *, na):
    j = pl.program_id(1)

    @pl.when(j == 0)
    def _():
        h_ref[...] = _rms(x_ref[...], g_ref[...]).astype(BF16)

    @pl.when(j < na)
    def _():
        a_ref[...] = _dot(h_ref[...], w_ref[...])

    @pl.when(j >= na)
    def _():
        u_ref[...] = _dot(h_ref[...], w_ref[...])


def _inproj(x, g, w, *, n_lat, row0, nrows):
    d = x.shape[1]
    n = w.shape[1]
    t = nrows
    tm = _pick(t, (512, 256, 128, 64))
    tn = math.gcd(_pick(n_lat, (512, 256, 128)), _pick(n - n_lat, (512, 256, 128)))
    na = n_lat // tn
    rb0 = row0 // tm
    body = functools.partial(_inproj_body, na=na)
    return pl.pallas_call(
        body,
        grid=(t // tm, n // tn),
        in_specs=[
            pl.BlockSpec((tm, d), lambda i, j: (rb0 + i, 0)),
            pl.BlockSpec((1, d), lambda i, j: (0, 0)),
            pl.BlockSpec((d, tn), lambda i, j: (0, j)),
        ],
        out_specs=[
            pl.BlockSpec((tm, tn), lambda i, j: (i, jnp.minimum(j, na - 1))),
            pl.BlockSpec((tm, tn), lambda i, j: (i, jnp.maximum(j - na, 0))),
        ],
        out_shape=[
            jax.ShapeDtypeStruct((t, n_lat), F32),
            jax.ShapeDtypeStruct((t, n - n_lat), F32),
        ],
        scratch_shapes=[pltpu.VMEM((tm, d), BF16)],
        compiler_params=_params("parallel", "arbitrary"),
        name="in_proj",
    )(x, g, w)


def _outproj_body(oa_ref, ob_ref, wa_ref, wb_ref, r_ref, o_ref):
    o_ref[...] = r_ref[...] + _dot(oa_ref[...], wa_ref[...]) + _dot(ob_ref[...], wb_ref[...])


def _outproj(oa, ob, w, res):
    t, ka = oa.shape
    kb = ob.shape[1]
    n = w.shape[1]
    assert ka == kb
    tm = _pick(t, (512, 256, 128, 64))
    tn = _pick(n, (1024, 512, 256, 128))
    return pl.pallas_call(
        _outproj_body,
        grid=(t // tm, n // tn),
        in_specs=[
            pl.BlockSpec((tm, ka), lambda i, j: (i, 0)),
            pl.BlockSpec((tm, kb), lambda i, j: (i, 0)),
            pl.BlockSpec((ka, tn), lambda i, j: (0, j)),
            pl.BlockSpec((kb, tn), lambda i, j: (1, j)),
            pl.BlockSpec((tm, tn), lambda i, j: (i, j)),
        ],
        out_specs=pl.BlockSpec((tm, tn), lambda i, j: (i, j)),
        out_shape=jax.ShapeDtypeStruct((t, n), F32),
        compiler_params=_params("parallel", "arbitrary"),
        name="out_proj",
    )(oa, ob, w, w, res)


def _kvpost_body(p_ref, g_ref, cos_ref, sin_ref, ckv_ref, kr_ref, *, kv, rope):
    ckv_ref[...] = _rms(p_ref[:, :kv], g_ref[...])
    k = p_ref[:, kv:kv + rope]
    k_rot = p_ref[:, kv + LANE:kv + LANE + rope]
    kr_ref[...] = k * cos_ref[:, :rope] + k_rot * sin_ref[:, :rope]


def _kvpost(lat, g_kv, cos, sin, *, q_lora, kv, rope):
    nrows = lat.shape[0]
    tm = _pick(nrows, (512, 256, 128, 64))
    body = functools.partial(_kvpost_body, kv=kv, rope=rope)
    return pl.pallas_call(
        body,
        grid=(nrows // tm,),
        in_specs=[
            pl.BlockSpec((tm, q_lora), lambda i: (i, 1)),
            pl.BlockSpec((1, kv), lambda i: (0, 0)),
            pl.BlockSpec((tm, LANE), lambda i: (i, 0)),
            pl.BlockSpec((tm, LANE), lambda i: (i, 0)),
        ],
        out_specs=[
            pl.BlockSpec((tm, kv), lambda i: (i, 0)),
            pl.BlockSpec((tm, rope), lambda i: (i, 0)),
        ],
        out_shape=[
            jax.ShapeDtypeStruct((nrows, kv), F32),
            jax.ShapeDtypeStruct((nrows, rope), F32),
        ],
        compiler_params=_params("parallel"),
        name="kv_post",
    )(lat, g_kv, cos, sin)


def _qproj_body(c_ref, g_ref, wq_ref, wrot_ref, cos_ref, sin_ref, o_ref, *, heads, nope):
    c = _rms(c_ref[...], g_ref[...]).astype(BF16)
    cos = cos_ref[...]
    sin = sin_ref[...]
    hw = nope + LANE
    for h in range(heads):
        main = _dot(c, wq_ref[:, h * hw:(h + 1) * hw])
        rot = _dot(c, wrot_ref[:, h * LANE:(h + 1) * LANE])
        o_ref[:, h * hw:h * hw + nope] = main[:, :nope].astype(BF16)
        o_ref[:, h * hw + nope:(h + 1) * hw] = (main[:, nope:] * cos + rot * sin).astype(BF16)


def _qproj(lat, g_q, wq, wrot, cos, sin, *, q_lora, heads, nope):
    t = lat.shape[0]
    tm = _pick(t, (512, 256, 128, 64))
    hw = nope + LANE
    body = functools.partial(_qproj_body, heads=heads, nope=nope)
    return pl.pallas_call(
        body,
        grid=(t // tm,),
        in_specs=[
            pl.BlockSpec((tm, q_lora), lambda i: (i, 0)),
            pl.BlockSpec((1, q_lora), lambda i: (0, 0)),
            pl.BlockSpec((q_lora, heads * hw), lambda i: (0, 0)),
            pl.BlockSpec((q_lora, heads * LANE), lambda i: (0, 0)),
            pl.BlockSpec((tm, LANE), lambda i: (i, 0)),
            pl.BlockSpec((tm, LANE), lambda i: (i, 0)),
        ],
        out_specs=pl.BlockSpec((tm, heads * hw), lambda i: (i, 0)),
        out_shape=jax.ShapeDtypeStruct((t, heads * hw), BF16),
        compiler_params=_params("parallel"),
        name="q_proj",
    )(lat, g_q, wq, wrot, cos, sin)


def _attn_prompt_body(q_ref, k_ref, v_ref, kr_ref, o_ref, kc_ref, *, tq, nope, rope, scale):
    seq = k_ref.shape[0]
    kc_ref[:, :nope] = k_ref[...]
    kc_ref[:, nope:nope + rope] = kr_ref[...].astype(BF16)
    kc_ref[:, nope + rope:] = jnp.zeros((seq, kc_ref.shape[1] - nope - rope), BF16)
    q_chunk = lax.broadcasted_iota(jnp.int32, (tq, tq), 0) // CHUNK
    k_chunk = lax.broadcasted_iota(jnp.int32, (tq, tq), 1) // CHUNK
    visible = k_chunk <= q_chunk
    los = [qi * tq for qi in range(seq // tq)]
    s_diag = [jnp.where(visible, _dot_nt(q_ref[lo:lo + tq, :], kc_ref[lo:lo + tq, :]) * scale, NEG_INF)
              for lo in los]
    s_off = [_dot_nt(q_ref[lo:lo + tq, :], kc_ref[0:lo, :]) * scale if lo else None for lo in los]
    probs = []
    for s_d, s_o in zip(s_diag, s_off):
        m = jnp.max(s_d, axis=-1, keepdims=True)
        if s_o is not None:
            m = jnp.maximum(m, jnp.max(s_o, axis=-1, keepdims=True))
        p_d = jnp.exp(s_d - m)
        l = jnp.sum(p_d, axis=-1, keepdims=True)
        p_o = None
        if s_o is not None:
            p_o = jnp.exp(s_o - m)
            l = l + jnp.sum(p_o, axis=-1, keepdims=True)
            p_o = p_o.astype(BF16)
        probs.append((p_d.astype(BF16), p_o, l))
    for lo, (p_d, p_o, l) in zip(los, probs):
        acc = _dot(p_d, v_ref[lo:lo + tq, :])
        if p_o is not None:
            acc = acc + _dot(p_o, v_ref[0:lo, :])
        o_ref[lo:lo + tq, :] = (acc / l).astype(o_ref.dtype)


def _attn_prompt(q_cat, kv_exp, kr, *, out_rows, batch, seq, heads, nope, rope, vdim, scale):
    tq = _pick(seq, (512, 256, 128, 64))
    hw = nope + LANE
    body = functools.partial(_attn_prompt_body, tq=tq, nope=nope, rope=rope, scale=scale)
    return pl.pallas_call(
        body,
        grid=(batch, heads),
        in_specs=[
            pl.BlockSpec((seq, hw), lambda b, h: (b, h)),
            pl.BlockSpec((seq, nope), lambda b, h: (b, h)),
            pl.BlockSpec((seq, vdim), lambda b, h: (b, heads * nope // vdim + h)),
            pl.BlockSpec((seq, rope), lambda b, h: (b, 0)),
        ],
        out_specs=pl.BlockSpec((seq, vdim), lambda b, h: (b, h)),
        out_shape=jax.ShapeDtypeStruct((out_rows, heads * vdim), BF16),
        scratch_shapes=[pltpu.VMEM((seq, hw), BF16)],
        compiler_params=_params("parallel", "arbitrary"),
        name="attn_prompt",
    )(q_cat, kv_exp, kv_exp, kr)


def _attn_sample_body(q_ref, cc_ref, ck_ref, nc_ref, nk_ref, wuk_ref, wuv_ref, prev_ref, o_ref,
                      qlat_ref, qr_ref, m_ref, l_ref, acc_ref, *, heads, nope, rope, vdim, tk, scale):
    del prev_ref
    ls = q_ref.shape[0]
    past = cc_ref.shape[0]
    hw = nope + LANE
    for h in range(heads):
        qh = q_ref[:, h * hw:(h + 1) * hw]
        qlat_ref[h * ls:(h + 1) * ls, :] = _dot_nt(qh[:, :nope], wuk_ref[:, h * nope:(h + 1) * nope]).astype(BF16)
        qr_ref[h * ls:(h + 1) * ls, :] = qh[:, nope:nope + rope]
    m_ref[...] = jnp.full(m_ref.shape, NEG_INF, F32)
    l_ref[...] = jnp.zeros(l_ref.shape, F32)
    acc_ref[...] = jnp.zeros(acc_ref.shape, F32)

    nrow = heads * ls
    hr = _pick(nrow, (256, 128, 64))

    def step(k, kr):
        groups = [slice(r0, r0 + hr) for r0 in range(0, nrow, hr)]
        scores = [(_dot_nt(qlat_ref[rows, :], k) + _dot_nt(qr_ref[rows, :], kr)) * scale for rows in groups]
        probs, alphas = [], []
        for rows, s in zip(groups, scores):
            m = m_ref[rows, :]
            m_new = jnp.maximum(m, jnp.max(s, axis=-1, keepdims=True))
            alpha = jnp.exp(m - m_new)
            p = jnp.exp(s - m_new)
            l_ref[rows, :] = alpha * l_ref[rows, :] + jnp.sum(p, axis=-1, keepdims=True)
            m_ref[rows, :] = m_new
            probs.append(p.astype(BF16))
            alphas.append(alpha)
        for rows, p, alpha in zip(groups, probs, alphas):
            acc_ref[rows, :] = alpha * acc_ref[rows, :] + _dot(p, k)

    def body(kb, carry):
        off = pl.multiple_of(kb * tk, tk)
        step(cc_ref[pl.ds(off, tk), :].astype(BF16), ck_ref[pl.ds(off, tk), :].astype(BF16))
        return carry

    lax.fori_loop(0, past // tk, body, 0)
    step(nc_ref[...].astype(BF16), nk_ref[...].astype(BF16))
    o_lat = (acc_ref[...] / l_ref[...]).astype(BF16)
    for h in range(heads):
        o_ref[:, h * vdim:(h + 1) * vdim] = _dot(
            o_lat[h * ls:(h + 1) * ls, :], wuv_ref[:, h * vdim:(h + 1) * vdim]).astype(o_ref.dtype)


def _attn_sample(q_cat, cache_ckv, cache_kr, ckv, kr, wuk, wuv, o_prev, *, layer, row0, batch, ls, heads,
                 nope, rope, vdim, scale):
    past, kvl = cache_ckv.shape[2], cache_ckv.shape[3]
    hw = nope + LANE
    tk = _pick(past, (512, 256, 128, 64))
    blk0 = row0 // ls
    body = functools.partial(_attn_sample_body, heads=heads, nope=nope, rope=rope, vdim=vdim, tk=tk,
                             scale=scale)
    return pl.pallas_call(
        body,
        grid=(batch,),
        in_specs=[
            pl.BlockSpec((ls, heads * hw), lambda b: (b, 0)),
            pl.BlockSpec((None, None, past, kvl), lambda b: (layer, b, 0, 0)),
            pl.BlockSpec((None, None, past, rope), lambda b: (layer, b, 0, 0)),
            pl.BlockSpec((ls, kvl), lambda b: (b, 0)),
            pl.BlockSpec((ls, rope), lambda b: (b, 0)),
            pl.BlockSpec((kvl, heads * nope), lambda b: (0, 0)),
            pl.BlockSpec((kvl, heads * vdim), lambda b: (0, 0)),
            pl.BlockSpec(memory_space=pl.ANY),
        ],
        out_specs=pl.BlockSpec((ls, heads * vdim), lambda b: (blk0 + b, 0)),
        out_shape=jax.ShapeDtypeStruct(o_prev.shape, BF16),
        scratch_shapes=[
            pltpu.VMEM((heads * ls, kvl), BF16),
            pltpu.VMEM((heads * ls, rope), BF16),
            pltpu.VMEM((heads * ls, 1), F32),
            pltpu.VMEM((heads * ls, 1), F32),
            pltpu.VMEM((heads * ls, kvl), F32),
        ],
        input_output_aliases={7: 0},
        compiler_params=_params("parallel"),
        name="attn_sample",
    )(q_cat, cache_ckv, cache_kr, ckv, kr, wuk, wuv, o_prev)


S5_PAIRS = 4


def _s5_body(u_ref, h0r_ref, h0i_ref, ar_ref, ai_ref, wb_ref, wc_ref, d_ref,
             z_ref, sr_ref, si_ref, bu_ref, *, tl):
    t = pl.program_id(2)

    @pl.when(t == 0)
    def _():
        sr_ref[...] = h0r_ref[...]
        si_ref[...] = h0i_ref[...]

    u = u_ref[...].reshape(tl * SUBLANE, LANE)
    bu_ref[...] = _dot(u.astype(BF16), wb_ref[0])
    a_re = [jnp.broadcast_to(ar_ref[0, p:p + 1, :], (SUBLANE, LANE)) for p in range(S5_PAIRS)]
    a_im = [jnp.broadcast_to(ai_ref[0, p:p + 1, :], (SUBLANE, LANE)) for p in range(S5_PAIRS)]

    def step(l, carry):
        row = pl.multiple_of(l * SUBLANE, SUBLANE)
        out = []
        for p in range(S5_PAIRS):
            s_re, s_im = carry[2 * p], carry[2 * p + 1]
            c0 = 2 * p * LANE
            n_re = a_re[p] * s_re - a_im[p] * s_im + bu_ref[pl.ds(row, SUBLANE), c0:c0 + LANE]
            n_im = a_re[p] * s_im + a_im[p] * s_re + bu_ref[pl.ds(row, SUBLANE), c0 + LANE:c0 + 2 * LANE]
            bu_ref[pl.ds(row, SUBLANE), c0:c0 + LANE] = n_re
            bu_ref[pl.ds(row, SUBLANE), c0 + LANE:c0 + 2 * LANE] = n_im
            out += [n_re, n_im]
        return tuple(out)

    init = []
    for p in range(S5_PAIRS):
        init += [sr_ref[:, p * LANE:(p + 1) * LANE], si_ref[:, p * LANE:(p + 1) * LANE]]
    fin = lax.fori_loop(0, tl, step, tuple(init), unroll=4)
    for p in range(S5_PAIRS):
        sr_ref[:, p * LANE:(p + 1) * LANE] = fin[2 * p]
        si_ref[:, p * LANE:(p + 1) * LANE] = fin[2 * p + 1]
    y = _dot(bu_ref[...].astype(BF16), wc_ref[0]) + d_ref[...] * u
    z_ref[...] = jax.nn.gelu(y).reshape(tl, SUBLANE, LANE)


def _s5(u_tm, h0_re, h0_im, a_re, a_im, wb, wc, d_skip):
    l, b, w = u_tm.shape
    nblk = w // LANE
    sw = h0_re.shape[1] // nblk
    tl = _pick(l, (256, 128, 64))
    body = functools.partial(_s5_body, tl=tl)
    return pl.pallas_call(
        body,
        grid=(b // SUBLANE, nblk, l // tl),
        in_specs=[
            pl.BlockSpec((tl, SUBLANE, LANE), lambda g, q, t: (t, g, q)),
            pl.BlockSpec((SUBLANE, sw), lambda g, q, t: (g, q)),
            pl.BlockSpec((SUBLANE, sw), lambda g, q, t: (g, q)),
            pl.BlockSpec((1, S5_PAIRS, LANE), lambda g, q, t: (q, 0, 0)),
            pl.BlockSpec((1, S5_PAIRS, LANE), lambda g, q, t: (q, 0, 0)),
            pl.BlockSpec((1, LANE, 2 * sw), lambda g, q, t: (q, 0, 0)),
            pl.BlockSpec((1, 2 * sw, LANE), lambda g, q, t: (q, 0, 0)),
            pl.BlockSpec((1, LANE), lambda g, q, t: (0, q)),
        ],
        out_specs=[
            pl.BlockSpec((tl, SUBLANE, LANE), lambda g, q, t: (t, g, q)),
            pl.BlockSpec((SUBLANE, sw), lambda g, q, t: (g, q)),
            pl.BlockSpec((SUBLANE, sw), lambda g, q, t: (g, q)),
        ],
        out_shape=[
            jax.ShapeDtypeStruct((l, b, w), F32),
            jax.ShapeDtypeStruct(h0_re.shape, F32),
            jax.ShapeDtypeStruct(h0_im.shape, F32),
        ],
        scratch_shapes=[pltpu.VMEM((tl * SUBLANE, 2 * sw), F32)],
        compiler_params=_params("parallel", "parallel", "arbitrary"),
        name="s5",
    )(u_tm, h0_re, h0_im, a_re, a_im, wb, wc, d_skip)


def _s5_weights(a_re, a_im, log_dt, b_re, b_im, c_re, c_im):
    g, p = a_re.shape
    grp = b_re.shape[2]
    gpb = LANE // grp
    nblk = g // gpb
    a_re, a_im = a_re.astype(F32), a_im.astype(F32)
    dt = jnp.exp(log_dt.astype(F32))[:, None]
    mag = jnp.exp(a_re * dt)
    ab_re, ab_im = mag * jnp.cos(a_im * dt), mag * jnp.sin(a_im * dt)
    den = a_re * a_re + a_im * a_im
    q_re = ((ab_re - 1.0) * a_re + ab_im * a_im) / den
    q_im = (ab_im * a_re - (ab_re - 1.0) * a_im) / den
    b_re, b_im = b_re.astype(F32), b_im.astype(F32)
    bb_re = q_re[..., None] * b_re - q_im[..., None] * b_im
    bb_im = q_re[..., None] * b_im + q_im[..., None] * b_re
    eye = jnp.eye(gpb, dtype=F32)

    def blockdiag_in(m):
        m = m.reshape(nblk, gpb, p, grp)
        return jnp.einsum("qjpc,jk->qjckp", m, eye).reshape(nblk, gpb * grp, gpb * p)

    def blockdiag_out(m):
        m = m.reshape(nblk, gpb, grp, p)
        return jnp.einsum("qjcp,jk->qjpkc", m, eye).reshape(nblk, gpb * p, gpb * grp)

    def interleave(re, im, axis):
        shp = list(re.shape)
        n = shp[axis] // LANE
        shp[axis:axis + 1] = [n, LANE]
        st = jnp.stack([re.reshape(shp), im.reshape(shp)], axis=axis + 1)
        shp[axis:axis + 2] = [2 * n * LANE]
        return st.reshape(shp)

    wb = interleave(blockdiag_in(bb_re), blockdiag_in(bb_im), 2).astype(BF16)
    wc = interleave(blockdiag_out(c_re.astype(F32)), blockdiag_out(-c_im.astype(F32)), 1).astype(BF16)
    pairs = g * p // LANE
    ar = ab_re.reshape(pairs // S5_PAIRS, S5_PAIRS, LANE)
    ai = ab_im.reshape(pairs // S5_PAIRS, S5_PAIRS, LANE)
    return ar, ai, wb, wc


def _pool_body(x_ref, g_ref, hist_ref, w_ref, sc_ref, o_ref, tail_ref, ext_ref, lva_ref, lvb_ref,
               *, tl, nt, pos0, gc):
    t = pl.program_id(1)
    front = SUBLANE
    cur0 = front + POOL_PAD
    nrow = cur0 + tl

    @pl.when(t == 0)
    def _():
        ext_ref[0:front, :] = jnp.zeros((front, ext_ref.shape[1]), F32)
        lva_ref[0:front, :] = jnp.zeros((front, gc), F32)
        lvb_ref[0:front, :] = jnp.zeros((front, gc), F32)
        ext_ref[front:cur0, :] = hist_ref[...]

    @pl.when(t > 0)
    def _():
        ext_ref[front:cur0, :] = ext_ref[tl + front:tl + cur0, :]

    x = x_ref[...]
    ext_ref[cur0:nrow, :] = _rms(x, g_ref[...])
    pos = pos0 + t * tl + lax.broadcasted_iota(jnp.int32, (tl, 1), 0)
    for gi, win in enumerate(POOL_WINDOWS):
        cols = slice(gi * gc, (gi + 1) * gc)
        cur = ext_ref[cur0:nrow, cols]
        src, dst, span = None, lva_ref, 1
        while 2 * span < win:
            lo = ext_ref[front - span:nrow - span, cols] if src is None else src[front - span:nrow - span, :]
            hi = ext_ref[front:nrow, cols] if src is None else src[front:nrow, :]
            dst[front:nrow, :] = hi + lo
            src, dst = dst, (lvb_ref if dst is lva_ref else lva_ref)
            span *= 2
        if src is None:
            wsum = cur + ext_ref[cur0 - span:nrow - span, cols]
        else:
            wsum = src[cur0:nrow, :] + src[cur0 - span:nrow - span, :]
        count = jnp.minimum(pos + 1, win).astype(F32)
        delta = (wsum / count - cur).astype(BF16)
        o_ref[:, cols] = x[:, cols] + _dot(delta, w_ref[gi]) * sc_ref[:, cols]

    @pl.when(t == nt - 1)
    def _():
        tail_ref[...] = ext_ref[tl + front:tl + cur0, :]


def _pool(x, g, hist, w, scale, *, row0, batch, seq, pos0):
    t, d = x.shape
    gc = d // len(POOL_WINDOWS)
    tl = _pick(seq, (256, 128, 64))
    nt = seq // tl
    blk0 = row0 // tl
    body = functools.partial(_pool_body, tl=tl, nt=nt, pos0=pos0, gc=gc)
    return pl.pallas_call(
        body,
        grid=(batch, nt),
        in_specs=[
            pl.BlockSpec((tl, d), lambda b, i: (blk0 + b * nt + i, 0)),
            pl.BlockSpec((1, d), lambda b, i: (0, 0)),
            pl.BlockSpec((None, POOL_PAD, d), lambda b, i: (b, 0, 0)),
            pl.BlockSpec(w.shape, lambda b, i: (0, 0, 0)),
            pl.BlockSpec((1, d), lambda b, i: (0, 0)),
        ],
        out_specs=[
            pl.BlockSpec((tl, d), lambda b, i: (blk0 + b * nt + i, 0)),
            pl.BlockSpec((None, POOL_PAD, d), lambda b, i: (b, 0, 0)),
        ],
        out_shape=[
            jax.ShapeDtypeStruct((t, d), F32),
            jax.ShapeDtypeStruct((batch, POOL_PAD, d), F32),
        ],
        scratch_shapes=[pltpu.VMEM((SUBLANE + POOL_PAD + tl, d), F32),
                        pltpu.VMEM((SUBLANE + POOL_PAD + tl, gc), F32),
                        pltpu.VMEM((SUBLANE + POOL_PAD + tl, gc), F32)],
        input_output_aliases={0: 0},
        compiler_params=_params("parallel", "arbitrary"),
        name="pool",
    )(x, g, hist, w, scale)


def _rot_cols(w):
    half = w.shape[-1] // 2
    return jnp.concatenate([-w[..., half:], w[..., :half]], axis=-1)


def _pad_cols(w, n):
    return jnp.pad(w, [(0, 0)] * (w.ndim - 1) + [(0, n - w.shape[-1])])


def kernel(x_prompt, x_sample, cache_mla_ckv, cache_mla_krope, state_s5_re, state_s5_im, cache_pool, ffn1_g, ffn1_w1, ffn1_w3, ffn1_w2, mix_g, ffn2_g, ffn2_w1, ffn2_w3, ffn2_w2, final_g, ab_w_in, mla_g_q, mla_g_kv, mla_w_uq, mla_w_uk, mla_w_uv, s5_a_re, s5_a_im, s5_log_dt, s5_b_re, s5_b_im, s5_c_re, s5_c_im, s5_d, s5_w_glu, ab_w_o, pool_w, pool_scale):
    bp, seq, d = x_prompt.shape
    bs, ls, _ = x_sample.shape
    tp, ts = bp * seq, bs * ls
    tt = tp + ts
    depth = ffn1_g.shape[0]
    past = cache_mla_ckv.shape[2]
    q_lora, heads, qk_head = mla_w_uq.shape[1:]
    kvl, _, nope = mla_w_uk.shape[1:]
    vdim = mla_w_uv.shape[3]
    rope = qk_head - nope
    s5w = s5_d.shape[1]
    groups, nstate = s5_a_re.shape[1:]
    hist_rows = cache_pool.shape[2]
    scale = qk_head ** -0.5
    assert nope == LANE and vdim == LANE and 2 * rope == LANE and kvl + 2 * LANE <= q_lora
    assert bp % SUBLANE == 0 and bs % SUBLANE == 0 and hist_rows == POOL_PAD - 1

    pos = jnp.concatenate([jnp.tile(jnp.arange(seq), bp), past + jnp.tile(jnp.arange(ls), bs)])
    inv = 1.0 / (ROPE_THETA ** (jnp.arange(0, rope, 2, dtype=F32) / rope))
    ang = pos.astype(F32)[:, None] * inv[None, :]
    cos = _pad_cols(jnp.tile(jnp.cos(ang), (1, 2)), LANE)
    sin = _pad_cols(jnp.tile(jnp.sin(ang), (1, 2)), LANE)

    row = lambda v: v.reshape(1, -1).astype(F32)
    fg = row(final_g)

    ffns = []
    for layer in range(depth):
        ffns.append((ffn1_g[layer], (ffn1_w1, ffn1_w3, ffn1_w2), layer))
        ffns.append((ffn2_g[layer], (ffn2_w1, ffn2_w3, ffn2_w2), layer))
    w_next = tuple(_cast(ws, ffns[0][2]) for ws in ffns[0][1])

    def ffn_pair(k, xs):
        nonlocal w_next
        g = row(ffns[k][0])
        w1, w3, w2 = w_next
        ride = (*ffns[k + 1][1], ffns[k + 1][2]) if k + 1 < len(ffns) else None
        (xp, rp), (xs_, rs) = xs
        act_p = _ffn_up(xp, g, w1, w3, row0=rp, nrows=tp)
        act_s = _ffn_up(xs_, g, w1, w3, row0=rs, nrows=ts)
        y, w_next = _ffn_down(act_p, w2, xp, row0=rp, nrows=tp, out_rows=tt, ride=ride)
        y, _ = _ffn_down(act_s, w2, xs_, row0=rs, nrows=ts, out_rows=tt, out_row0=tp, out_alias=y)
        return y

    ckv_out, kr_out, s5re_out, s5im_out, pool_out = [], [], [], [], []
    x = None
    for layer in range(depth):
        src = ((x_prompt.reshape(tp, d), 0), (x_sample.reshape(ts, d), 0)) if x is None else ((x, 0), (x, tp))
        x = ffn_pair(2 * layer, src)
        if layer % 2 == 0:
            i = layer // 2
            w_in = ab_w_in[i]
            w_kr = w_in[:, q_lora + kvl:q_lora + kvl + rope]
            w_mid = jnp.concatenate([w_in[:, q_lora:q_lora + kvl], _pad_cols(w_kr, LANE),
                                     _pad_cols(_rot_cols(w_kr), LANE)], axis=1)
            w_proj = jnp.concatenate([w_in[:, :q_lora], _pad_cols(w_mid, q_lora),
                                      w_in[:, q_lora + kvl + rope:]], axis=1).astype(BF16)
            g_mix = row(mix_g[layer])
            lat_p, u_p = _inproj(x, g_mix, w_proj, n_lat=2 * q_lora, row0=0, nrows=tp)
            lat_s, u_s = _inproj(x, g_mix, w_proj, n_lat=2 * q_lora, row0=tp, nrows=ts)
            g_kv = row(mla_g_kv[i])
            cos_p, sin_p, cos_s, sin_s = cos[:tp], sin[:tp], cos[tp:], sin[tp:]
            ckv_p, kr_p = _kvpost(lat_p, g_kv, cos_p, sin_p, q_lora=q_lora, kv=kvl, rope=rope)
            ckv_s, kr_s = _kvpost(lat_s, g_kv, cos_s, sin_s, q_lora=q_lora, kv=kvl, rope=rope)

            w_uq = mla_w_uq[i]
            wq = jnp.concatenate([w_uq[..., :nope], _pad_cols(w_uq[..., nope:], LANE)], axis=-1)
            wq = wq.reshape(q_lora, -1).astype(BF16)
            wrot = _pad_cols(_rot_cols(w_uq[..., nope:]), LANE).reshape(q_lora, -1).astype(BF16)
            g_q = row(mla_g_q[i])
            q_p = _qproj(lat_p, g_q, wq, wrot, cos_p, sin_p, q_lora=q_lora, heads=heads, nope=nope)
            q_s = _qproj(lat_s, g_q, wq, wrot, cos_s, sin_s, q_lora=q_lora, heads=heads, nope=nope)

            wuk = mla_w_uk[i].reshape(kvl, heads * nope).astype(BF16)
            wuv = mla_w_uv[i].reshape(kvl, heads * vdim).astype(BF16)
            kv_exp = _mm(ckv_p, jnp.concatenate([wuk, wuv], axis=1), out_dtype=BF16, name="kv_expand")
            o_a = _attn_prompt(q_p, kv_exp, kr_p, out_rows=tt, batch=bp, seq=seq, heads=heads, nope=nope,
                               rope=rope, vdim=vdim, scale=scale)
            o_a = _attn_sample(q_s, cache_mla_ckv, cache_mla_krope, ckv_s, kr_s, wuk, wuv, o_a, layer=i,
                               row0=tp, batch=bs, ls=ls, heads=heads, nope=nope, rope=rope, vdim=vdim,
                               scale=scale)

            ar, ai, wb, wc = _s5_weights(s5_a_re[i], s5_a_im[i], s5_log_dt[i], s5_b_re[i], s5_b_im[i],
                                         s5_c_re[i], s5_c_im[i])
            u_p = u_p.reshape(bp, seq, s5w).swapaxes(0, 1)
            u_s = u_s.reshape(bs, ls, s5w).swapaxes(0, 1)
            zeros = jnp.zeros((bp, groups * nstate), F32)
            d_row = row(s5_d[i])
            z_p, pre, pim = _s5(u_p, zeros, zeros, ar, ai, wb, wc, d_row)
            z_s, sre, sim = _s5(u_s, state_s5_re[i].reshape(bs, -1).astype(F32),
                                state_s5_im[i].reshape(bs, -1).astype(F32), ar, ai, wb, wc, d_row)
            wglu = s5_w_glu[i].astype(BF16)
            z_p = z_p.reshape(tp, s5w)
            z_s = z_s.reshape(ts, s5w)
            ob_p = _mm(z_p, wglu, out_dtype=BF16, epilogue="glu", extra=z_p, name="s5_glu")
            ob_s = _mm(z_s, wglu, out_dtype=BF16, epilogue="glu", extra=z_s, name="s5_glu")
            o_b = jnp.concatenate([ob_p.reshape(seq, bp, s5w).swapaxes(0, 1).reshape(tp, s5w),
                                   ob_s.reshape(ls, bs, s5w).swapaxes(0, 1).reshape(ts, s5w)], axis=0)
            x = _outproj(o_a, o_b, ab_w_o[i].astype(BF16), x)

            ckv_out.append((ckv_p.reshape(bp, seq, kvl), ckv_s.reshape(bs, ls, kvl)))
            kr_out.append((kr_p.reshape(bp, seq, rope), kr_s.reshape(bs, ls, rope)))
            s5re_out.append((pre.reshape(bp, groups, nstate), sre.reshape(bs, groups, nstate)))
            s5im_out.append((pim.reshape(bp, groups, nstate), sim.reshape(bs, groups, nstate)))
        else:
            j = layer // 2
            wp = pool_w[j].astype(BF16)
            g_mix, sc = row(mix_g[layer]), row(pool_scale[j])
            hist_p = jnp.zeros((bp, POOL_PAD, d), F32)
            hist_s = jnp.pad(cache_pool[j].astype(F32), ((0, 0), (1, 0), (0, 0)))
            x, tail_p = _pool(x, g_mix, hist_p, wp, sc, row0=0, batch=bp, seq=seq, pos0=0)
            x, tail_s = _pool(x, g_mix, hist_s, wp, sc, row0=tp, batch=bs, seq=ls, pos0=past)
            pool_out.append((tail_p[:, 1:], tail_s[:, 1:]))
        x = ffn_pair(2 * layer + 1, ((x, 0), (x, tp)))

    y_p = _final_norm(x, fg, row0=0, nrows=tp)
    y_s = _final_norm(x, fg, row0=tp, nrows=ts)
    stack = lambda items, k: jnp.stack([it[k] for it in items])
    return (y_p.reshape(bp, seq, d), y_s.reshape(bs, ls, d),
            stack(ckv_out, 0), stack(kr_out, 0), stack(s5re_out, 0), stack(s5im_out, 0), stack(pool_out, 0),
            stack(ckv_out, 1), stack(kr_out, 1), stack(s5re_out, 1), stack(s5im_out, 1), stack(pool_out, 1))
```

```python
import functools
import math

import jax
import jax.numpy as jnp
from jax import lax
from jax.experimental import pallas as pl
from jax.experimental.pallas import tpu as pltpu

F32 = jnp.float32
BF16 = jnp.bfloat16
NORM_EPS = 1e-6
CHUNK = 64
ROPE_THETA = 10000.0
NEG_INF = -1e30
POOL_WINDOWS = (2, 4, 8, 16)
POOL_PAD = 16
LANE = 128
SUBLANE = 8
BF16_SUBLANE = 16
ROW_GROUP = 256
VMEM_LIMIT = 56 * 1024 * 1024


def _params(*sem):
    return pltpu.CompilerParams(dimension_semantics=sem, vmem_limit_bytes=VMEM_LIMIT)


def _pick(n, candidates):
    for c in candidates:
        if n % c == 0:
            return c
    return n


def _rms(x, g):
    y = x * lax.rsqrt(jnp.mean(x * x, axis=-1, keepdims=True) + NORM_EPS)
    return y * g


def _dot(a, b):
    return jnp.dot(a, b, preferred_element_type=F32)


def _dot_nt(a, b):
    return lax.dot_general(a, b, (((1,), (1,)), ((), ())), preferred_element_type=F32)


def _cast_body(w_ref, o_ref):
    o_ref[...] = w_ref[...].astype(BF16)


def _cast(w_stack, layer):
    _, r, c = w_stack.shape
    br = _pick(r, (256, 128, 64, 16))
    return pl.pallas_call(
        _cast_body,
        grid=(r // br,),
        in_specs=[pl.BlockSpec((None, br, c), lambda i: (layer, i, 0))],
        out_specs=pl.BlockSpec((br, c), lambda i: (i, 0)),
        out_shape=jax.ShapeDtypeStruct((r, c), BF16),
        compiler_params=_params("parallel"),
        name="cast",
    )(w_stack)


FF_TILE = 512


def _ffn_up_body(x_hbm, g_ref, w1_ref, w3_ref, o_ref, h_ref, x_ref, sem, *, halves, last_cols, rb0):
    tm, tn = o_ref.shape
    hr = tm // halves
    i, j = pl.program_id(0), pl.program_id(1)
    ni, nj = pl.num_programs(0), pl.num_programs(1)

    def x_copy(blk):
        start = pl.multiple_of((rb0 + blk) * tm, tm)
        return pltpu.make_async_copy(x_hbm.at[pl.ds(start, tm), :], x_ref, sem)

    @pl.when((i == 0) & (j == 0))
    def _():
        x_copy(0).start()

    @pl.when(j == 0)
    def _():
        x_copy(i).wait()
        for r0 in range(0, tm, hr):
            h_ref[r0:r0 + hr, :] = _rms(x_ref[r0:r0 + hr, :], g_ref[...]).astype(BF16)

    @pl.when((j == nj - 1) & (i + 1 < ni))
    def _():
        x_copy(i + 1).start()

    def block(cols):
        for r0 in range(0, tm, hr):
            h = h_ref[r0:r0 + hr, :]
            a = _dot(h, w1_ref[:, :cols])
            b = _dot(h, w3_ref[:, :cols])
            o_ref[r0:r0 + hr, :cols] = (a * jax.nn.sigmoid(a) * b).astype(BF16)

    if last_cols == tn:
        block(tn)
    else:
        pl.when(j < nj - 1)(lambda: block(tn))
        pl.when(j == nj - 1)(lambda: block(last_cols))


def _ffn_up(x, g, w1, w3, *, row0, nrows):
    d = x.shape[1]
    dff = w1.shape[1]
    tm = _pick(nrows, (1024, 512, 256, 128, 64))
    tn = FF_TILE
    nj = pl.cdiv(dff, tn)
    body = functools.partial(_ffn_up_body, halves=max(1, tm // ROW_GROUP), last_cols=dff - (nj - 1) * tn,
                             rb0=row0 // tm)
    return pl.pallas_call(
        body,
        grid=(nrows // tm, nj),
        in_specs=[
            pl.BlockSpec(memory_space=pl.ANY),
            pl.BlockSpec((1, d), lambda i, j: (0, 0)),
            pl.BlockSpec((d, tn), lambda i, j: (0, j)),
            pl.BlockSpec((d, tn), lambda i, j: (0, j)),
        ],
        out_specs=pl.BlockSpec((tm, tn), lambda i, j: (i, j)),
        out_shape=jax.ShapeDtypeStruct((nrows, dff), BF16),
        scratch_shapes=[pltpu.VMEM((tm, d), BF16), pltpu.VMEM((tm, d), F32), pltpu.SemaphoreType.DMA(())],
        compiler_params=_params("arbitrary", "arbitrary"),
        name="ffn_up",
    )(x, g, w1, w3)


def _ffn_down_body(*refs, ride, aliased, ni):
    refs = list(refs)
    act_ref, w2_ref, x_ref = refs[:3]
    del refs[:3]
    if ride:
        nxt = refs[:3]
        del refs[:3]
    if aliased:
        del refs[:1]
    o_ref = refs.pop(0)
    i = pl.program_id(1)

    @pl.when(i < ni)
    def _():
        o_ref[...] = x_ref[...] + 0.5 * _dot(act_ref[...], w2_ref[...])

    @pl.when(i >= ni)
    def _():
        o_ref[...] = jnp.zeros_like(o_ref)

    if ride:
        for src, dst in zip(nxt, refs):
            dst[...] = src[...].astype(BF16)


def _round_up(n, m):
    return -(-n // m) * m


def _ffn_down(act, w2, x, *, row0, nrows, out_rows=None, out_row0=0, out_alias=None, ride=None):
    dff, d = w2.shape
    tm = _pick(nrows, (512, 256, 128, 64))
    tn = _pick(d, (512, 256, 128))
    ni, nj = nrows // tm, d // tn
    out_rows = nrows if out_rows is None else out_rows
    rb0, ob0 = row0 // tm, out_row0 // tm
    nfill = 0 if out_alias is not None else (out_rows - out_row0 - nrows) // tm
    assert out_alias is not None or (out_row0 + nrows + nfill * tm == out_rows)
    own = lambda i: jnp.minimum(i, ni - 1)
    in_specs = [
        pl.BlockSpec((tm, dff), lambda j, i: (own(i), 0)),
        pl.BlockSpec((dff, tn), lambda j, i: (0, j), pipeline_mode=pl.Buffered(1)),
        pl.BlockSpec((tm, tn), lambda j, i: (rb0 + own(i), j)),
    ]
    args = [act, w2, x]
    out_specs = [pl.BlockSpec((tm, tn), lambda j, i: (ob0 + i, j))]
    out_shape = [jax.ShapeDtypeStruct((out_rows, d), F32)]
    if ride is not None:
        n1, n3, n2, layer = ride
        assert d % ni == 0 and d % nj == 0
        t13 = (d // ni, _round_up(pl.cdiv(dff, nj), LANE))
        t2 = (_round_up(pl.cdiv(dff, ni), BF16_SUBLANE), d // nj)
        assert t13[0] % BF16_SUBLANE == 0 and t2[1] % LANE == 0
        assert (nj - 1) * t13[1] < dff and (ni - 1) * t2[0] < dff
        in_specs += [
            pl.BlockSpec((None,) + t13, lambda j, i: (layer, own(i), j)),
            pl.BlockSpec((None,) + t13, lambda j, i: (layer, own(i), j)),
            pl.BlockSpec((None,) + t2, lambda j, i: (layer, own(i), j)),
        ]
        args += [n1, n3, n2]
        out_specs += [pl.BlockSpec(t13, lambda j, i: (own(i), j)), pl.BlockSpec(t13, lambda j, i: (own(i), j)),
                      pl.BlockSpec(t2, lambda j, i: (own(i), j))]
        out_shape += [jax.ShapeDtypeStruct((d, dff), BF16), jax.ShapeDtypeStruct((d, dff), BF16),
                      jax.ShapeDtypeStruct((dff, d), BF16)]
    aliases = {}
    if out_alias is not None:
        aliases = {len(args): 0}
        in_specs.append(pl.BlockSpec(memory_space=pl.ANY))
        args.append(out_alias)
    body = functools.partial(_ffn_down_body, ride=ride is not None, aliased=out_alias is not None, ni=ni)
    outs = pl.pallas_call(
        body,
        grid=(nj, ni + nfill),
        in_specs=in_specs,
        out_specs=out_specs,
        out_shape=out_shape,
        input_output_aliases=aliases,
        compiler_params=_params("arbitrary", "arbitrary"),
        name="ffn_down",
    )(*args)
    return outs[0], tuple(outs[1:])


def _final_norm_body(x_ref, g_ref, o_ref):
    o_ref[...] = _rms(x_ref[...], g_ref[...])


def _final_norm(x, g, *, row0, nrows):
    d = x.shape[1]
    tm = _pick(nrows, (512, 256, 128, 64))
    rb0 = row0 // tm
    return pl.pallas_call(
        _final_norm_body,
        grid=(nrows // tm,),
        in_specs=[pl.BlockSpec((tm, d), lambda i: (rb0 + i, 0)), pl.BlockSpec((1, d), lambda i: (0, 0))],
        out_specs=pl.BlockSpec((tm, d), lambda i: (i, 0)),
        out_shape=jax.ShapeDtypeStruct((nrows, d), F32),
        compiler_params=_params("parallel"),
        name="final_norm",
    )(x, g)


def _mm_body(*refs, use_scratch, epilogue):
    refs = list(refs)
    lhs_ref = refs.pop(0)
    w_ref = refs.pop(0)
    e_ref = refs.pop(0) if epilogue is not None else None
    o_ref = refs.pop(0)
    if use_scratch:
        s_ref = refs.pop(0)

        @pl.when(pl.program_id(1) == 0)
        def _():
            s_ref[...] = lhs_ref[...].astype(BF16)

        lhs = s_ref[...]
    else:
        lhs = lhs_ref[...]
    acc = _dot(lhs, w_ref[...])
    if epilogue == "glu":
        acc = e_ref[...] * jax.nn.sigmoid(acc)
    o_ref[...] = acc.astype(o_ref.dtype)


def _mm(lhs, w, *, out_dtype, epilogue=None, extra=None, name):
    t = lhs.shape[0]
    k, n = w.shape
    tm = _pick(t, (512, 256, 128, 64))
    tn = _pick(n, (1024, 512, 256, 128))
    use_scratch = lhs.dtype != BF16
    in_specs = [pl.BlockSpec((tm, k), lambda i, j: (i, 0)),
                pl.BlockSpec((k, tn), lambda i, j: (0, j))]
    args = [lhs, w]
    if epilogue is not None:
        in_specs.append(pl.BlockSpec((tm, tn), lambda i, j: (i, j)))
        args.append(extra)
    body = functools.partial(_mm_body, use_scratch=use_scratch, epilogue=epilogue)
    return pl.pallas_call(
        body,
        grid=(t // tm, n // tn),
        in_specs=in_specs,
        out_specs=pl.BlockSpec((tm, tn), lambda i, j: (i, j)),
        out_shape=jax.ShapeDtypeStruct((t, n), out_dtype),
        scratch_shapes=[pltpu.VMEM((tm, k), BF16)] if use_scratch else [],
        compiler_params=_params("parallel", "arbitrary"),
        name=name,
    )(*args)


def _inproj_body(x_ref, g_ref, w_ref, a_ref, u_ref, h_ref, *, na):
    j = pl.program_id(1)

    @pl.when(j == 0)
    def _():
        h_ref[...] = _rms(x_ref[...], g_ref[...]).astype(BF16)

    @pl.when(j < na)
    def _():
        a_ref[...] = _dot(h_ref[...], w_ref[...])

    @pl.when(j >= na)
    def _():
        u_ref[...] = _dot(h_ref[...], w_ref[...])


def _inproj(x, g, w, *, n_lat, row0, nrows):
    d = x.shape[1]
    n = w.shape[1]
    t = nrows
    tm = _pick(t, (512, 256, 128, 64))
    tn = math.gcd(_pick(n_lat, (512, 256, 128)), _pick(n - n_lat, (512, 256, 128)))
    na = n_lat // tn
    rb0 = row0 // tm
    body = functools.partial(_inproj_body, na=na)
    return pl.pallas_call(
        body,
        grid=(t // tm, n // tn),
        in_specs=[
            pl.BlockSpec((tm, d), lambda i, j: (rb0 + i, 0)),
            pl.BlockSpec((1, d), lambda i, j: (0, 0)),
            pl.BlockSpec((d, tn), lambda i, j: (0, j)),
        ],
        out_specs=[
            pl.BlockSpec((tm, tn), lambda i, j: (i, jnp.minimum(j, na - 1))),
            pl.BlockSpec((tm, tn), lambda i, j: (i, jnp.maximum(j - na, 0))),
        ],
        out_shape=[
            jax.ShapeDtypeStruct((t, n_lat), F32),
            jax.ShapeDtypeStruct((t, n - n_lat), F32),
        ],
        scratch_shapes=[pltpu.VMEM((tm, d), BF16)],
        compiler_params=_params("parallel", "arbitrary"),
        name="in_proj",
    )(x, g, w)


def _outproj_body(oa_ref, ob_ref, wa_ref, wb_ref, r_ref, o_ref):
    o_ref[...] = r_ref[...] + _dot(oa_ref[...], wa_ref[...]) + _dot(ob_ref[...], wb_ref[...])


def _outproj(oa, ob, w, res):
    t, ka = oa.shape
    kb = ob.shape[1]
    n = w.shape[1]
    assert ka == kb
    tm = _pick(t, (512, 256, 128, 64))
    tn = _pick(n, (1024, 512, 256, 128))
    return pl.pallas_call(
        _outproj_body,
        grid=(t // tm, n // tn),
        in_specs=[
            pl.BlockSpec((tm, ka), lambda i, j: (i, 0)),
            pl.BlockSpec((tm, kb), lambda i, j: (i, 0)),
            pl.BlockSpec((ka, tn), lambda i, j: (0, j)),
            pl.BlockSpec((kb, tn), lambda i, j: (1, j)),
            pl.BlockSpec((tm, tn), lambda i, j: (i, j)),
        ],
        out_specs=pl.BlockSpec((tm, tn), lambda i, j: (i, j)),
        out_shape=jax.ShapeDtypeStruct((t, n), F32),
        compiler_params=_params("parallel", "arbitrary"),
        name="out_proj",
    )(oa, ob, w, w, res)


def _kvpost_body(p_ref, g_ref, cos_ref, sin_ref, ckv_ref, kr_ref, *, kv, rope):
    ckv_ref[...] = _rms(p_ref[:, :kv], g_ref[...])
    k = p_ref[:, kv:kv + rope]
    k_rot = p_ref[:, kv + LANE:kv + LANE + rope]
    kr_ref[...] = k * cos_ref[:, :rope] + k_rot * sin_ref[:, :rope]


def _kvpost(lat, g_kv, cos, sin, *, q_lora, kv, rope):
    nrows = lat.shape[0]
    tm = _pick(nrows, (512, 256, 128, 64))
    body = functools.partial(_kvpost_body, kv=kv, rope=rope)
    return pl.pallas_call(
        body,
        grid=(nrows // tm,),
        in_specs=[
            pl.BlockSpec((tm, q_lora), lambda i: (i, 1)),
            pl.BlockSpec((1, kv), lambda i: (0, 0)),
            pl.BlockSpec((tm, LANE), lambda i: (i, 0)),
            pl.BlockSpec((tm, LANE), lambda i: (i, 0)),
        ],
        out_specs=[
            pl.BlockSpec((tm, kv), lambda i: (i, 0)),
            pl.BlockSpec((tm, rope), lambda i: (i, 0)),
        ],
        out_shape=[
            jax.ShapeDtypeStruct((nrows, kv), F32),
            jax.ShapeDtypeStruct((nrows, rope), F32),
        ],
        compiler_params=_params("parallel"),
        name="kv_post",
    )(lat, g_kv, cos, sin)


def _qproj_body(c_ref, g_ref, wq_ref, wrot_ref, cos_ref, sin_ref, o_ref, *, heads, nope):
    c = _rms(c_ref[...], g_ref[...]).astype(BF16)
    cos = cos_ref[...]
    sin = sin_ref[...]
    hw = nope + LANE
    for h in range(heads):
        main = _dot(c, wq_ref[:, h * hw:(h + 1) * hw])
        rot = _dot(c, wrot_ref[:, h * LANE:(h + 1) * LANE])
        o_ref[:, h * hw:h * hw + nope] = main[:, :nope].astype(BF16)
        o_ref[:, h * hw + nope:(h + 1) * hw] = (main[:, nope:] * cos + rot * sin).astype(BF16)


def _qproj(lat, g_q, wq, wrot, cos, sin, *, q_lora, heads, nope):
    t = lat.shape[0]
    tm = _pick(t, (512, 256, 128, 64))
    hw = nope + LANE
    body = functools.partial(_qproj_body, heads=heads, nope=nope)
    return pl.pallas_call(
        body,
        grid=(t // tm,),
        in_specs=[
            pl.BlockSpec((tm, q_lora), lambda i: (i, 0)),
            pl.BlockSpec((1, q_lora), lambda i: (0, 0)),
            pl.BlockSpec((q_lora, heads * hw), lambda i: (0, 0)),
            pl.BlockSpec((q_lora, heads * LANE), lambda i: (0, 0)),
            pl.BlockSpec((tm, LANE), lambda i: (i, 0)),
            pl.BlockSpec((tm, LANE), lambda i: (i, 0)),
        ],
        out_specs=pl.BlockSpec((tm, heads * hw), lambda i: (i, 0)),
        out_shape=jax.ShapeDtypeStruct((t, heads * hw), BF16),
        compiler_params=_params("parallel"),
        name="q_proj",
    )(lat, g_q, wq, wrot, cos, sin)


def _attn_prompt_body(q_ref, k_ref, v_ref, kr_ref, o_ref, kc_ref, *, batch, **kw):
    b = pl.program_id(0)

    @pl.when(b < batch)
    def _():
        _attn_prompt_block(q_ref, k_ref, v_ref, kr_ref, o_ref, kc_ref, **kw)

    @pl.when(b >= batch)
    def _():
        o_ref[...] = jnp.zeros_like(o_ref)


def _attn_prompt_block(q_ref, k_ref, v_ref, kr_ref, o_ref, kc_ref, *, tq, nope, rope, scale):
    seq = k_ref.shape[0]
    kc_ref[:, :nope] = k_ref[...]
    kc_ref[:, nope:nope + rope] = kr_ref[...].astype(BF16)
    kc_ref[:, nope + rope:] = jnp.zeros((seq, kc_ref.shape[1] - nope - rope), BF16)
    q_chunk = lax.broadcasted_iota(jnp.int32, (tq, tq), 0) // CHUNK
    k_chunk = lax.broadcasted_iota(jnp.int32, (tq, tq), 1) // CHUNK
    visible = k_chunk <= q_chunk
    los = [qi * tq for qi in range(seq // tq)]
    s_diag = [jnp.where(visible, _dot_nt(q_ref[lo:lo + tq, :], kc_ref[lo:lo + tq, :]) * scale, NEG_INF)
              for lo in los]
    s_off = [_dot_nt(q_ref[lo:lo + tq, :], kc_ref[0:lo, :]) * scale if lo else None for lo in los]
    probs = []
    for s_d, s_o in zip(s_diag, s_off):
        m = jnp.max(s_d, axis=-1, keepdims=True)
        if s_o is not None:
            m = jnp.maximum(m, jnp.max(s_o, axis=-1, keepdims=True))
        p_d = jnp.exp(s_d - m)
        l = jnp.sum(p_d, axis=-1, keepdims=True)
        p_o = None
        if s_o is not None:
            p_o = jnp.exp(s_o - m)
            l = l + jnp.sum(p_o, axis=-1, keepdims=True)
            p_o = p_o.astype(BF16)
        probs.append((p_d.astype(BF16), p_o, l))
    for lo, (p_d, p_o, l) in zip(los, probs):
        acc = _dot(p_d, v_ref[lo:lo + tq, :])
        if p_o is not None:
            acc = acc + _dot(p_o, v_ref[0:lo, :])
        o_ref[lo:lo + tq, :] = (acc / l).astype(o_ref.dtype)


def _attn_prompt(q_cat, kv_exp, kr, *, out_rows, batch, seq, heads, nope, rope, vdim, scale):
    tq = _pick(seq, (512, 256, 128, 64))
    hw = nope + LANE
    body = functools.partial(_attn_prompt_body, batch=batch, tq=tq, nope=nope, rope=rope, scale=scale)
    nfill = (out_rows - batch * seq) // seq
    assert (batch + nfill) * seq == out_rows
    own = lambda b: jnp.minimum(b, batch - 1)
    return pl.pallas_call(
        body,
        grid=(batch + nfill, heads),
        in_specs=[
            pl.BlockSpec((seq, hw), lambda b, h: (own(b), h)),
            pl.BlockSpec((seq, nope), lambda b, h: (own(b), h)),
            pl.BlockSpec((seq, vdim), lambda b, h: (own(b), heads * nope // vdim + h)),
            pl.BlockSpec((seq, rope), lambda b, h: (own(b), 0)),
        ],
        out_specs=pl.BlockSpec((seq, vdim), lambda b, h: (b, h)),
        out_shape=jax.ShapeDtypeStruct((out_rows, heads * vdim), BF16),
        scratch_shapes=[pltpu.VMEM((seq, hw), BF16)],
        compiler_params=_params("parallel", "arbitrary"),
        name="attn_prompt",
    )(q_cat, kv_exp, kv_exp, kr)


def _attn_sample_body(q_ref, cc_ref, ck_ref, nc_ref, nk_ref, wuk_ref, wuv_ref, prev_ref, o_ref,
                      qlat_ref, qr_ref, m_ref, l_ref, acc_ref, *, heads, nope, rope, vdim, tk, scale):
    del prev_ref
    ls = q_ref.shape[0]
    past = cc_ref.shape[0]
    hw = nope + LANE
    for h in range(heads):
        qh = q_ref[:, h * hw:(h + 1) * hw]
        qlat_ref[h * ls:(h + 1) * ls, :] = _dot_nt(qh[:, :nope], wuk_ref[:, h * nope:(h + 1) * nope]).astype(BF16)
        qr_ref[h * ls:(h + 1) * ls, :] = qh[:, nope:nope + rope]
    m_ref[...] = jnp.full(m_ref.shape, NEG_INF, F32)
    l_ref[...] = jnp.zeros(l_ref.shape, F32)
    acc_ref[...] = jnp.zeros(acc_ref.shape, F32)

    nrow = heads * ls
    hr = _pick(nrow, (256, 128, 64))

    def step(k, kr):
        groups = [slice(r0, r0 + hr) for r0 in range(0, nrow, hr)]
        scores = [(_dot_nt(qlat_ref[rows, :], k) + _dot_nt(qr_ref[rows, :], kr)) * scale for rows in groups]
        probs, alphas = [], []
        for rows, s in zip(groups, scores):
            m = m_ref[rows, :]
            m_new = jnp.maximum(m, jnp.max(s, axis=-1, keepdims=True))
            alpha = jnp.exp(m - m_new)
            p = jnp.exp(s - m_new)
            l_ref[rows, :] = alpha * l_ref[rows, :] + jnp.sum(p, axis=-1, keepdims=True)
            m_ref[rows, :] = m_new
            probs.append(p.astype(BF16))
            alphas.append(alpha)
        for rows, p, alpha in zip(groups, probs, alphas):
            acc_ref[rows, :] = alpha * acc_ref[rows, :] + _dot(p, k)

    def body(kb, carry):
        off = pl.multiple_of(kb * tk, tk)
        step(cc_ref[pl.ds(off, tk), :].astype(BF16), ck_ref[pl.ds(off, tk), :].astype(BF16))
        return carry

    lax.fori_loop(0, past // tk, body, 0)
    step(nc_ref[...].astype(BF16), nk_ref[...].astype(BF16))
    o_lat = (acc_ref[...] / l_ref[...]).astype(BF16)
    for h in range(heads):
        o_ref[:, h * vdim:(h + 1) * vdim] = _dot(
            o_lat[h * ls:(h + 1) * ls, :], wuv_ref[:, h * vdim:(h + 1) * vdim]).astype(o_ref.dtype)


def _attn_sample(q_cat, cache_ckv, cache_kr, ckv, kr, wuk, wuv, o_prev, *, layer, row0, batch, ls, heads,
                 nope, rope, vdim, scale):
    past, kvl = cache_ckv.shape[2], cache_ckv.shape[3]
    hw = nope + LANE
    tk = _pick(past, (512, 256, 128, 64))
    blk0 = row0 // ls
    body = functools.partial(_attn_sample_body, heads=heads, nope=nope, rope=rope, vdim=vdim, tk=tk,
                             scale=scale)
    return pl.pallas_call(
        body,
        grid=(batch,),
        in_specs=[
            pl.BlockSpec((ls, heads * hw), lambda b: (b, 0)),
            pl.BlockSpec((None, None, past, kvl), lambda b: (layer, b, 0, 0)),
            pl.BlockSpec((None, None, past, rope), lambda b: (layer, b, 0, 0)),
            pl.BlockSpec((ls, kvl), lambda b: (b, 0)),
            pl.BlockSpec((ls, rope), lambda b: (b, 0)),
            pl.BlockSpec((kvl, heads * nope), lambda b: (0, 0)),
            pl.BlockSpec((kvl, heads * vdim), lambda b: (0, 0)),
            pl.BlockSpec(memory_space=pl.ANY),
        ],
        out_specs=pl.BlockSpec((ls, heads * vdim), lambda b: (blk0 + b, 0)),
        out_shape=jax.ShapeDtypeStruct(o_prev.shape, BF16),
        scratch_shapes=[
            pltpu.VMEM((heads * ls, kvl), BF16),
            pltpu.VMEM((heads * ls, rope), BF16),
            pltpu.VMEM((heads * ls, 1), F32),
            pltpu.VMEM((heads * ls, 1), F32),
            pltpu.VMEM((heads * ls, kvl), F32),
        ],
        input_output_aliases={7: 0},
        compiler_params=_params("parallel"),
        name="attn_sample",
    )(q_cat, cache_ckv, cache_kr, ckv, kr, wuk, wuv, o_prev)


S5_PAIRS = 4


def _s5_body(u_ref, h0r_ref, h0i_ref, ar_ref, ai_ref, wb_ref, wc_ref, d_ref,
             z_ref, sr_ref, si_ref, bu_ref, *, tl):
    t = pl.program_id(2)

    @pl.when(t == 0)
    def _():
        sr_ref[...] = h0r_ref[...]
        si_ref[...] = h0i_ref[...]

    u = u_ref[...].reshape(tl * SUBLANE, LANE)
    bu_ref[...] = _dot(u.astype(BF16), wb_ref[0])
    a_re = [jnp.broadcast_to(ar_ref[0, p:p + 1, :], (SUBLANE, LANE)) for p in range(S5_PAIRS)]
    a_im = [jnp.broadcast_to(ai_ref[0, p:p + 1, :], (SUBLANE, LANE)) for p in range(S5_PAIRS)]

    def step(l, carry):
        row = pl.multiple_of(l * SUBLANE, SUBLANE)
        out = []
        for p in range(S5_PAIRS):
            s_re, s_im = carry[2 * p], carry[2 * p + 1]
            c0 = 2 * p * LANE
            n_re = a_re[p] * s_re - a_im[p] * s_im + bu_ref[pl.ds(row, SUBLANE), c0:c0 + LANE]
            n_im = a_re[p] * s_im + a_im[p] * s_re + bu_ref[pl.ds(row, SUBLANE), c0 + LANE:c0 + 2 * LANE]
            bu_ref[pl.ds(row, SUBLANE), c0:c0 + LANE] = n_re
            bu_ref[pl.ds(row, SUBLANE), c0 + LANE:c0 + 2 * LANE] = n_im
            out += [n_re, n_im]
        return tuple(out)

    init = []
    for p in range(S5_PAIRS):
        init += [sr_ref[:, p * LANE:(p + 1) * LANE], si_ref[:, p * LANE:(p + 1) * LANE]]
    fin = lax.fori_loop(0, tl, step, tuple(init), unroll=4)
    for p in range(S5_PAIRS):
        sr_ref[:, p * LANE:(p + 1) * LANE] = fin[2 * p]
        si_ref[:, p * LANE:(p + 1) * LANE] = fin[2 * p + 1]
    y = _dot(bu_ref[...].astype(BF16), wc_ref[0]) + d_ref[...] * u
    z_ref[...] = jax.nn.gelu(y).reshape(tl, SUBLANE, LANE)


def _s5(u_tm, h0_re, h0_im, a_re, a_im, wb, wc, d_skip):
    l, b, w = u_tm.shape
    nblk = w // LANE
    sw = h0_re.shape[1] // nblk
    tl = _pick(l, (256, 128, 64))
    body = functools.partial(_s5_body, tl=tl)
    return pl.pallas_call(
        body,
        grid=(b // SUBLANE, nblk, l // tl),
        in_specs=[
            pl.BlockSpec((tl, SUBLANE, LANE), lambda g, q, t: (t, g, q)),
            pl.BlockSpec((SUBLANE, sw), lambda g, q, t: (g, q)),
            pl.BlockSpec((SUBLANE, sw), lambda g, q, t: (g, q)),
            pl.BlockSpec((1, S5_PAIRS, LANE), lambda g, q, t: (q, 0, 0)),
            pl.BlockSpec((1, S5_PAIRS, LANE), lambda g, q, t: (q, 0, 0)),
            pl.BlockSpec((1, LANE, 2 * sw), lambda g, q, t: (q, 0, 0)),
            pl.BlockSpec((1, 2 * sw, LANE), lambda g, q, t: (q, 0, 0)),
            pl.BlockSpec((1, LANE), lambda g, q, t: (0, q)),
        ],
        out_specs=[
            pl.BlockSpec((tl, SUBLANE, LANE), lambda g, q, t: (t, g, q)),
            pl.BlockSpec((SUBLANE, sw), lambda g, q, t: (g, q)),
            pl.BlockSpec((SUBLANE, sw), lambda g, q, t: (g, q)),
        ],
        out_shape=[
            jax.ShapeDtypeStruct((l, b, w), F32),
            jax.ShapeDtypeStruct(h0_re.shape, F32),
            jax.ShapeDtypeStruct(h0_im.shape, F32),
        ],
        scratch_shapes=[pltpu.VMEM((tl * SUBLANE, 2 * sw), F32)],
        compiler_params=_params("parallel", "parallel", "arbitrary"),
        name="s5",
    )(u_tm, h0_re, h0_im, a_re, a_im, wb, wc, d_skip)


def _s5_weights(a_re, a_im, log_dt, b_re, b_im, c_re, c_im):
    g, p = a_re.shape
    grp = b_re.shape[2]
    gpb = LANE // grp
    nblk = g // gpb
    a_re, a_im = a_re.astype(F32), a_im.astype(F32)
    dt = jnp.exp(log_dt.astype(F32))[:, None]
    mag = jnp.exp(a_re * dt)
    ab_re, ab_im = mag * jnp.cos(a_im * dt), mag * jnp.sin(a_im * dt)
    den = a_re * a_re + a_im * a_im
    q_re = ((ab_re - 1.0) * a_re + ab_im * a_im) / den
    q_im = (ab_im * a_re - (ab_re - 1.0) * a_im) / den
    b_re, b_im = b_re.astype(F32), b_im.astype(F32)
    bb_re = q_re[..., None] * b_re - q_im[..., None] * b_im
    bb_im = q_re[..., None] * b_im + q_im[..., None] * b_re
    eye = jnp.eye(gpb, dtype=F32)

    def blockdiag_in(m):
        m = m.reshape(nblk, gpb, p, grp)
        return jnp.einsum("qjpc,jk->qjckp", m, eye).reshape(nblk, gpb * grp, gpb * p)

    def blockdiag_out(m):
        m = m.reshape(nblk, gpb, grp, p)
        return jnp.einsum("qjcp,jk->qjpkc", m, eye).reshape(nblk, gpb * p, gpb * grp)

    def interleave(re, im, axis):
        shp = list(re.shape)
        n = shp[axis] // LANE
        shp[axis:axis + 1] = [n, LANE]
        st = jnp.stack([re.reshape(shp), im.reshape(shp)], axis=axis + 1)
        shp[axis:axis + 2] = [2 * n * LANE]
        return st.reshape(shp)

    wb = interleave(blockdiag_in(bb_re), blockdiag_in(bb_im), 2).astype(BF16)
    wc = interleave(blockdiag_out(c_re.astype(F32)), blockdiag_out(-c_im.astype(F32)), 1).astype(BF16)
    pairs = g * p // LANE
    ar = ab_re.reshape(pairs // S5_PAIRS, S5_PAIRS, LANE)
    ai = ab_im.reshape(pairs // S5_PAIRS, S5_PAIRS, LANE)
    return ar, ai, wb, wc


def _pool_body(x_ref, g_ref, hist_ref, w_ref, sc_ref, o_ref, tail_ref, ext_ref, lva_ref, lvb_ref,
               *, tl, nt, pos0, gc):
    t = pl.program_id(1)
    front = SUBLANE
    cur0 = front + POOL_PAD
    nrow = cur0 + tl

    @pl.when(t == 0)
    def _():
        ext_ref[0:front, :] = jnp.zeros((front, ext_ref.shape[1]), F32)
        lva_ref[0:front, :] = jnp.zeros((front, gc), F32)
        lvb_ref[0:front, :] = jnp.zeros((front, gc), F32)
        ext_ref[front:cur0, :] = hist_ref[...]

    @pl.when(t > 0)
    def _():
        ext_ref[front:cur0, :] = ext_ref[tl + front:tl + cur0, :]

    x = x_ref[...]
    ext_ref[cur0:nrow, :] = _rms(x, g_ref[...])
    pos = pos0 + t * tl + lax.broadcasted_iota(jnp.int32, (tl, 1), 0)
    for gi, win in enumerate(POOL_WINDOWS):
        cols = slice(gi * gc, (gi + 1) * gc)
        cur = ext_ref[cur0:nrow, cols]
        src, dst, span = None, lva_ref, 1
        while 2 * span < win:
            lo = ext_ref[front - span:nrow - span, cols] if src is None else src[front - span:nrow - span, :]
            hi = ext_ref[front:nrow, cols] if src is None else src[front:nrow, :]
            dst[front:nrow, :] = hi + lo
            src, dst = dst, (lvb_ref if dst is lva_ref else lva_ref)
            span *= 2
        if src is None:
            wsum = cur + ext_ref[cur0 - span:nrow - span, cols]
        else:
            wsum = src[cur0:nrow, :] + src[cur0 - span:nrow - span, :]
        count = jnp.minimum(pos + 1, win).astype(F32)
        delta = (wsum / count - cur).astype(BF16)
        o_ref[:, cols] = x[:, cols] + _dot(delta, w_ref[gi]) * sc_ref[:, cols]

    @pl.when(t == nt - 1)
    def _():
        tail_ref[...] = ext_ref[tl + front:tl + cur0, :]


def _pool(x, g, hist, w, scale, *, row0, batch, seq, pos0):
    t, d = x.shape
    gc = d // len(POOL_WINDOWS)
    tl = _pick(seq, (256, 128, 64))
    nt = seq // tl
    blk0 = row0 // tl
    body = functools.partial(_pool_body, tl=tl, nt=nt, pos0=pos0, gc=gc)
    return pl.pallas_call(
        body,
        grid=(batch, nt),
        in_specs=[
            pl.BlockSpec((tl, d), lambda b, i: (blk0 + b * nt + i, 0)),
            pl.BlockSpec((1, d), lambda b, i: (0, 0)),
            pl.BlockSpec((None, POOL_PAD, d), lambda b, i: (b, 0, 0)),
            pl.BlockSpec(w.shape, lambda b, i: (0, 0, 0)),
            pl.BlockSpec((1, d), lambda b, i: (0, 0)),
        ],
        out_specs=[
            pl.BlockSpec((tl, d), lambda b, i: (blk0 + b * nt + i, 0)),
            pl.BlockSpec((None, POOL_PAD, d), lambda b, i: (b, 0, 0)),
        ],
        out_shape=[
            jax.ShapeDtypeStruct((t, d), F32),
            jax.ShapeDtypeStruct((batch, POOL_PAD, d), F32),
        ],
        scratch_shapes=[pltpu.VMEM((SUBLANE + POOL_PAD + tl, d), F32),
                        pltpu.VMEM((SUBLANE + POOL_PAD + tl, gc), F32),
                        pltpu.VMEM((SUBLANE + POOL_PAD + tl, gc), F32)],
        input_output_aliases={0: 0},
        compiler_params=_params("parallel", "arbitrary"),
        name="pool",
    )(x, g, hist, w, scale)


def _rot_cols(w):
    half = w.shape[-1] // 2
    return jnp.concatenate([-w[..., half:], w[..., :half]], axis=-1)


def _pad_cols(w, n):
    return jnp.pad(w, [(0, 0)] * (w.ndim - 1) + [(0, n - w.shape[-1])])


def kernel(x_prompt, x_sample, cache_mla_ckv, cache_mla_krope, state_s5_re, state_s5_im, cache_pool, ffn1_g, ffn1_w1, ffn1_w3, ffn1_w2, mix_g, ffn2_g, ffn2_w1, ffn2_w3, ffn2_w2, final_g, ab_w_in, mla_g_q, mla_g_kv, mla_w_uq, mla_w_uk, mla_w_uv, s5_a_re, s5_a_im, s5_log_dt, s5_b_re, s5_b_im, s5_c_re, s5_c_im, s5_d, s5_w_glu, ab_w_o, pool_w, pool_scale):
    bp, seq, d = x_prompt.shape
    bs, ls, _ = x_sample.shape
    tp, ts = bp * seq, bs * ls
    tt = tp + ts
    depth = ffn1_g.shape[0]
    past = cache_mla_ckv.shape[2]
    q_lora, heads, qk_head = mla_w_uq.shape[1:]
    kvl, _, nope = mla_w_uk.shape[1:]
    vdim = mla_w_uv.shape[3]
    rope = qk_head - nope
    s5w = s5_d.shape[1]
    groups, nstate = s5_a_re.shape[1:]
    hist_rows = cache_pool.shape[2]
    scale = qk_head ** -0.5
    assert nope == LANE and vdim == LANE and 2 * rope == LANE and kvl + 2 * LANE <= q_lora
    assert bp % SUBLANE == 0 and bs % SUBLANE == 0 and hist_rows == POOL_PAD - 1

    pos = jnp.concatenate([jnp.tile(jnp.arange(seq), bp), past + jnp.tile(jnp.arange(ls), bs)])
    inv = 1.0 / (ROPE_THETA ** (jnp.arange(0, rope, 2, dtype=F32) / rope))
    ang = pos.astype(F32)[:, None] * inv[None, :]
    cos = _pad_cols(jnp.tile(jnp.cos(ang), (1, 2)), LANE)
    sin = _pad_cols(jnp.tile(jnp.sin(ang), (1, 2)), LANE)

    row = lambda v: v.reshape(1, -1).astype(F32)
    fg = row(final_g)

    ffns = []
    for layer in range(depth):
        ffns.append((ffn1_g[layer], (ffn1_w1, ffn1_w3, ffn1_w2), layer))
        ffns.append((ffn2_g[layer], (ffn2_w1, ffn2_w3, ffn2_w2), layer))
    w_next = tuple(_cast(ws, ffns[0][2]) for ws in ffns[0][1])

    def ffn_pair(k, xs):
        nonlocal w_next
        g = row(ffns[k][0])
        w1, w3, w2 = w_next
        ride = (*ffns[k + 1][1], ffns[k + 1][2]) if k + 1 < len(ffns) else None
        (xp, rp), (xs_, rs) = xs
        act_p = _ffn_up(xp, g, w1, w3, row0=rp, nrows=tp)
        act_s = _ffn_up(xs_, g, w1, w3, row0=rs, nrows=ts)
        y, w_next = _ffn_down(act_p, w2, xp, row0=rp, nrows=tp, out_rows=tt, ride=ride)
        y, _ = _ffn_down(act_s, w2, xs_, row0=rs, nrows=ts, out_rows=tt, out_row0=tp, out_alias=y)
        return y

    ckv_out, kr_out, s5re_out, s5im_out, pool_out = [], [], [], [], []
    x = None
    for layer in range(depth):
        src = ((x_prompt.reshape(tp, d), 0), (x_sample.reshape(ts, d), 0)) if x is None else ((x, 0), (x, tp))
        x = ffn_pair(2 * layer, src)
        if layer % 2 == 0:
            i = layer // 2
            w_in = ab_w_in[i]
            w_kr = w_in[:, q_lora + kvl:q_lora + kvl + rope]
            w_mid = jnp.concatenate([w_in[:, q_lora:q_lora + kvl], _pad_cols(w_kr, LANE),
                                     _pad_cols(_rot_cols(w_kr), LANE)], axis=1)
            w_proj = jnp.concatenate([w_in[:, :q_lora], _pad_cols(w_mid, q_lora),
                                      w_in[:, q_lora + kvl + rope:]], axis=1).astype(BF16)
            g_mix = row(mix_g[layer])
            lat_p, u_p = _inproj(x, g_mix, w_proj, n_lat=2 * q_lora, row0=0, nrows=tp)
            lat_s, u_s = _inproj(x, g_mix, w_proj, n_lat=2 * q_lora, row0=tp, nrows=ts)
            g_kv = row(mla_g_kv[i])
            cos_p, sin_p, cos_s, sin_s = cos[:tp], sin[:tp], cos[tp:], sin[tp:]
            ckv_p, kr_p = _kvpost(lat_p, g_kv, cos_p, sin_p, q_lora=q_lora, kv=kvl, rope=rope)
            ckv_s, kr_s = _kvpost(lat_s, g_kv, cos_s, sin_s, q_lora=q_lora, kv=kvl, rope=rope)

            w_uq = mla_w_uq[i]
            wq = jnp.concatenate([w_uq[..., :nope], _pad_cols(w_uq[..., nope:], LANE)], axis=-1)
            wq = wq.reshape(q_lora, -1).astype(BF16)
            wrot = _pad_cols(_rot_cols(w_uq[..., nope:]), LANE).reshape(q_lora, -1).astype(BF16)
            g_q = row(mla_g_q[i])
            q_p = _qproj(lat_p, g_q, wq, wrot, cos_p, sin_p, q_lora=q_lora, heads=heads, nope=nope)
            q_s = _qproj(lat_s, g_q, wq, wrot, cos_s, sin_s, q_lora=q_lora, heads=heads, nope=nope)

            wuk = mla_w_uk[i].reshape(kvl, heads * nope).astype(BF16)
            wuv = mla_w_uv[i].reshape(kvl, heads * vdim).astype(BF16)
            kv_exp = _mm(ckv_p, jnp.concatenate([wuk, wuv], axis=1), out_dtype=BF16, name="kv_expand")
            o_a = _attn_prompt(q_p, kv_exp, kr_p, out_rows=tt, batch=bp, seq=seq, heads=heads, nope=nope,
                               rope=rope, vdim=vdim, scale=scale)
            o_a = _attn_sample(q_s, cache_mla_ckv, cache_mla_krope, ckv_s, kr_s, wuk, wuv, o_a, layer=i,
                               row0=tp, batch=bs, ls=ls, heads=heads, nope=nope, rope=rope, vdim=vdim,
                               scale=scale)

            ar, ai, wb, wc = _s5_weights(s5_a_re[i], s5_a_im[i], s5_log_dt[i], s5_b_re[i], s5_b_im[i],
                                         s5_c_re[i], s5_c_im[i])
            u_p = u_p.reshape(bp, seq, s5w).swapaxes(0, 1)
            u_s = u_s.reshape(bs, ls, s5w).swapaxes(0, 1)
            zeros = jnp.zeros((bp, groups * nstate), F32)
            d_row = row(s5_d[i])
            z_p, pre, pim = _s5(u_p, zeros, zeros, ar, ai, wb, wc, d_row)
            z_s, sre, sim = _s5(u_s, state_s5_re[i].reshape(bs, -1).astype(F32),
                                state_s5_im[i].reshape(bs, -1).astype(F32), ar, ai, wb, wc, d_row)
            wglu = s5_w_glu[i].astype(BF16)
            z_p = z_p.reshape(tp, s5w)
            z_s = z_s.reshape(ts, s5w)
            ob_p = _mm(z_p, wglu, out_dtype=BF16, epilogue="glu", extra=z_p, name="s5_glu")
            ob_s = _mm(z_s, wglu, out_dtype=BF16, epilogue="glu", extra=z_s, name="s5_glu")
            o_b = jnp.concatenate([ob_p.reshape(seq, bp, s5w).swapaxes(0, 1).reshape(tp, s5w),
                                   ob_s.reshape(ls, bs, s5w).swapaxes(0, 1).reshape(ts, s5w)], axis=0)
            x = _outproj(o_a, o_b, ab_w_o[i].astype(BF16), x)

            ckv_out.append((ckv_p.reshape(bp, seq, kvl), ckv_s.reshape(bs, ls, kvl)))
            kr_out.append((kr_p.reshape(bp, seq, rope), kr_s.reshape(bs, ls, rope)))
            s5re_out.append((pre.reshape(bp, groups, nstate), sre.reshape(bs, groups, nstate)))
            s5im_out.append((pim.reshape(bp, groups, nstate), sim.reshape(bs, groups, nstate)))
        else:
            j = layer // 2
            wp = pool_w[j].astype(BF16)
            g_mix, sc = row(mix_g[layer]), row(pool_scale[j])
            hist_p = jnp.zeros((bp, POOL_PAD, d), F32)
            hist_s = jnp.pad(cache_pool[j].astype(F32), ((0, 0), (1, 0), (0, 0)))
            x, tail_p = _pool(x, g_mix, hist_p, wp, sc, row0=0, batch=bp, seq=seq, pos0=0)
            x, tail_s = _pool(x, g_mix, hist_s, wp, sc, row0=tp, batch=bs, seq=ls, pos0=past)
            pool_out.append((tail_p[:, 1:], tail_s[:, 1:]))
        x = ffn_pair(2 * layer + 1, ((x, 0), (x, tp)))

    y_p = _final_norm(x, fg, row0=0, nrows=tp)
    y_s = _final_norm(x, fg, row0=tp, nrows=ts)
    stack = lambda items, k: jnp.stack([it[k] for it in items])
    return (y_p.reshape(bp, seq, d), y_s.reshape(bs, ls, d),
            stack(ckv_out, 0), stack(kr_out, 0), stack(s5re_out, 0), stack(s5im_out, 0), stack(pool_out, 0),
            stack(ckv_out, 1), stack(kr_out, 1), stack(s5re_out, 1), stack(s5im_out, 1), stack(pool_out, 1))
```

```python
import functools
import math

import jax
import jax.numpy as jnp
from jax import lax
from jax.experimental import pallas as pl
from jax.experimental.pallas import tpu as pltpu

F32 = jnp.float32
BF16 = jnp.bfloat16
NORM_EPS = 1e-6
CHUNK = 64
ROPE_THETA = 10000.0
NEG_INF = -1e30
POOL_WINDOWS = (2, 4, 8, 16)
POOL_PAD = 16
LANE = 128
SUBLANE = 8
BF16_SUBLANE = 16
ROW_GROUP = 256
VMEM_LIMIT = 56 * 1024 * 1024


def _params(*sem):
    return pltpu.CompilerParams(dimension_semantics=sem, vmem_limit_bytes=VMEM_LIMIT)


def _pick(n, candidates):
    for c in candidates:
        if n % c == 0:
            return c
    return n


def _rms(x, g):
    y = x * lax.rsqrt(jnp.mean(x * x, axis=-1, keepdims=True) + NORM_EPS)
    return y * g


def _dot(a, b):
    return jnp.dot(a, b, preferred_element_type=F32)


def _dot_nt(a, b):
    return lax.dot_general(a, b, (((1,), (1,)), ((), ())), preferred_element_type=F32)


def _cast_body(w_ref, o_ref):
    o_ref[...] = w_ref[...].astype(BF16)


def _cast(w_stack, layer):
    _, r, c = w_stack.shape
    br = _pick(r, (256, 128, 64, 16))
    return pl.pallas_call(
        _cast_body,
        grid=(r // br,),
        in_specs=[pl.BlockSpec((None, br, c), lambda i: (layer, i, 0))],
        out_specs=pl.BlockSpec((br, c), lambda i: (i, 0)),
        out_shape=jax.ShapeDtypeStruct((r, c), BF16),
        compiler_params=_params("parallel"),
        name="cast",
    )(w_stack)


FF_TILE = 512


def _ffn_up_body(x_hbm, g_ref, w1_ref, w3_ref, o_ref, h_ref, x_ref, sem, *, halves, last_cols, rb0):
    tm, tn = o_ref.shape
    hr = tm // halves
    i, j = pl.program_id(0), pl.program_id(1)
    ni, nj = pl.num_programs(0), pl.num_programs(1)

    def x_copy(blk):
        start = pl.multiple_of((rb0 + blk) * tm, tm)
        return pltpu.make_async_copy(x_hbm.at[pl.ds(start, tm), :], x_ref, sem)

    @pl.when((i == 0) & (j == 0))
    def _():
        x_copy(0).start()

    @pl.when(j == 0)
    def _():
        x_copy(i).wait()
        for r0 in range(0, tm, hr):
            h_ref[r0:r0 + hr, :] = _rms(x_ref[r0:r0 + hr, :], g_ref[...]).astype(BF16)

    @pl.when((j == nj - 1) & (i + 1 < ni))
    def _():
        x_copy(i + 1).start()

    def block(cols):
        for r0 in range(0, tm, hr):
            h = h_ref[r0:r0 + hr, :]
            a = _dot(h, w1_ref[:, :cols])
            b = _dot(h, w3_ref[:, :cols])
            o_ref[r0:r0 + hr, :cols] = (a * jax.nn.sigmoid(a) * b).astype(BF16)

    if last_cols == tn:
        block(tn)
    else:
        pl.when(j < nj - 1)(lambda: block(tn))
        pl.when(j == nj - 1)(lambda: block(last_cols))


def _ffn_up(x, g, w1, w3, *, row0, nrows):
    d = x.shape[1]
    dff = w1.shape[1]
    tm = _pick(nrows, (1024, 512, 256, 128, 64))
    tn = FF_TILE
    nj = pl.cdiv(dff, tn)
    body = functools.partial(_ffn_up_body, halves=max(1, tm // ROW_GROUP), last_cols=dff - (nj - 1) * tn,
                             rb0=row0 // tm)
    return pl.pallas_call(
        body,
        grid=(nrows // tm, nj),
        in_specs=[
            pl.BlockSpec(memory_space=pl.ANY),
            pl.BlockSpec((1, d), lambda i, j: (0, 0)),
            pl.BlockSpec((d, tn), lambda i, j: (0, j)),
            pl.BlockSpec((d, tn), lambda i, j: (0, j)),
        ],
        out_specs=pl.BlockSpec((tm, tn), lambda i, j: (i, j)),
        out_shape=jax.ShapeDtypeStruct((nrows, dff), BF16),
        scratch_shapes=[pltpu.VMEM((tm, d), BF16), pltpu.VMEM((tm, d), F32), pltpu.SemaphoreType.DMA(())],
        compiler_params=_params("arbitrary", "arbitrary"),
        name="ffn_up",
    )(x, g, w1, w3)


def _ffn_down_body(*refs, ride):
    refs = list(refs)
    act_ref, w2_ref, x_ref = refs[:3]
    del refs[:3]
    if ride:
        nxt = refs[:3]
        del refs[:3]
    o_ref = refs.pop(0)
    o_ref[...] = x_ref[...] + 0.5 * _dot(act_ref[...], w2_ref[...])
    if ride:
        for src, dst in zip(nxt, refs):
            dst[...] = src[...].astype(BF16)


def _round_up(n, m):
    return -(-n // m) * m


def _ffn_down(act, w2, x, *, ride=None):
    dff, d = w2.shape
    nrows = act.shape[0]
    tm = _pick(nrows, (512, 256, 128, 64))
    tn = _pick(d, (512, 256, 128))
    ni, nj = nrows // tm, d // tn
    in_specs = [
        pl.BlockSpec((tm, dff), lambda j, i: (i, 0)),
        pl.BlockSpec((dff, tn), lambda j, i: (0, j), pipeline_mode=pl.Buffered(1)),
        pl.BlockSpec((tm, tn), lambda j, i: (i, j)),
    ]
    args = [act, w2, x]
    out_specs = [pl.BlockSpec((tm, tn), lambda j, i: (i, j))]
    out_shape = [jax.ShapeDtypeStruct((nrows, d), F32)]
    if ride is not None:
        n1, n3, n2, layer = ride
        assert d % ni == 0 and d % nj == 0
        t13 = (d // ni, _round_up(pl.cdiv(dff, nj), LANE))
        t2 = (_round_up(pl.cdiv(dff, ni), BF16_SUBLANE), d // nj)
        assert t13[0] % BF16_SUBLANE == 0 and t2[1] % LANE == 0
        assert (nj - 1) * t13[1] < dff and (ni - 1) * t2[0] < dff
        in_specs += [
            pl.BlockSpec((None,) + t13, lambda j, i: (layer, i, j)),
            pl.BlockSpec((None,) + t13, lambda j, i: (layer, i, j)),
            pl.BlockSpec((None,) + t2, lambda j, i: (layer, i, j)),
        ]
        args += [n1, n3, n2]
        out_specs += [pl.BlockSpec(t13, lambda j, i: (i, j)), pl.BlockSpec(t13, lambda j, i: (i, j)),
                      pl.BlockSpec(t2, lambda j, i: (i, j))]
        out_shape += [jax.ShapeDtypeStruct((d, dff), BF16), jax.ShapeDtypeStruct((d, dff), BF16),
                      jax.ShapeDtypeStruct((dff, d), BF16)]
    body = functools.partial(_ffn_down_body, ride=ride is not None)
    outs = pl.pallas_call(
        body,
        grid=(nj, ni),
        in_specs=in_specs,
        out_specs=out_specs,
        out_shape=out_shape,
        compiler_params=_params("arbitrary", "arbitrary"),
        name="ffn_down",
    )(*args)
    return outs[0], tuple(outs[1:])


def _final_norm_body(x_ref, g_ref, o_ref):
    o_ref[...] = _rms(x_ref[...], g_ref[...])


def _final_norm(x, g, *, row0, nrows):
    d = x.shape[1]
    tm = _pick(nrows, (512, 256, 128, 64))
    rb0 = row0 // tm
    return pl.pallas_call(
        _final_norm_body,
        grid=(nrows // tm,),
        in_specs=[pl.BlockSpec((tm, d), lambda i: (rb0 + i, 0)), pl.BlockSpec((1, d), lambda i: (0, 0))],
        out_specs=pl.BlockSpec((tm, d), lambda i: (i, 0)),
        out_shape=jax.ShapeDtypeStruct((nrows, d), F32),
        compiler_params=_params("parallel"),
        name="final_norm",
    )(x, g)


def _mm_body(*refs, use_scratch, epilogue):
    refs = list(refs)
    lhs_ref = refs.pop(0)
    w_ref = refs.pop(0)
    e_ref = refs.pop(0) if epilogue is not None else None
    o_ref = refs.pop(0)
    if use_scratch:
        s_ref = refs.pop(0)

        @pl.when(pl.program_id(1) == 0)
        def _():
            s_ref[...] = lhs_ref[...].astype(BF16)

        lhs = s_ref[...]
    else:
        lhs = lhs_ref[...]
    acc = _dot(lhs, w_ref[...])
    if epilogue == "glu":
        acc = e_ref[...] * jax.nn.sigmoid(acc)
    o_ref[...] = acc.astype(o_ref.dtype)


def _mm(lhs, w, *, out_dtype, epilogue=None, extra=None, name):
    t = lhs.shape[0]
    k, n = w.shape
    tm = _pick(t, (512, 256, 128, 64))
    tn = _pick(n, (1024, 512, 256, 128))
    use_scratch = lhs.dtype != BF16
    in_specs = [pl.BlockSpec((tm, k), lambda i, j: (i, 0)),
                pl.BlockSpec((k, tn), lambda i, j: (0, j))]
    args = [lhs, w]
    if epilogue is not None:
        in_specs.append(pl.BlockSpec((tm, tn), lambda i, j: (i, j)))
        args.append(extra)
    body = functools.partial(_mm_body, use_scratch=use_scratch, epilogue=epilogue)
    return pl.pallas_call(
        body,
        grid=(t // tm, n // tn),
        in_specs=in_specs,
        out_specs=pl.BlockSpec((tm, tn), lambda i, j: (i, j)),
        out_shape=jax.ShapeDtypeStruct((t, n), out_dtype),
        scratch_shapes=[pltpu.VMEM((tm, k), BF16)] if use_scratch else [],
        compiler_params=_params("parallel", "arbitrary"),
        name=name,
    )(*args)


def _inproj_body(x_ref, g_ref, w_ref, a_ref, u_ref, h_ref, *, na):
    j = pl.program_id(1)

    @pl.when(j == 0)
    def _():
        h_ref[...] = _rms(x_ref[...], g_ref[...]).astype(BF16)

    @pl.when(j < na)
    def _():
        a_ref[...] = _dot(h_ref[...], w_ref[...])

    @pl.when(j >= na)
    def _():
        u_ref[...] = _dot(h_ref[...], w_ref[...])


def _inproj(x, g, w, *, n_lat, row0, nrows):
    d = x.shape[1]
    n = w.shape[1]
    t = nrows
    tm = _pick(t, (512, 256, 128, 64))
    tn = math.gcd(_pick(n_lat, (512, 256, 128)), _pick(n - n_lat, (512, 256, 128)))
    na = n_lat // tn
    rb0 = row0 // tm
    body = functools.partial(_inproj_body, na=na)
    return pl.pallas_call(
        body,
        grid=(t // tm, n // tn),
        in_specs=[
            pl.BlockSpec((tm, d), lambda i, j: (rb0 + i, 0)),
            pl.BlockSpec((1, d), lambda i, j: (0, 0)),
            pl.BlockSpec((d, tn), lambda i, j: (0, j)),
        ],
        out_specs=[
            pl.BlockSpec((tm, tn), lambda i, j: (i, jnp.minimum(j, na - 1))),
            pl.BlockSpec((tm, tn), lambda i, j: (i, jnp.maximum(j - na, 0))),
        ],
        out_shape=[
            jax.ShapeDtypeStruct((t, n_lat), F32),
            jax.ShapeDtypeStruct((t, n - n_lat), F32),
        ],
        scratch_shapes=[pltpu.VMEM((tm, d), BF16)],
        compiler_params=_params("parallel", "arbitrary"),
        name="in_proj",
    )(x, g, w)


def _outproj_body(oa_ref, ob_ref, wa_ref, wb_ref, r_ref, o_ref):
    o_ref[...] = r_ref[...] + _dot(oa_ref[...], wa_ref[...]) + _dot(ob_ref[...], wb_ref[...])


def _outproj(oa, ob, w, res):
    t, ka = oa.shape
    kb = ob.shape[1]
    n = w.shape[1]
    assert ka == kb
    tm = _pick(t, (512, 256, 128, 64))
    tn = _pick(n, (1024, 512, 256, 128))
    return pl.pallas_call(
        _outproj_body,
        grid=(t // tm, n // tn),
        in_specs=[
            pl.BlockSpec((tm, ka), lambda i, j: (i, 0)),
            pl.BlockSpec((tm, kb), lambda i, j: (i, 0)),
            pl.BlockSpec((ka, tn), lambda i, j: (0, j)),
            pl.BlockSpec((kb, tn), lambda i, j: (1, j)),
            pl.BlockSpec((tm, tn), lambda i, j: (i, j)),
        ],
        out_specs=pl.BlockSpec((tm, tn), lambda i, j: (i, j)),
        out_shape=jax.ShapeDtypeStruct((t, n), F32),
        compiler_params=_params("parallel", "arbitrary"),
        name="out_proj",
    )(oa, ob, w, w, res)


def _kvpost_body(p_ref, g_ref, cos_ref, sin_ref, ckv_ref, kr_ref, *, kv, rope):
    ckv_ref[...] = _rms(p_ref[:, :kv], g_ref[...])
    k = p_ref[:, kv:kv + rope]
    k_rot = p_ref[:, kv + LANE:kv + LANE + rope]
    kr_ref[...] = k * cos_ref[:, :rope] + k_rot * sin_ref[:, :rope]


def _kvpost(lat, g_kv, cos, sin, *, q_lora, kv, rope):
    nrows = lat.shape[0]
    tm = _pick(nrows, (512, 256, 128, 64))
    body = functools.partial(_kvpost_body, kv=kv, rope=rope)
    return pl.pallas_call(
        body,
        grid=(nrows // tm,),
        in_specs=[
            pl.BlockSpec((tm, q_lora), lambda i: (i, 1)),
            pl.BlockSpec((1, kv), lambda i: (0, 0)),
            pl.BlockSpec((tm, LANE), lambda i: (i, 0)),
            pl.BlockSpec((tm, LANE), lambda i: (i, 0)),
        ],
        out_specs=[
            pl.BlockSpec((tm, kv), lambda i: (i, 0)),
            pl.BlockSpec((tm, rope), lambda i: (i, 0)),
        ],
        out_shape=[
            jax.ShapeDtypeStruct((nrows, kv), F32),
            jax.ShapeDtypeStruct((nrows, rope), F32),
        ],
        compiler_params=_params("parallel"),
        name="kv_post",
    )(lat, g_kv, cos, sin)


def _qproj_body(c_ref, g_ref, wq_ref, wrot_ref, cos_ref, sin_ref, o_ref, *, heads, nope):
    c = _rms(c_ref[...], g_ref[...]).astype(BF16)
    cos = cos_ref[...]
    sin = sin_ref[...]
    hw = nope + LANE
    for h in range(heads):
        main = _dot(c, wq_ref[:, h * hw:(h + 1) * hw])
        rot = _dot(c, wrot_ref[:, h * LANE:(h + 1) * LANE])
        o_ref[:, h * hw:h * hw + nope] = main[:, :nope].astype(BF16)
        o_ref[:, h * hw + nope:(h + 1) * hw] = (main[:, nope:] * cos + rot * sin).astype(BF16)


def _qproj(lat, g_q, wq, wrot, cos, sin, *, q_lora, heads, nope):
    t = lat.shape[0]
    tm = _pick(t, (512, 256, 128, 64))
    hw = nope + LANE
    body = functools.partial(_qproj_body, heads=heads, nope=nope)
    return pl.pallas_call(
        body,
        grid=(t // tm,),
        in_specs=[
            pl.BlockSpec((tm, q_lora), lambda i: (i, 0)),
            pl.BlockSpec((1, q_lora), lambda i: (0, 0)),
            pl.BlockSpec((q_lora, heads * hw), lambda i: (0, 0)),
            pl.BlockSpec((q_lora, heads * LANE), lambda i: (0, 0)),
            pl.BlockSpec((tm, LANE), lambda i: (i, 0)),
            pl.BlockSpec((tm, LANE), lambda i: (i, 0)),
        ],
        out_specs=pl.BlockSpec((tm, heads * hw), lambda i: (i, 0)),
        out_shape=jax.ShapeDtypeStruct((t, heads * hw), BF16),
        compiler_params=_params("parallel"),
        name="q_proj",
    )(lat, g_q, wq, wrot, cos, sin)


def _attn_prompt_body(q_ref, k_ref, v_ref, kr_ref, o_ref, kc_ref, *, tq, nope, rope, scale):
    seq = k_ref.shape[0]
    kc_ref[:, :nope] = k_ref[...]
    kc_ref[:, nope:nope + rope] = kr_ref[...].astype(BF16)
    kc_ref[:, nope + rope:] = jnp.zeros((seq, kc_ref.shape[1] - nope - rope), BF16)
    q_chunk = lax.broadcasted_iota(jnp.int32, (tq, tq), 0) // CHUNK
    k_chunk = lax.broadcasted_iota(jnp.int32, (tq, tq), 1) // CHUNK
    visible = k_chunk <= q_chunk
    los = [qi * tq for qi in range(seq // tq)]
    s_diag = [jnp.where(visible, _dot_nt(q_ref[lo:lo + tq, :], kc_ref[lo:lo + tq, :]) * scale, NEG_INF)
              for lo in los]
    s_off = [_dot_nt(q_ref[lo:lo + tq, :], kc_ref[0:lo, :]) * scale if lo else None for lo in los]
    probs = []
    for s_d, s_o in zip(s_diag, s_off):
        m = jnp.max(s_d, axis=-1, keepdims=True)
        if s_o is not None:
            m = jnp.maximum(m, jnp.max(s_o, axis=-1, keepdims=True))
        p_d = jnp.exp(s_d - m)
        l = jnp.sum(p_d, axis=-1, keepdims=True)
        p_o = None
        if s_o is not None:
            p_o = jnp.exp(s_o - m)
            l = l + jnp.sum(p_o, axis=-1, keepdims=True)
            p_o = p_o.astype(BF16)
        probs.append((p_d.astype(BF16), p_o, l))
    for lo, (p_d, p_o, l) in zip(los, probs):
        acc = _dot(p_d, v_ref[lo:lo + tq, :])
        if p_o is not None:
            acc = acc + _dot(p_o, v_ref[0:lo, :])
        o_ref[lo:lo + tq, :] = (acc / l).astype(o_ref.dtype)


def _attn_prompt(q_cat, kv_exp, kr, *, batch, seq, heads, nope, rope, vdim, scale):
    tq = _pick(seq, (512, 256, 128, 64))
    hw = nope + LANE
    body = functools.partial(_attn_prompt_body, tq=tq, nope=nope, rope=rope, scale=scale)
    return pl.pallas_call(
        body,
        grid=(batch, heads),
        in_specs=[
            pl.BlockSpec((seq, hw), lambda b, h: (b, h)),
            pl.BlockSpec((seq, nope), lambda b, h: (b, h)),
            pl.BlockSpec((seq, vdim), lambda b, h: (b, heads * nope // vdim + h)),
            pl.BlockSpec((seq, rope), lambda b, h: (b, 0)),
        ],
        out_specs=pl.BlockSpec((seq, vdim), lambda b, h: (b, h)),
        out_shape=jax.ShapeDtypeStruct((batch * seq, heads * vdim), BF16),
        scratch_shapes=[pltpu.VMEM((seq, hw), BF16)],
        compiler_params=_params("parallel", "arbitrary"),
        name="attn_prompt",
    )(q_cat, kv_exp, kv_exp, kr)


def _attn_sample_body(q_ref, cc_ref, ck_ref, nc_ref, nk_ref, wuk_ref, wuv_ref, o_ref,
                      qlat_ref, qr_ref, m_ref, l_ref, acc_ref, *, heads, nope, rope, vdim, tk, scale):
    ls = q_ref.shape[0]
    past = cc_ref.shape[0]
    hw = nope + LANE
    for h in range(heads):
        qh = q_ref[:, h * hw:(h + 1) * hw]
        qlat_ref[h * ls:(h + 1) * ls, :] = _dot_nt(qh[:, :nope], wuk_ref[:, h * nope:(h + 1) * nope]).astype(BF16)
        qr_ref[h * ls:(h + 1) * ls, :] = qh[:, nope:nope + rope]
    m_ref[...] = jnp.full(m_ref.shape, NEG_INF, F32)
    l_ref[...] = jnp.zeros(l_ref.shape, F32)
    acc_ref[...] = jnp.zeros(acc_ref.shape, F32)

    nrow = heads * ls
    hr = _pick(nrow, (256, 128, 64))

    def step(k, kr):
        groups = [slice(r0, r0 + hr) for r0 in range(0, nrow, hr)]
        scores = [(_dot_nt(qlat_ref[rows, :], k) + _dot_nt(qr_ref[rows, :], kr)) * scale for rows in groups]
        probs, alphas = [], []
        for rows, s in zip(groups, scores):
            m = m_ref[rows, :]
            m_new = jnp.maximum(m, jnp.max(s, axis=-1, keepdims=True))
            alpha = jnp.exp(m - m_new)
            p = jnp.exp(s - m_new)
            l_ref[rows, :] = alpha * l_ref[rows, :] + jnp.sum(p, axis=-1, keepdims=True)
            m_ref[rows, :] = m_new
            probs.append(p.astype(BF16))
            alphas.append(alpha)
        for rows, p, alpha in zip(groups, probs, alphas):
            acc_ref[rows, :] = alpha * acc_ref[rows, :] + _dot(p, k)

    def body(kb, carry):
        off = pl.multiple_of(kb * tk, tk)
        step(cc_ref[pl.ds(off, tk), :].astype(BF16), ck_ref[pl.ds(off, tk), :].astype(BF16))
        return carry

    lax.fori_loop(0, past // tk, body, 0)
    step(nc_ref[...].astype(BF16), nk_ref[...].astype(BF16))
    o_lat = (acc_ref[...] / l_ref[...]).astype(BF16)
    for h in range(heads):
        o_ref[:, h * vdim:(h + 1) * vdim] = _dot(
            o_lat[h * ls:(h + 1) * ls, :], wuv_ref[:, h * vdim:(h + 1) * vdim]).astype(o_ref.dtype)


def _attn_sample(q_cat, cache_ckv, cache_kr, ckv, kr, wuk, wuv, *, layer, batch, ls, heads,
                 nope, rope, vdim, scale):
    past, kvl = cache_ckv.shape[2], cache_ckv.shape[3]
    hw = nope + LANE
    tk = _pick(past, (512, 256, 128, 64))
    body = functools.partial(_attn_sample_body, heads=heads, nope=nope, rope=rope, vdim=vdim, tk=tk,
                             scale=scale)
    return pl.pallas_call(
        body,
        grid=(batch,),
        in_specs=[
            pl.BlockSpec((ls, heads * hw), lambda b: (b, 0)),
            pl.BlockSpec((None, None, past, kvl), lambda b: (layer, b, 0, 0)),
            pl.BlockSpec((None, None, past, rope), lambda b: (layer, b, 0, 0)),
            pl.BlockSpec((ls, kvl), lambda b: (b, 0)),
            pl.BlockSpec((ls, rope), lambda b: (b, 0)),
            pl.BlockSpec((kvl, heads * nope), lambda b: (0, 0)),
            pl.BlockSpec((kvl, heads * vdim), lambda b: (0, 0)),
        ],
        out_specs=pl.BlockSpec((ls, heads * vdim), lambda b: (b, 0)),
        out_shape=jax.ShapeDtypeStruct((batch * ls, heads * vdim), BF16),
        scratch_shapes=[
            pltpu.VMEM((heads * ls, kvl), BF16),
            pltpu.VMEM((heads * ls, rope), BF16),
            pltpu.VMEM((heads * ls, 1), F32),
            pltpu.VMEM((heads * ls, 1), F32),
            pltpu.VMEM((heads * ls, kvl), F32),
        ],
        compiler_params=_params("parallel"),
        name="attn_sample",
    )(q_cat, cache_ckv, cache_kr, ckv, kr, wuk, wuv)


S5_PAIRS = 4


def _s5_body(u_ref, h0r_ref, h0i_ref, ar_ref, ai_ref, wb_ref, wc_ref, d_ref,
             z_ref, sr_ref, si_ref, bu_ref, *, tl):
    t = pl.program_id(2)

    @pl.when(t == 0)
    def _():
        sr_ref[...] = h0r_ref[...]
        si_ref[...] = h0i_ref[...]

    u = u_ref[...].reshape(tl * SUBLANE, LANE)
    bu_ref[...] = _dot(u.astype(BF16), wb_ref[0])
    a_re = [jnp.broadcast_to(ar_ref[0, p:p + 1, :], (SUBLANE, LANE)) for p in range(S5_PAIRS)]
    a_im = [jnp.broadcast_to(ai_ref[0, p:p + 1, :], (SUBLANE, LANE)) for p in range(S5_PAIRS)]

    def step(l, carry):
        row = pl.multiple_of(l * SUBLANE, SUBLANE)
        out = []
        for p in range(S5_PAIRS):
            s_re, s_im = carry[2 * p], carry[2 * p + 1]
            c0 = 2 * p * LANE
            n_re = a_re[p] * s_re - a_im[p] * s_im + bu_ref[pl.ds(row, SUBLANE), c0:c0 + LANE]
            n_im = a_re[p] * s_im + a_im[p] * s_re + bu_ref[pl.ds(row, SUBLANE), c0 + LANE:c0 + 2 * LANE]
            bu_ref[pl.ds(row, SUBLANE), c0:c0 + LANE] = n_re
            bu_ref[pl.ds(row, SUBLANE), c0 + LANE:c0 + 2 * LANE] = n_im
            out += [n_re, n_im]
        return tuple(out)

    init = []
    for p in range(S5_PAIRS):
        init += [sr_ref[:, p * LANE:(p + 1) * LANE], si_ref[:, p * LANE:(p + 1) * LANE]]
    fin = lax.fori_loop(0, tl, step, tuple(init), unroll=4)
    for p in range(S5_PAIRS):
        sr_ref[:, p * LANE:(p + 1) * LANE] = fin[2 * p]
        si_ref[:, p * LANE:(p + 1) * LANE] = fin[2 * p + 1]
    y = _dot(bu_ref[...].astype(BF16), wc_ref[0]) + d_ref[...] * u
    z_ref[...] = jax.nn.gelu(y).reshape(tl, SUBLANE, LANE)


def _s5(u_tm, h0_re, h0_im, a_re, a_im, wb, wc, d_skip):
    l, b, w = u_tm.shape
    nblk = w // LANE
    sw = h0_re.shape[1] // nblk
    tl = _pick(l, (256, 128, 64))
    body = functools.partial(_s5_body, tl=tl)
    return pl.pallas_call(
        body,
        grid=(b // SUBLANE, nblk, l // tl),
        in_specs=[
            pl.BlockSpec((tl, SUBLANE, LANE), lambda g, q, t: (t, g, q)),
            pl.BlockSpec((SUBLANE, sw), lambda g, q, t: (g, q)),
            pl.BlockSpec((SUBLANE, sw), lambda g, q, t: (g, q)),
            pl.BlockSpec((1, S5_PAIRS, LANE), lambda g, q, t: (q, 0, 0)),
            pl.BlockSpec((1, S5_PAIRS, LANE), lambda g, q, t: (q, 0, 0)),
            pl.BlockSpec((1, LANE, 2 * sw), lambda g, q, t: (q, 0, 0)),
            pl.BlockSpec((1, 2 * sw, LANE), lambda g, q, t: (q, 0, 0)),
            pl.BlockSpec((1, LANE), lambda g, q, t: (0, q)),
        ],
        out_specs=[
            pl.BlockSpec((tl, SUBLANE, LANE), lambda g, q, t: (t, g, q)),
            pl.BlockSpec((SUBLANE, sw), lambda g, q, t: (g, q)),
            pl.BlockSpec((SUBLANE, sw), lambda g, q, t: (g, q)),
        ],
        out_shape=[
            jax.ShapeDtypeStruct((l, b, w), F32),
            jax.ShapeDtypeStruct(h0_re.shape, F32),
            jax.ShapeDtypeStruct(h0_im.shape, F32),
        ],
        scratch_shapes=[pltpu.VMEM((tl * SUBLANE, 2 * sw), F32)],
        compiler_params=_params("parallel", "parallel", "arbitrary"),
        name="s5",
    )(u_tm, h0_re, h0_im, a_re, a_im, wb, wc, d_skip)


def _s5_weights(a_re, a_im, log_dt, b_re, b_im, c_re, c_im):
    g, p = a_re.shape
    grp = b_re.shape[2]
    gpb = LANE // grp
    nblk = g // gpb
    a_re, a_im = a_re.astype(F32), a_im.astype(F32)
    dt = jnp.exp(log_dt.astype(F32))[:, None]
    mag = jnp.exp(a_re * dt)
    ab_re, ab_im = mag * jnp.cos(a_im * dt), mag * jnp.sin(a_im * dt)
    den = a_re * a_re + a_im * a_im
    q_re = ((ab_re - 1.0) * a_re + ab_im * a_im) / den
    q_im = (ab_im * a_re - (ab_re - 1.0) * a_im) / den
    b_re, b_im = b_re.astype(F32), b_im.astype(F32)
    bb_re = q_re[..., None] * b_re - q_im[..., None] * b_im
    bb_im = q_re[..., None] * b_im + q_im[..., None] * b_re
    eye = jnp.eye(gpb, dtype=F32)

    def blockdiag_in(m):
        m = m.reshape(nblk, gpb, p, grp)
        return jnp.einsum("qjpc,jk->qjckp", m, eye).reshape(nblk, gpb * grp, gpb * p)

    def blockdiag_out(m):
        m = m.reshape(nblk, gpb, grp, p)
        return jnp.einsum("qjcp,jk->qjpkc", m, eye).reshape(nblk, gpb * p, gpb * grp)

    def interleave(re, im, axis):
        shp = list(re.shape)
        n = shp[axis] // LANE
        shp[axis:axis + 1] = [n, LANE]
        st = jnp.stack([re.reshape(shp), im.reshape(shp)], axis=axis + 1)
        shp[axis:axis + 2] = [2 * n * LANE]
        return st.reshape(shp)

    wb = interleave(blockdiag_in(bb_re), blockdiag_in(bb_im), 2).astype(BF16)
    wc = interleave(blockdiag_out(c_re.astype(F32)), blockdiag_out(-c_im.astype(F32)), 1).astype(BF16)
    pairs = g * p // LANE
    ar = ab_re.reshape(pairs // S5_PAIRS, S5_PAIRS, LANE)
    ai = ab_im.reshape(pairs // S5_PAIRS, S5_PAIRS, LANE)
    return ar, ai, wb, wc


def _pool_body(x_ref, g_ref, hist_ref, w_ref, sc_ref, o_ref, tail_ref, ext_ref, lva_ref, lvb_ref,
               *, tl, nt, pos0, gc):
    t = pl.program_id(1)
    front = SUBLANE
    cur0 = front + POOL_PAD
    nrow = cur0 + tl

    @pl.when(t == 0)
    def _():
        ext_ref[0:front, :] = jnp.zeros((front, ext_ref.shape[1]), F32)
        lva_ref[0:front, :] = jnp.zeros((front, gc), F32)
        lvb_ref[0:front, :] = jnp.zeros((front, gc), F32)
        ext_ref[front:cur0, :] = hist_ref[...]

    @pl.when(t > 0)
    def _():
        ext_ref[front:cur0, :] = ext_ref[tl + front:tl + cur0, :]

    x = x_ref[...]
    ext_ref[cur0:nrow, :] = _rms(x, g_ref[...])
    pos = pos0 + t * tl + lax.broadcasted_iota(jnp.int32, (tl, 1), 0)
    for gi, win in enumerate(POOL_WINDOWS):
        cols = slice(gi * gc, (gi + 1) * gc)
        cur = ext_ref[cur0:nrow, cols]
        src, dst, span = None, lva_ref, 1
        while 2 * span < win:
            lo = ext_ref[front - span:nrow - span, cols] if src is None else src[front - span:nrow - span, :]
            hi = ext_ref[front:nrow, cols] if src is None else src[front:nrow, :]
            dst[front:nrow, :] = hi + lo
            src, dst = dst, (lvb_ref if dst is lva_ref else lva_ref)
            span *= 2
        if src is None:
            wsum = cur + ext_ref[cur0 - span:nrow - span, cols]
        else:
            wsum = src[cur0:nrow, :] + src[cur0 - span:nrow - span, :]
        count = jnp.minimum(pos + 1, win).astype(F32)
        delta = (wsum / count - cur).astype(BF16)
        o_ref[:, cols] = x[:, cols] + _dot(delta, w_ref[gi]) * sc_ref[:, cols]

    @pl.when(t == nt - 1)
    def _():
        tail_ref[...] = ext_ref[tl + front:tl + cur0, :]


def _pool(x, g, hist, w, scale, *, row0, batch, seq, pos0):
    t, d = x.shape
    gc = d // len(POOL_WINDOWS)
    tl = _pick(seq, (256, 128, 64))
    nt = seq // tl
    blk0 = row0 // tl
    body = functools.partial(_pool_body, tl=tl, nt=nt, pos0=pos0, gc=gc)
    return pl.pallas_call(
        body,
        grid=(batch, nt),
        in_specs=[
            pl.BlockSpec((tl, d), lambda b, i: (blk0 + b * nt + i, 0)),
            pl.BlockSpec((1, d), lambda b, i: (0, 0)),
            pl.BlockSpec((None, POOL_PAD, d), lambda b, i: (b, 0, 0)),
            pl.BlockSpec(w.shape, lambda b, i: (0, 0, 0)),
            pl.BlockSpec((1, d), lambda b, i: (0, 0)),
        ],
        out_specs=[
            pl.BlockSpec((tl, d), lambda b, i: (blk0 + b * nt + i, 0)),
            pl.BlockSpec((None, POOL_PAD, d), lambda b, i: (b, 0, 0)),
        ],
        out_shape=[
            jax.ShapeDtypeStruct((t, d), F32),
            jax.ShapeDtypeStruct((batch, POOL_PAD, d), F32),
        ],
        scratch_shapes=[pltpu.VMEM((SUBLANE + POOL_PAD + tl, d), F32),
                        pltpu.VMEM((SUBLANE + POOL_PAD + tl, gc), F32),
                        pltpu.VMEM((SUBLANE + POOL_PAD + tl, gc), F32)],
        input_output_aliases={0: 0},
        compiler_params=_params("parallel", "arbitrary"),
        name="pool",
    )(x, g, hist, w, scale)


def _rot_cols(w):
    half = w.shape[-1] // 2
    return jnp.concatenate([-w[..., half:], w[..., :half]], axis=-1)


def _pad_cols(w, n):
    return jnp.pad(w, [(0, 0)] * (w.ndim - 1) + [(0, n - w.shape[-1])])


def kernel(x_prompt, x_sample, cache_mla_ckv, cache_mla_krope, state_s5_re, state_s5_im, cache_pool, ffn1_g, ffn1_w1, ffn1_w3, ffn1_w2, mix_g, ffn2_g, ffn2_w1, ffn2_w3, ffn2_w2, final_g, ab_w_in, mla_g_q, mla_g_kv, mla_w_uq, mla_w_uk, mla_w_uv, s5_a_re, s5_a_im, s5_log_dt, s5_b_re, s5_b_im, s5_c_re, s5_c_im, s5_d, s5_w_glu, ab_w_o, pool_w, pool_scale):
    bp, seq, d = x_prompt.shape
    bs, ls, _ = x_sample.shape
    tp, ts = bp * seq, bs * ls
    depth = ffn1_g.shape[0]
    past = cache_mla_ckv.shape[2]
    q_lora, heads, qk_head = mla_w_uq.shape[1:]
    kvl, _, nope = mla_w_uk.shape[1:]
    vdim = mla_w_uv.shape[3]
    rope = qk_head - nope
    s5w = s5_d.shape[1]
    groups, nstate = s5_a_re.shape[1:]
    hist_rows = cache_pool.shape[2]
    scale = qk_head ** -0.5
    assert nope == LANE and vdim == LANE and 2 * rope == LANE and kvl + 2 * LANE <= q_lora
    assert bp % SUBLANE == 0 and bs % SUBLANE == 0 and hist_rows == POOL_PAD - 1

    pos = jnp.concatenate([jnp.tile(jnp.arange(seq), bp), past + jnp.tile(jnp.arange(ls), bs)])
    inv = 1.0 / (ROPE_THETA ** (jnp.arange(0, rope, 2, dtype=F32) / rope))
    ang = pos.astype(F32)[:, None] * inv[None, :]
    cos = _pad_cols(jnp.tile(jnp.cos(ang), (1, 2)), LANE)
    sin = _pad_cols(jnp.tile(jnp.sin(ang), (1, 2)), LANE)

    row = lambda v: v.reshape(1, -1).astype(F32)
    fg = row(final_g)

    ffns = []
    for layer in range(depth):
        ffns.append((ffn1_g[layer], (ffn1_w1, ffn1_w3, ffn1_w2), layer))
        ffns.append((ffn2_g[layer], (ffn2_w1, ffn2_w3, ffn2_w2), layer))
    w_next = tuple(_cast(ws, ffns[0][2]) for ws in ffns[0][1])

    def ffn_pair(k, xp, xs):
        nonlocal w_next
        g = row(ffns[k][0])
        w1, w3, w2 = w_next
        ride = (*ffns[k + 1][1], ffns[k + 1][2]) if k + 1 < len(ffns) else None
        act_p = _ffn_up(xp, g, w1, w3, row0=0, nrows=tp)
        act_s = _ffn_up(xs, g, w1, w3, row0=0, nrows=ts)
        yp, w_next = _ffn_down(act_p, w2, xp, ride=ride)
        ys, _ = _ffn_down(act_s, w2, xs)
        return yp, ys

    ckv_out, kr_out, s5re_out, s5im_out, pool_out = [], [], [], [], []
    x_p, x_s = x_prompt.reshape(tp, d), x_sample.reshape(ts, d)
    for layer in range(depth):
        x_p, x_s = ffn_pair(2 * layer, x_p, x_s)
        if layer % 2 == 0:
            i = layer // 2
            w_in = ab_w_in[i]
            w_kr = w_in[:, q_lora + kvl:q_lora + kvl + rope]
            w_mid = jnp.concatenate([w_in[:, q_lora:q_lora + kvl], _pad_cols(w_kr, LANE),
                                     _pad_cols(_rot_cols(w_kr), LANE)], axis=1)
            w_proj = jnp.concatenate([w_in[:, :q_lora], _pad_cols(w_mid, q_lora),
                                      w_in[:, q_lora + kvl + rope:]], axis=1).astype(BF16)
            g_mix = row(mix_g[layer])
            lat_p, u_p = _inproj(x_p, g_mix, w_proj, n_lat=2 * q_lora, row0=0, nrows=tp)
            lat_s, u_s = _inproj(x_s, g_mix, w_proj, n_lat=2 * q_lora, row0=0, nrows=ts)
            g_kv = row(mla_g_kv[i])
            cos_p, sin_p, cos_s, sin_s = cos[:tp], sin[:tp], cos[tp:], sin[tp:]
            ckv_p, kr_p = _kvpost(lat_p, g_kv, cos_p, sin_p, q_lora=q_lora, kv=kvl, rope=rope)
            ckv_s, kr_s = _kvpost(lat_s, g_kv, cos_s, sin_s, q_lora=q_lora, kv=kvl, rope=rope)

            w_uq = mla_w_uq[i]
            wq = jnp.concatenate([w_uq[..., :nope], _pad_cols(w_uq[..., nope:], LANE)], axis=-1)
            wq = wq.reshape(q_lora, -1).astype(BF16)
            wrot = _pad_cols(_rot_cols(w_uq[..., nope:]), LANE).reshape(q_lora, -1).astype(BF16)
            g_q = row(mla_g_q[i])
            q_p = _qproj(lat_p, g_q, wq, wrot, cos_p, sin_p, q_lora=q_lora, heads=heads, nope=nope)
            q_s = _qproj(lat_s, g_q, wq, wrot, cos_s, sin_s, q_lora=q_lora, heads=heads, nope=nope)

            wuk = mla_w_uk[i].reshape(kvl, heads * nope).astype(BF16)
            wuv = mla_w_uv[i].reshape(kvl, heads * vdim).astype(BF16)
            kv_exp = _mm(ckv_p, jnp.concatenate([wuk, wuv], axis=1), out_dtype=BF16, name="kv_expand")
            oa_p = _attn_prompt(q_p, kv_exp, kr_p, batch=bp, seq=seq, heads=heads, nope=nope,
                                rope=rope, vdim=vdim, scale=scale)
            oa_s = _attn_sample(q_s, cache_mla_ckv, cache_mla_krope, ckv_s, kr_s, wuk, wuv, layer=i,
                                batch=bs, ls=ls, heads=heads, nope=nope, rope=rope, vdim=vdim, scale=scale)

            ar, ai, wb, wc = _s5_weights(s5_a_re[i], s5_a_im[i], s5_log_dt[i], s5_b_re[i], s5_b_im[i],
                                         s5_c_re[i], s5_c_im[i])
            u_p = u_p.reshape(bp, seq, s5w).swapaxes(0, 1)
            u_s = u_s.reshape(bs, ls, s5w).swapaxes(0, 1)
            zeros = jnp.zeros((bp, groups * nstate), F32)
            d_row = row(s5_d[i])
            z_p, pre, pim = _s5(u_p, zeros, zeros, ar, ai, wb, wc, d_row)
            z_s, sre, sim = _s5(u_s, state_s5_re[i].reshape(bs, -1).astype(F32),
                                state_s5_im[i].reshape(bs, -1).astype(F32), ar, ai, wb, wc, d_row)
            wglu = s5_w_glu[i].astype(BF16)
            z_p = z_p.reshape(tp, s5w)
            z_s = z_s.reshape(ts, s5w)
            ob_p = _mm(z_p, wglu, out_dtype=BF16, epilogue="glu", extra=z_p, name="s5_glu")
            ob_s = _mm(z_s, wglu, out_dtype=BF16, epilogue="glu", extra=z_s, name="s5_glu")
            ob_p = ob_p.reshape(seq, bp, s5w).swapaxes(0, 1).reshape(tp, s5w)
            ob_s = ob_s.reshape(ls, bs, s5w).swapaxes(0, 1).reshape(ts, s5w)
            w_o = ab_w_o[i].astype(BF16)
            x_p = _outproj(oa_p, ob_p, w_o, x_p)
            x_s = _outproj(oa_s, ob_s, w_o, x_s)

            ckv_out.append((ckv_p.reshape(bp, seq, kvl), ckv_s.reshape(bs, ls, kvl)))
            kr_out.append((kr_p.reshape(bp, seq, rope), kr_s.reshape(bs, ls, rope)))
            s5re_out.append((pre.reshape(bp, groups, nstate), sre.reshape(bs, groups, nstate)))
            s5im_out.append((pim.reshape(bp, groups, nstate), sim.reshape(bs, groups, nstate)))
        else:
            j = layer // 2
            wp = pool_w[j].astype(BF16)
            g_mix, sc = row(mix_g[layer]), row(pool_scale[j])
            hist_p = jnp.zeros((bp, POOL_PAD, d), F32)
            hist_s = jnp.pad(cache_pool[j].astype(F32), ((0, 0), (1, 0), (0, 0)))
            x_p, tail_p = _pool(x_p, g_mix, hist_p, wp, sc, row0=0, batch=bp, seq=seq, pos0=0)
            x_s, tail_s = _pool(x_s, g_mix, hist_s, wp, sc, row0=0, batch=bs, seq=ls, pos0=past)
            pool_out.append((tail_p[:, 1:], tail_s[:, 1:]))
        x_p, x_s = ffn_pair(2 * layer + 1, x_p, x_s)

    y_p = _final_norm(x_p, fg, row0=0, nrows=tp)
    y_s = _final_norm(x_s, fg, row0=0, nrows=ts)
    stack = lambda items, k: jnp.stack([it[k] for it in items])
    return (y_p.reshape(bp, seq, d), y_s.reshape(bs, ls, d),
            stack(ckv_out, 0), stack(kr_out, 0), stack(s5re_out, 0), stack(s5im_out, 0), stack(pool_out, 0),
            stack(ckv_out, 1), stack(kr_out, 1), stack(s5re_out, 1), stack(s5im_out, 1), stack(pool_out, 1))
```
